```python
import math
import jax, jax.numpy as jnp
from jax import lax
import numpy as np

D_MODEL = 1024
BATCH = 4
SEQ = 8192
DEPTH = 1

CHUNK = 64
Q_BLOCK = 128
CONV_WIDTH = 3
D_CONV = D_MODEL // 2
N_HEADS_DIFF = 4
DH_DIFF = 64
DV_DIFF = 2 * DH_DIFF
D_ATTN = N_HEADS_DIFF * DV_DIFF
D_FF = ((8 * D_MODEL // 3 + 255) // 256) * 256
EPS = 1e-6
NEG_INF = -1e30
COLS_CONV = 3 * D_CONV
COLS_ATTN = 3 * D_ATTN
COLS_GATES = 2 * D_MODEL
D_IN_TOTAL = COLS_CONV + COLS_ATTN + COLS_GATES

kernel_name = "hybrid_shortconv_diffattn_gated_block"


def rms_norm(x, g):
    xf = x.astype(jnp.float32)
    y = xf * lax.rsqrt(jnp.mean(xf * xf, axis=-1, keepdims=True) + EPS)
    return y.astype(x.dtype) * g


def alibi_slopes(n_heads):
    return jnp.asarray([2.0 ** (-8.0 * (i + 1) / n_heads) for i in range(n_heads)], dtype=jnp.float32)


def lambda_init_for(layer_idx):
    return 0.8 - 0.6 * math.exp(-0.3 * layer_idx)


def causal_depthwise_conv(u, w):
    S = u.shape[1]
    k = w.shape[0]
    up = jnp.pad(u, ((0, 0), (k - 1, 0), (0, 0)))
    out = up[:, 0:S] * w[0]
    for i in range(1, k):
        out = out + up[:, i:i + S] * w[i]
    return out


def short_conv_branch(u, gb, gc, conv_w, w_out_a):
    z = causal_depthwise_conv(gc * u, conv_w)
    return (gb * z) @ w_out_a


def diff_attention_branch(q, k, v, lam, subln_g, lambda_init, w_out_b):
    Bsz, S = q.shape[0], q.shape[1]
    nblk = S // Q_BLOCK
    scale = 1.0 / math.sqrt(DH_DIFF)
    slopes = alibi_slopes(N_HEADS_DIFF)
    kpos = jnp.arange(S, dtype=jnp.int32)
    kchunk = kpos // CHUNK
    q_blocks = jnp.moveaxis(q.reshape(Bsz, nblk, Q_BLOCK, N_HEADS_DIFF, 2, DH_DIFF), 1, 0)
    qpos_blocks = kpos.reshape(nblk, Q_BLOCK)

    def one_block(args):
        qb, qpos = args
        s = jnp.einsum('bqhcd,bkhcd->bhcqk', qb.astype(jnp.float32), k.astype(jnp.float32)) * scale
        dist = jnp.abs(qpos[:, None] - kpos[None, :]).astype(jnp.float32)
        bias = -slopes[:, None, None] * dist[None]
        allowed = kchunk[None, :] <= (qpos // CHUNK)[:, None]
        s = jnp.where(allowed[None, None, None], s + bias[None, :, None], NEG_INF)
        p = jax.nn.softmax(s, axis=-1)
        a = (p[:, :, 0] - lam * p[:, :, 1]).astype(v.dtype)
        return jnp.einsum('bhqk,bkhe->bqhe', a, v)

    o = lax.map(one_block, (q_blocks, qpos_blocks))
    o = jnp.moveaxis(o, 0, 1).reshape(Bsz, S, N_HEADS_DIFF, DV_DIFF)
    o = rms_norm(o, subln_g) * (1.0 - lambda_init)
    return o.reshape(Bsz, S, D_ATTN) @ w_out_b


def swiglu(h, w_gate, w_up, w_down):
    return (jax.nn.silu(h @ w_gate) * (h @ w_up)) @ w_down


def setup_inputs(seed: int = 0) -> dict:
    key = jax.random.key(seed)
    ks = jax.random.split(key, 24)
    n = jax.random.normal
    L, D = DEPTH, D_MODEL
    return {
        "x": n(ks[0], (BATCH, SEQ, D), jnp.float32),
        "c": n(ks[1], (BATCH, D), jnp.float32),
        "w_ada": n(ks[2], (L, D, 6 * D), jnp.float32) * (0.2 * D ** -0.5),
        "b_ada": n(ks[3], (L, 6 * D), jnp.float32) * 0.02,
        "g_mix": 1.0 + 0.05 * n(ks[4], (L, D), jnp.float32),
        "w_in": n(ks[5], (L, D, D_IN_TOTAL), jnp.float32) * D ** -0.5,
        "conv_w": n(ks[6], (L, CONV_WIDTH, D_CONV), jnp.float32) * CONV_WIDTH ** -0.5,
        "w_out_a": n(ks[7], (L, D_CONV, D), jnp.float32) * D_CONV ** -0.5,
        "lambda_q1": n(ks[8], (L, DH_DIFF), jnp.float32) * 0.1,
        "lambda_k1": n(ks[9], (L, DH_DIFF), jnp.float32) * 0.1,
        "lambda_q2": n(ks[10], (L, DH_DIFF), jnp.float32) * 0.1,
        "lambda_k2": n(ks[11], (L, DH_DIFF), jnp.float32) * 0.1,
        "subln_g": 1.0 + 0.05 * n(ks[12], (L, DV_DIFF), jnp.float32),
        "w_out_b": n(ks[13], (L, D_ATTN, D), jnp.float32) * D_ATTN ** -0.5,
        "w_out": n(ks[14], (L, D, D), jnp.float32) * D ** -0.5,
        "g_ffn": 1.0 + 0.05 * n(ks[15], (L, D), jnp.float32),
        "w_gate": n(ks[16], (L, D, D_FF), jnp.float32) * D ** -0.5,
        "w_up": n(ks[17], (L, D, D_FF), jnp.float32) * D ** -0.5,
        "w_down": n(ks[18], (L, D_FF, D), jnp.float32) * D_FF ** -0.5,
        "g_final": 1.0 + 0.05 * n(ks[19], (D,), jnp.float32),
    }


def reference(x, c, w_ada, b_ada, g_mix, w_in, conv_w, w_out_a, lambda_q1, lambda_k1,
              lambda_q2, lambda_k2, subln_g, w_out_b, w_out, g_ffn, w_gate, w_up, w_down,
              g_final):
    Bsz, S, D = x.shape
    for l in range(DEPTH):
        lambda_init = lambda_init_for(l + 1)
        mod = jax.nn.silu(c) @ w_ada[l] + b_ada[l]
        sh_m, sc_m, gt_m, sh_f, sc_f, gt_f = [m[:, None, :] for m in jnp.split(mod, 6, axis=-1)]

        h = rms_norm(x, g_mix[l]) * (1.0 + sc_m) + sh_m
        proj = h @ w_in[l]
        p_conv = proj[..., :COLS_CONV]
        p_attn = proj[..., COLS_CONV:COLS_CONV + COLS_ATTN]
        p_gate = proj[..., COLS_CONV + COLS_ATTN:]

        u, gb, gc = jnp.split(p_conv, 3, axis=-1)
        y_a = short_conv_branch(u, gb, gc, conv_w[l], w_out_a[l])

        q, k, v = jnp.split(p_attn, 3, axis=-1)
        q = q.reshape(Bsz, S, N_HEADS_DIFF, 2, DH_DIFF)
        k = k.reshape(Bsz, S, N_HEADS_DIFF, 2, DH_DIFF)
        v = v.reshape(Bsz, S, N_HEADS_DIFF, DV_DIFF)
        lam = (jnp.exp(jnp.sum(lambda_q1[l].astype(jnp.float32) * lambda_k1[l].astype(jnp.float32)))
               - jnp.exp(jnp.sum(lambda_q2[l].astype(jnp.float32) * lambda_k2[l].astype(jnp.float32)))
               + lambda_init)
        y_b = diff_attention_branch(q, k, v, lam, subln_g[l], lambda_init, w_out_b[l])

        g_a, g_b = jnp.split(p_gate, 2, axis=-1)
        merged = jax.nn.sigmoid(g_a) * y_a + jax.nn.sigmoid(g_b) * y_b
        x = x + gt_m * (merged @ w_out[l])

        h2 = rms_norm(x, g_ffn[l]) * (1.0 + sc_f) + sh_f
        x = x + gt_f * swiglu(h2, w_gate[l], w_up[l], w_down[l])
    return rms_norm(x, g_final)
```

```python
import functools
import math

import jax
import jax.numpy as jnp
import numpy as np
from jax import lax
from jax.experimental import pallas as pl
from jax.experimental.pallas import tpu as pltpu

CHUNK = 64
CONV_WIDTH = 3
N_HEADS = 4
DH = 64
DV = 2 * DH
EPS = 1e-6
NEG_INF = -1e30
LAMBDA_INIT = 0.8 - 0.6 * math.exp(-0.3 * 1)
ALIBI_SLOPES = tuple(2.0 ** (-8.0 * (i + 1) / N_HEADS) for i in range(N_HEADS))

ADA_BLOCK_N = 1536
PRE_BLOCK_M = 512
POST_BLOCK_M = 256
ATTN_BLOCK = 256
V7X_VMEM_LIMIT = 56 * 1024 * 1024

BF16 = jnp.bfloat16
F32 = jnp.float32


def _dot(a, b):
    return jnp.dot(a, b, preferred_element_type=F32)


def _sigmoid(x):
    return 1.0 / (1.0 + jnp.exp(-x))


def _ada_kernel(c_ref, w_ref, b_ref, o_ref):
    c = c_ref[...]
    sc = (c * _sigmoid(c)).astype(BF16)
    o_ref[...] = _dot(sc, w_ref[...].astype(BF16)) + b_ref[...]


def _ada_call(c_pad, w_ada, b_ada):
    rows, d = c_pad.shape
    n = w_ada.shape[1]
    return pl.pallas_call(
        _ada_kernel,
        grid=(n // ADA_BLOCK_N,),
        in_specs=[
            pl.BlockSpec((rows, d), lambda j: (0, 0)),
            pl.BlockSpec((d, ADA_BLOCK_N), lambda j: (0, j)),
            pl.BlockSpec((1, ADA_BLOCK_N), lambda j: (0, j)),
        ],
        out_specs=pl.BlockSpec((rows, ADA_BLOCK_N), lambda j: (0, j)),
        out_shape=jax.ShapeDtypeStruct((rows, n), F32),
        compiler_params=pltpu.CompilerParams(
            dimension_semantics=("arbitrary",), vmem_limit_bytes=V7X_VMEM_LIMIT),
        name="ada_mod",
    )(c_pad, w_ada, b_ada)


def _pre_kernel(x_ref, mod_ref, g_ref, win_ref, cw_ref, woa_ref,
                ma_ref, sgb_ref, qt_ref, k_ref, vt_ref, carry_ref,
                *, tiles_per_seq, d_conv, d_attn, d_model):
    i = pl.program_id(0)
    tm = x_ref.shape[0]

    x = x_ref[...]
    mod = mod_ref[0]
    xn = x * lax.rsqrt(jnp.mean(x * x, axis=-1, keepdims=True) + EPS)
    h = (xn * g_ref[...]) * (1.0 + mod[1:2, :]) + mod[0:1, :]
    hb = h.astype(BF16)

    c0, c1 = 0, 3 * d_conv
    pc = _dot(hb, win_ref[:, c0:c1])
    u = pc[:, 0:d_conv]
    gb = pc[:, d_conv:2 * d_conv]
    gc = pc[:, 2 * d_conv:3 * d_conv]
    cu = gc * u

    @pl.when(i % tiles_per_seq == 0)
    def _():
        carry_ref[...] = jnp.zeros_like(carry_ref)

    ext = jnp.concatenate([carry_ref[...], cu], axis=0)
    prev1 = pltpu.roll(ext, 1, axis=0)[8:, :]
    prev2 = pltpu.roll(ext, 2, axis=0)[8:, :]
    carry_ref[...] = cu[tm - 8:, :]
    cw = cw_ref[...]
    z = cw[0:1, :] * prev2 + cw[1:2, :] * prev1 + cw[2:3, :] * cu
    y_a = _dot((gb * z).astype(BF16), woa_ref[...])

    g0 = 3 * d_conv + 3 * d_attn
    pg = _dot(hb, win_ref[:, g0:g0 + 2 * d_model])
    ma_ref[...] = (_sigmoid(pg[:, 0:d_model]) * y_a).astype(BF16)
    sgb_ref[...] = _sigmoid(pg[:, d_model:2 * d_model]).astype(BF16)

    a0 = 3 * d_conv
    pa = _dot(hb, win_ref[:, a0:a0 + 3 * d_attn])
    q = pa[:, 0:d_attn] * (1.0 / math.sqrt(DH))
    qt_ref[0] = q.T.astype(BF16)
    k_ref[...] = pa[:, d_attn:2 * d_attn].astype(BF16)
    v = pa[:, 2 * d_attn:3 * d_attn]
    for s in range(tm // ATTN_BLOCK):
        vt_ref[0, s] = v[s * ATTN_BLOCK:(s + 1) * ATTN_BLOCK, :].T.astype(BF16)


def _pre_call(x2, mod3, g_mix, w_in_b, conv_w, w_out_a_b, *, batch, seq):
    t, d = x2.shape
    d_conv = conv_w.shape[1]
    d_attn = N_HEADS * DV
    tm = PRE_BLOCK_M
    tps = seq // tm
    nkb = seq // ATTN_BLOCK
    kern = functools.partial(_pre_kernel, tiles_per_seq=tps, d_conv=d_conv,
                             d_attn=d_attn, d_model=d)
    full = lambda a: pl.BlockSpec(a.shape, lambda i: (0,) * a.ndim)
    return pl.pallas_call(
        kern,
        grid=(t // tm,),
        in_specs=[
            pl.BlockSpec((tm, d), lambda i: (i, 0)),
            pl.BlockSpec((1,) + mod3.shape[1:], lambda i: (i // tps, 0, 0)),
            full(g_mix), full(w_in_b), full(conv_w), full(w_out_a_b),
        ],
        out_specs=[
            pl.BlockSpec((tm, d), lambda i: (i, 0)),
            pl.BlockSpec((tm, d), lambda i: (i, 0)),
            pl.BlockSpec((1, d_attn, tm), lambda i: (i // tps, 0, i % tps)),
            pl.BlockSpec((tm, d_attn), lambda i: (i, 0)),
            pl.BlockSpec((1, tm // ATTN_BLOCK, d_attn, ATTN_BLOCK),
                         lambda i: (i // tps, i % tps, 0, 0)),
        ],
        out_shape=[
            jax.ShapeDtypeStruct((t, d), BF16),
            jax.ShapeDtypeStruct((t, d), BF16),
            jax.ShapeDtypeStruct((batch, d_attn, seq), BF16),
            jax.ShapeDtypeStruct((t, d_attn), BF16),
            jax.ShapeDtypeStruct((batch, nkb, d_attn, ATTN_BLOCK), BF16),
        ],
        scratch_shapes=[pltpu.VMEM((8, d_conv), F32)],
        compiler_params=pltpu.CompilerParams(
            dimension_semantics=("arbitrary",), vmem_limit_bytes=V7X_VMEM_LIMIT),
        name="mixer_in",
    )(x2, mod3, g_mix, w_in_b, conv_w, w_out_a_b)


def _attn_kernel(slope_ref, qt_ref, k_ref, vt_ref, bias_ref, lam_ref, g_ref,
                 o_ref, acc1_ref, acc2_ref):
    h = pl.program_id(1)
    qi = pl.program_id(2)
    tb = ATTN_BLOCK
    slope = slope_ref[h]

    qt = qt_ref[0]
    row = lax.broadcasted_iota(jnp.int32, qt.shape, 0)
    zero = jnp.zeros_like(qt)
    q1 = jnp.where(row < DH, qt, zero)
    q2 = jnp.where(row >= DH, qt, zero)

    acc1_ref[...] = jnp.zeros_like(acc1_ref)
    acc2_ref[...] = jnp.zeros_like(acc2_ref)

    def one_map(s, m, l, acc_ref, vb, shift):
        m_new = jnp.maximum(m, jnp.max(s, axis=0, keepdims=True) + shift)
        alpha = jnp.exp(m - m_new)
        p = jnp.exp(s - (m_new - shift))
        l_new = alpha * l + jnp.sum(p, axis=0, keepdims=True)
        acc_ref[...] = alpha * acc_ref[...] + _dot(vb, p.astype(BF16))
        return m_new, l_new

    def step(kj, which, carry):
        m1, l1, m2, l2 = carry
        kb = k_ref[0, pl.ds(pl.multiple_of(kj * tb, tb), tb), :]
        vb = vt_ref[0, kj]
        bias = bias_ref[0, which]
        shift = -slope * ((qi - kj) * tb).astype(F32)
        m1, l1 = one_map(_dot(kb, q1) + bias, m1, l1, acc1_ref, vb, shift)
        m2, l2 = one_map(_dot(kb, q2) + bias, m2, l2, acc2_ref, vb, shift)
        return m1, l1, m2, l2

    m0 = jnp.full((1, tb), NEG_INF, F32)
    l0 = jnp.zeros((1, tb), F32)
    carry = lax.fori_loop(0, qi, lambda kj, c: step(kj, 0, c), (m0, l0, m0, l0))
    m1, l1, m2, l2 = step(qi, 1, carry)

    lp = lam_ref[...]
    lam = (jnp.exp(jnp.sum(lp[0:1, :] * lp[1:2, :], axis=-1, keepdims=True))
           - jnp.exp(jnp.sum(lp[2:3, :] * lp[3:4, :], axis=-1, keepdims=True))
           + LAMBDA_INIT)
    ot = acc1_ref[...] * (1.0 / l1) - (lam * acc2_ref[...]) * (1.0 / l2)
    o = ot.T
    o = o * lax.rsqrt(jnp.mean(o * o, axis=-1, keepdims=True) + EPS)
    o_ref[0] = ((o * g_ref[...]) * (1.0 - LAMBDA_INIT)).astype(BF16)


def _attn_bias_tiles():
    tb = ATTN_BLOCK
    jj = np.arange(tb, dtype=np.int64)[:, None]
    ii = np.arange(tb, dtype=np.int64)[None, :]
    slopes = np.asarray(ALIBI_SLOPES, np.float64)[:, None, None]
    off = -slopes * (ii - jj)[None].astype(np.float64)
    allowed = (jj // CHUNK) <= (ii // CHUNK)
    diag = np.where(allowed[None], -slopes * np.abs(ii - jj)[None], NEG_INF)
    return np.stack([off, diag], axis=1).astype(np.float32)


def _attn_call(qt, k3, vt, lam_params, subln_g, *, batch, seq):
    tb = ATTN_BLOCK
    nb = seq // tb
    d_attn = N_HEADS * DV
    bias = jnp.asarray(_attn_bias_tiles())
    slopes = jnp.asarray(ALIBI_SLOPES, F32)
    return pl.pallas_call(
        _attn_kernel,
        grid=(batch, N_HEADS, nb),
        in_specs=[
            pl.BlockSpec(memory_space=pltpu.SMEM),
            pl.BlockSpec((1, DV, tb), lambda b, h, q: (b, h, q)),
            pl.BlockSpec((1, seq, DV), lambda b, h, q: (b, 0, h)),
            pl.BlockSpec((1, nb, DV, tb), lambda b, h, q: (b, 0, h, 0)),
            pl.BlockSpec((1, 2, tb, tb), lambda b, h, q: (h, 0, 0, 0)),
            pl.BlockSpec(lam_params.shape, lambda b, h, q: (0, 0)),
            pl.BlockSpec(subln_g.shape, lambda b, h, q: (0, 0)),
        ],
        out_specs=pl.BlockSpec((1, tb, DV), lambda b, h, q: (b, q, h)),
        out_shape=jax.ShapeDtypeStruct((batch, seq, d_attn), BF16),
        scratch_shapes=[pltpu.VMEM((DV, tb), F32), pltpu.VMEM((DV, tb), F32)],
        compiler_params=pltpu.CompilerParams(
            dimension_semantics=("arbitrary", "arbitrary", "arbitrary"),
            vmem_limit_bytes=V7X_VMEM_LIMIT),
        name="diff_attn",
    )(slopes, qt, k3, vt, bias, lam_params, subln_g)


def _post_kernel(x_ref, o_ref, ma_ref, sgb_ref, mod_ref, gf_ref, gfin_ref,
                 wob_ref, wo_ref, wg_ref, wu_ref, wd_ref, y_ref):
    mod = mod_ref[0]
    y_b = _dot(o_ref[...], wob_ref[...])
    merged = ma_ref[...].astype(F32) + sgb_ref[...].astype(F32) * y_b
    x1 = x_ref[...] + mod[2:3, :] * _dot(merged.astype(BF16), wo_ref[...])

    xn = x1 * lax.rsqrt(jnp.mean(x1 * x1, axis=-1, keepdims=True) + EPS)
    h2 = ((xn * gf_ref[...]) * (1.0 + mod[4:5, :]) + mod[3:4, :]).astype(BF16)
    gate = _dot(h2, wg_ref[...])
    up = _dot(h2, wu_ref[...])
    act = ((gate * _sigmoid(gate)) * up).astype(BF16)
    x2 = x1 + mod[5:6, :] * _dot(act, wd_ref[...])

    y_ref[...] = (x2 * lax.rsqrt(jnp.mean(x2 * x2, axis=-1, keepdims=True) + EPS)) * gfin_ref[...]


def _post_call(x2, o2, ma, sgb, mod3, g_ffn, g_final, wob, wo, wg, wu, wd, *, seq):
    t, d = x2.shape
    tm = POST_BLOCK_M
    tps = seq // tm
    full = lambda a: pl.BlockSpec(a.shape, lambda i: (0,) * a.ndim)
    row = lambda a: pl.BlockSpec((tm, a.shape[1]), lambda i: (i, 0))
    return pl.pallas_call(
        _post_kernel,
        grid=(t // tm,),
        in_specs=[
            row(x2), row(o2), row(ma), row(sgb),
            pl.BlockSpec((1,) + mod3.shape[1:], lambda i: (i // tps, 0, 0)),
            full(g_ffn), full(g_final), full(wob), full(wo), full(wg), full(wu), full(wd),
        ],
        out_specs=pl.BlockSpec((tm, d), lambda i: (i, 0)),
        out_shape=jax.ShapeDtypeStruct((t, d), F32),
        compiler_params=pltpu.CompilerParams(
            dimension_semantics=("arbitrary",), vmem_limit_bytes=V7X_VMEM_LIMIT),
        name="mixer_out_ffn",
    )(x2, o2, ma, sgb, mod3, g_ffn, g_final, wob, wo, wg, wu, wd)


def kernel(x, c, w_ada, b_ada, g_mix, w_in, conv_w, w_out_a, lambda_q1, lambda_k1, lambda_q2,
           lambda_k2, subln_g, w_out_b, w_out, g_ffn, w_gate, w_up, w_down, g_final):
    batch, seq, d = x.shape
    depth = w_ada.shape[0]
    assert depth == 1 and seq % PRE_BLOCK_M == 0 and seq % POST_BLOCK_M == 0
    assert PRE_BLOCK_M % ATTN_BLOCK == 0 and ATTN_BLOCK % CHUNK == 0
    l = 0
    x2 = x.reshape(batch * seq, d)

    c_pad = jnp.pad(c, ((0, 8 - batch), (0, 0)))
    mod = _ada_call(c_pad, w_ada[l], b_ada[l].reshape(1, -1))
    mod3 = mod[:batch].reshape(batch, 6, d)

    ma, sgb, qt, k2, vt = _pre_call(
        x2, mod3, g_mix[l].reshape(1, d), w_in[l].astype(BF16), conv_w[l],
        w_out_a[l].astype(BF16), batch=batch, seq=seq)

    lam_params = jnp.stack([lambda_q1[l], lambda_k1[l], lambda_q2[l], lambda_k2[l]], axis=0)
    o = _attn_call(qt, k2.reshape(batch, seq, -1), vt, lam_params,
                   subln_g[l].reshape(1, DV), batch=batch, seq=seq)

    y = _post_call(
        x2, o.reshape(batch * seq, -1), ma, sgb, mod3, g_ffn[l].reshape(1, d),
        g_final.reshape(1, d), w_out_b[l].astype(BF16), w_out[l].astype(BF16),
        w_gate[l].astype(BF16), w_up[l].astype(BF16), w_down[l].astype(BF16), seq=seq)
    return y.reshape(batch, seq, d)
```

```python
import functools
import math

import jax
import jax.numpy as jnp
import numpy as np
from jax import lax
from jax.experimental import pallas as pl
from jax.experimental.pallas import tpu as pltpu

CHUNK = 64
CONV_WIDTH = 3
N_HEADS = 4
DH = 64
DV = 2 * DH
EPS = 1e-6
NEG_INF = -1e30
LAMBDA_INIT = 0.8 - 0.6 * math.exp(-0.3 * 1)
LOG2E = math.log2(math.e)
ALIBI_SLOPES = tuple(2.0 ** (-8.0 * (i + 1) / N_HEADS) for i in range(N_HEADS))

ADA_BLOCK_N = 1536
PRE_BLOCK_M = 512
POST_BLOCK_M = 256
ATTN_BLOCK = 512
ATTN_ONES_ROWS = 16
V7X_VMEM_LIMIT = 56 * 1024 * 1024

BF16 = jnp.bfloat16
F32 = jnp.float32


def _dot(a, b):
    return jnp.dot(a, b, preferred_element_type=F32)


def _sigmoid(x):
    return 1.0 / (1.0 + jnp.exp(-x))


def _ada_kernel(c_ref, w_ref, b_ref, o_ref):
    c = c_ref[...]
    sc = (c * _sigmoid(c)).astype(BF16)
    o_ref[...] = _dot(sc, w_ref[...].astype(BF16)) + b_ref[...]


def _ada_call(c_pad, w_ada, b_ada):
    rows, d = c_pad.shape
    n = w_ada.shape[1]
    return pl.pallas_call(
        _ada_kernel,
        grid=(n // ADA_BLOCK_N,),
        in_specs=[
            pl.BlockSpec((rows, d), lambda j: (0, 0)),
            pl.BlockSpec((d, ADA_BLOCK_N), lambda j: (0, j)),
            pl.BlockSpec((1, ADA_BLOCK_N), lambda j: (0, j)),
        ],
        out_specs=pl.BlockSpec((rows, ADA_BLOCK_N), lambda j: (0, j)),
        out_shape=jax.ShapeDtypeStruct((rows, n), F32),
        compiler_params=pltpu.CompilerParams(
            dimension_semantics=("arbitrary",), vmem_limit_bytes=V7X_VMEM_LIMIT),
        name="ada_mod",
    )(c_pad, w_ada, b_ada)


def _pre_kernel(x_ref, mod_ref, g_ref, win_ref, cw_ref, woa_ref,
                ma_ref, sgb_ref, qt_ref, k_ref, vt_ref, carry_ref,
                *, tiles_per_seq, d_conv, d_attn, d_model):
    i = pl.program_id(0)
    tm = x_ref.shape[0]

    x = x_ref[...]
    mod = mod_ref[0]
    xn = x * lax.rsqrt(jnp.mean(x * x, axis=-1, keepdims=True) + EPS)
    h = (xn * g_ref[...]) * (1.0 + mod[1:2, :]) + mod[0:1, :]
    hb = h.astype(BF16)

    c0, c1 = 0, 3 * d_conv
    pc = _dot(hb, win_ref[:, c0:c1])
    u = pc[:, 0:d_conv]
    gb = pc[:, d_conv:2 * d_conv]
    gc = pc[:, 2 * d_conv:3 * d_conv]
    cu = gc * u

    @pl.when(i % tiles_per_seq == 0)
    def _():
        carry_ref[...] = jnp.zeros_like(carry_ref)

    ext = jnp.concatenate([carry_ref[...], cu], axis=0)
    prev1 = pltpu.roll(ext, 1, axis=0)[8:, :]
    prev2 = pltpu.roll(ext, 2, axis=0)[8:, :]
    carry_ref[...] = cu[tm - 8:, :]
    cw = cw_ref[...]
    z = cw[0:1, :] * prev2 + cw[1:2, :] * prev1 + cw[2:3, :] * cu
    y_a = _dot((gb * z).astype(BF16), woa_ref[...])

    g0 = 3 * d_conv + 3 * d_attn
    pg = _dot(hb, win_ref[:, g0:g0 + 2 * d_model])
    ma_ref[...] = (_sigmoid(pg[:, 0:d_model]) * y_a).astype(BF16)
    sgb_ref[...] = _sigmoid(pg[:, d_model:2 * d_model]).astype(BF16)

    a0 = 3 * d_conv
    pa = _dot(hb, win_ref[:, a0:a0 + 3 * d_attn])
    q = pa[:, 0:d_attn] * (LOG2E / math.sqrt(DH))
    qt_ref[0] = q.T.astype(BF16)
    k_ref[...] = pa[:, d_attn:2 * d_attn].astype(BF16)
    v = pa[:, 2 * d_attn:3 * d_attn]
    for s in range(tm // ATTN_BLOCK):
        vt_ref[0, s] = v[s * ATTN_BLOCK:(s + 1) * ATTN_BLOCK, :].T.astype(BF16)


def _pre_call(x2, mod3, g_mix, w_in_b, conv_w, w_out_a_b, *, batch, seq):
    t, d = x2.shape
    d_conv = conv_w.shape[1]
    d_attn = N_HEADS * DV
    tm = PRE_BLOCK_M
    tps = seq // tm
    nkb = seq // ATTN_BLOCK
    kern = functools.partial(_pre_kernel, tiles_per_seq=tps, d_conv=d_conv,
                             d_attn=d_attn, d_model=d)
    full = lambda a: pl.BlockSpec(a.shape, lambda i: (0,) * a.ndim)
    return pl.pallas_call(
        kern,
        grid=(t // tm,),
        in_specs=[
            pl.BlockSpec((tm, d), lambda i: (i, 0)),
            pl.BlockSpec((1,) + mod3.shape[1:], lambda i: (i // tps, 0, 0)),
            full(g_mix), full(w_in_b), full(conv_w), full(w_out_a_b),
        ],
        out_specs=[
            pl.BlockSpec((tm, d), lambda i: (i, 0)),
            pl.BlockSpec((tm, d), lambda i: (i, 0)),
            pl.BlockSpec((1, d_attn, tm), lambda i: (i // tps, 0, i % tps)),
            pl.BlockSpec((tm, d_attn), lambda i: (i, 0)),
            pl.BlockSpec((1, tm // ATTN_BLOCK, d_attn, ATTN_BLOCK),
                         lambda i: (i // tps, i % tps, 0, 0)),
        ],
        out_shape=[
            jax.ShapeDtypeStruct((t, d), BF16),
            jax.ShapeDtypeStruct((t, d), BF16),
            jax.ShapeDtypeStruct((batch, d_attn, seq), BF16),
            jax.ShapeDtypeStruct((t, d_attn), BF16),
            jax.ShapeDtypeStruct((batch, nkb, d_attn, ATTN_BLOCK), BF16),
        ],
        scratch_shapes=[pltpu.VMEM((8, d_conv), F32)],
        compiler_params=pltpu.CompilerParams(
            dimension_semantics=("arbitrary",), vmem_limit_bytes=V7X_VMEM_LIMIT),
        name="mixer_in",
    )(x2, mod3, g_mix, w_in_b, conv_w, w_out_a_b)


def _attn_kernel(slope_ref, qt_ref, k_ref, vt_ref, bias_ref, lam_ref, g_ref,
                 o_ref, acc_ref, s_ref):
    h = pl.program_id(1)
    qi = pl.program_id(2)
    tb = ATTN_BLOCK
    slope = slope_ref[h]

    qt = qt_ref[0]
    row = lax.broadcasted_iota(jnp.int32, qt.shape, 0)
    zero = jnp.zeros_like(qt)
    qmaps = (jnp.where(row < DH, qt, zero), jnp.where(row >= DH, qt, zero))
    ones = jnp.ones((ATTN_ONES_ROWS, tb), BF16)

    acc_ref[...] = jnp.zeros_like(acc_ref)

    def step(blocks, ms):
        for n, (kj, which) in enumerate(blocks):
            kb = k_ref[0, pl.ds(pl.multiple_of(kj * tb, tb), tb), :]
            for c in range(2):
                s_ref[n, c] = _dot(kb, qmaps[c]) + bias_ref[0, which]
        ms = list(ms)
        for n, (kj, which) in enumerate(blocks):
            vb = jnp.concatenate([vt_ref[0, kj], ones], axis=0)
            shift = -slope * ((qi - kj) * tb).astype(F32)
            for c in range(2):
                s = s_ref[n, c]
                m_new = jnp.maximum(ms[c], jnp.max(s, axis=0, keepdims=True) + shift)
                alpha = jnp.exp2(ms[c] - m_new)
                p = jnp.exp2(s - (m_new - shift)).astype(BF16)
                acc_ref[c] = alpha * acc_ref[c] + _dot(vb, p)
                ms[c] = m_new
        return tuple(ms)

    m0 = jnp.full((1, tb), NEG_INF, F32)
    ms = lax.fori_loop(0, qi // 2,
                       lambda t, c: step([(2 * t, 0), (2 * t + 1, 0)], c), (m0, m0))
    ms = lax.cond(qi % 2 == 1,
                  lambda c: step([(qi - 1, 0), (qi, 1)], c),
                  lambda c: step([(qi, 1)], c), ms)

    lp = lam_ref[...]
    lam = (jnp.exp(jnp.sum(lp[0:1, :] * lp[1:2, :], axis=-1, keepdims=True))
           - jnp.exp(jnp.sum(lp[2:3, :] * lp[3:4, :], axis=-1, keepdims=True))
           + LAMBDA_INIT)
    a1 = acc_ref[0]
    a2 = acc_ref[1]
    ot = (a1[0:DV, :] * (1.0 / a1[DV:DV + 1, :])
          - (lam * a2[0:DV, :]) * (1.0 / a2[DV:DV + 1, :]))
    o = ot.T
    o = o * lax.rsqrt(jnp.mean(o * o, axis=-1, keepdims=True) + EPS)
    o_ref[0] = ((o * g_ref[...]) * (1.0 - LAMBDA_INIT)).astype(BF16)


def _attn_bias_tiles():
    tb = ATTN_BLOCK
    jj = np.arange(tb, dtype=np.int64)[:, None]
    ii = np.arange(tb, dtype=np.int64)[None, :]
    slopes = np.asarray(ALIBI_SLOPES, np.float64)[:, None, None]
    off = -slopes * (ii - jj)[None].astype(np.float64)
    allowed = (jj // CHUNK) <= (ii // CHUNK)
    diag = np.where(allowed[None], -slopes * np.abs(ii - jj)[None], NEG_INF)
    return (LOG2E * np.stack([off, diag], axis=1)).astype(np.float32)


def _attn_call(qt, k3, vt, lam_params, subln_g, *, batch, seq):
    tb = ATTN_BLOCK
    nb = seq // tb
    d_attn = N_HEADS * DV
    bias = jnp.asarray(_attn_bias_tiles())
    slopes = jnp.asarray([LOG2E * a for a in ALIBI_SLOPES], F32)
    return pl.pallas_call(
        _attn_kernel,
        grid=(batch, N_HEADS, nb),
        in_specs=[
            pl.BlockSpec(memory_space=pltpu.SMEM),
            pl.BlockSpec((1, DV, tb), lambda b, h, q: (b, h, q)),
            pl.BlockSpec((1, seq, DV), lambda b, h, q: (b, 0, h)),
            pl.BlockSpec((1, nb, DV, tb), lambda b, h, q: (b, 0, h, 0)),
            pl.BlockSpec((1, 2, tb, tb), lambda b, h, q: (h, 0, 0, 0)),
            pl.BlockSpec(lam_params.shape, lambda b, h, q: (0, 0)),
            pl.BlockSpec(subln_g.shape, lambda b, h, q: (0, 0)),
        ],
        out_specs=pl.BlockSpec((1, tb, DV), lambda b, h, q: (b, q, h)),
        out_shape=jax.ShapeDtypeStruct((batch, seq, d_attn), BF16),
        scratch_shapes=[pltpu.VMEM((2, DV + ATTN_ONES_ROWS, tb), F32),
                        pltpu.VMEM((2, 2, tb, tb), F32)],
        compiler_params=pltpu.CompilerParams(
            dimension_semantics=("arbitrary", "arbitrary", "arbitrary"),
            vmem_limit_bytes=V7X_VMEM_LIMIT),
        name="diff_attn",
    )(slopes, qt, k3, vt, bias, lam_params, subln_g)


def _post_kernel(x_ref, o_ref, ma_ref, sgb_ref, mod_ref, gf_ref, gfin_ref,
                 wob_ref, wo_ref, wg_ref, wu_ref, wd_ref, y_ref):
    mod = mod_ref[0]
    y_b = _dot(o_ref[...], wob_ref[...])
    merged = ma_ref[...].astype(F32) + sgb_ref[...].astype(F32) * y_b
    x1 = x_ref[...] + mod[2:3, :] * _dot(merged.astype(BF16), wo_ref[...])

    xn = x1 * lax.rsqrt(jnp.mean(x1 * x1, axis=-1, keepdims=True) + EPS)
    h2 = ((xn * gf_ref[...]) * (1.0 + mod[4:5, :]) + mod[3:4, :]).astype(BF16)
    gate = _dot(h2, wg_ref[...])
    up = _dot(h2, wu_ref[...])
    act = ((gate * _sigmoid(gate)) * up).astype(BF16)
    x2 = x1 + mod[5:6, :] * _dot(act, wd_ref[...])

    y_ref[...] = (x2 * lax.rsqrt(jnp.mean(x2 * x2, axis=-1, keepdims=True) + EPS)) * gfin_ref[...]


def _post_call(x2, o2, ma, sgb, mod3, g_ffn, g_final, wob, wo, wg, wu, wd, *, seq):
    t, d = x2.shape
    tm = POST_BLOCK_M
    tps = seq // tm
    full = lambda a: pl.BlockSpec(a.shape, lambda i: (0,) * a.ndim)
    row = lambda a: pl.BlockSpec((tm, a.shape[1]), lambda i: (i, 0))
    return pl.pallas_call(
        _post_kernel,
        grid=(t // tm,),
        in_specs=[
            row(x2), row(o2), row(ma), row(sgb),
            pl.BlockSpec((1,) + mod3.shape[1:], lambda i: (i // tps, 0, 0)),
            full(g_ffn), full(g_final), full(wob), full(wo), full(wg), full(wu), full(wd),
        ],
        out_specs=pl.BlockSpec((tm, d), lambda i: (i, 0)),
        out_shape=jax.ShapeDtypeStruct((t, d), F32),
        compiler_params=pltpu.CompilerParams(
            dimension_semantics=("arbitrary",), vmem_limit_bytes=V7X_VMEM_LIMIT),
        name="mixer_out_ffn",
    )(x2, o2, ma, sgb, mod3, g_ffn, g_final, wob, wo, wg, wu, wd)


def kernel(x, c, w_ada, b_ada, g_mix, w_in, conv_w, w_out_a, lambda_q1, lambda_k1, lambda_q2,
           lambda_k2, subln_g, w_out_b, w_out, g_ffn, w_gate, w_up, w_down, g_final):
    batch, seq, d = x.shape
    depth = w_ada.shape[0]
    assert depth == 1 and seq % PRE_BLOCK_M == 0 and seq % POST_BLOCK_M == 0
    assert PRE_BLOCK_M % ATTN_BLOCK == 0 and ATTN_BLOCK % CHUNK == 0
    l = 0
    x2 = x.reshape(batch * seq, d)

    c_pad = jnp.pad(c, ((0, 8 - batch), (0, 0)))
    mod = _ada_call(c_pad, w_ada[l], b_ada[l].reshape(1, -1))
    mod3 = mod[:batch].reshape(batch, 6, d)

    ma, sgb, qt, k2, vt = _pre_call(
        x2, mod3, g_mix[l].reshape(1, d), w_in[l].astype(BF16), conv_w[l],
        w_out_a[l].astype(BF16), batch=batch, seq=seq)

    lam_params = jnp.stack([lambda_q1[l], lambda_k1[l], lambda_q2[l], lambda_k2[l]], axis=0)
    o = _attn_call(qt, k2.reshape(batch, seq, -1), vt, lam_params,
                   subln_g[l].reshape(1, DV), batch=batch, seq=seq)

    y = _post_call(
        x2, o.reshape(batch * seq, -1), ma, sgb, mod3, g_ffn[l].reshape(1, d),
        g_final.reshape(1, d), w_out_b[l].astype(BF16), w_out[l].astype(BF16),
        w_gate[l].astype(BF16), w_up[l].astype(BF16), w_down[l].astype(BF16), seq=seq)
    return y.reshape(batch, seq, d)
```

```python
import functools
import math

import jax
import jax.numpy as jnp
import numpy as np
from jax import lax
from jax.experimental import pallas as pl
from jax.experimental.pallas import tpu as pltpu

CHUNK = 64
CONV_WIDTH = 3
N_HEADS = 4
DH = 64
DV = 2 * DH
EPS = 1e-6
NEG_INF = -1e30
LAMBDA_INIT = 0.8 - 0.6 * math.exp(-0.3 * 1)
LOG2E = math.log2(math.e)
ALIBI_SLOPES = tuple(2.0 ** (-8.0 * (i + 1) / N_HEADS) for i in range(N_HEADS))

ADA_BLOCK_N = 1536
PRE_BLOCK_M = 512
POST_BLOCK_M = 512
ATTN_KBLOCK = 512
ATTN_QBLOCK = 1024
ATTN_ONES_ROWS = 16
V7X_VMEM_LIMIT = 56 * 1024 * 1024

BF16 = jnp.bfloat16
F32 = jnp.float32


def _dot(a, b):
    return jnp.dot(a, b, preferred_element_type=F32)


def _sigmoid(x):
    return 1.0 / (1.0 + jnp.exp(-x))


def _ada_kernel(c_ref, w_ref, b_ref, o_ref):
    c = c_ref[...]
    sc = (c * _sigmoid(c)).astype(BF16)
    o_ref[...] = _dot(sc, w_ref[...].astype(BF16)) + b_ref[...]


def _ada_call(c_pad, w_ada, b_ada):
    rows, d = c_pad.shape
    n = w_ada.shape[1]
    return pl.pallas_call(
        _ada_kernel,
        grid=(n // ADA_BLOCK_N,),
        in_specs=[
            pl.BlockSpec((rows, d), lambda j: (0, 0)),
            pl.BlockSpec((d, ADA_BLOCK_N), lambda j: (0, j)),
            pl.BlockSpec((1, ADA_BLOCK_N), lambda j: (0, j)),
        ],
        out_specs=pl.BlockSpec((rows, ADA_BLOCK_N), lambda j: (0, j)),
        out_shape=jax.ShapeDtypeStruct((rows, n), F32),
        compiler_params=pltpu.CompilerParams(
            dimension_semantics=("arbitrary",), vmem_limit_bytes=V7X_VMEM_LIMIT),
        name="ada_mod",
    )(c_pad, w_ada, b_ada)


def _pre_kernel(x_ref, mod_ref, g_ref, win_ref, cw_ref, woa_ref,
                ma_ref, sgb_ref, qt_ref, k_ref, vt_ref, carry_ref,
                *, tiles_per_seq, d_conv, d_attn, d_model):
    i = pl.program_id(0)
    tm = x_ref.shape[0]

    x = x_ref[...]
    mod = mod_ref[0]
    xn = x * lax.rsqrt(jnp.mean(x * x, axis=-1, keepdims=True) + EPS)
    h = (xn * g_ref[...]) * (1.0 + mod[1:2, :]) + mod[0:1, :]
    hb = h.astype(BF16)

    c0, c1 = 0, 3 * d_conv
    pc = _dot(hb, win_ref[:, c0:c1])
    u = pc[:, 0:d_conv]
    gb = pc[:, d_conv:2 * d_conv]
    gc = pc[:, 2 * d_conv:3 * d_conv]
    cu = gc * u

    @pl.when(i % tiles_per_seq == 0)
    def _():
        carry_ref[...] = jnp.zeros_like(carry_ref)

    ext = jnp.concatenate([carry_ref[...], cu], axis=0)
    prev1 = pltpu.roll(ext, 1, axis=0)[8:, :]
    prev2 = pltpu.roll(ext, 2, axis=0)[8:, :]
    carry_ref[...] = cu[tm - 8:, :]
    cw = cw_ref[...]
    z = cw[0:1, :] * prev2 + cw[1:2, :] * prev1 + cw[2:3, :] * cu
    y_a = _dot((gb * z).astype(BF16), woa_ref[...])

    g0 = 3 * d_conv + 3 * d_attn
    pg = _dot(hb, win_ref[:, g0:g0 + 2 * d_model])
    ma_ref[...] = (_sigmoid(pg[:, 0:d_model]) * y_a).astype(BF16)
    sgb_ref[...] = _sigmoid(pg[:, d_model:2 * d_model]).astype(BF16)

    a0 = 3 * d_conv
    pa = _dot(hb, win_ref[:, a0:a0 + 3 * d_attn])
    q = pa[:, 0:d_attn] * (LOG2E / math.sqrt(DH))
    qt_ref[0] = q.T.astype(BF16)
    k_ref[...] = pa[:, d_attn:2 * d_attn].astype(BF16)
    v = pa[:, 2 * d_attn:3 * d_attn]
    for s in range(tm // ATTN_KBLOCK):
        vt_ref[0, s] = v[s * ATTN_KBLOCK:(s + 1) * ATTN_KBLOCK, :].T.astype(BF16)


def _pre_call(x2, mod3, g_mix, w_in_b, conv_w, w_out_a_b, *, batch, seq):
    t, d = x2.shape
    d_conv = conv_w.shape[1]
    d_attn = N_HEADS * DV
    tm = PRE_BLOCK_M
    tps = seq // tm
    nkb = seq // ATTN_KBLOCK
    kern = functools.partial(_pre_kernel, tiles_per_seq=tps, d_conv=d_conv,
                             d_attn=d_attn, d_model=d)
    full = lambda a: pl.BlockSpec(a.shape, lambda i: (0,) * a.ndim)
    return pl.pallas_call(
        kern,
        grid=(t // tm,),
        in_specs=[
            pl.BlockSpec((tm, d), lambda i: (i, 0)),
            pl.BlockSpec((1,) + mod3.shape[1:], lambda i: (i // tps, 0, 0)),
            full(g_mix), full(w_in_b), full(conv_w), full(w_out_a_b),
        ],
        out_specs=[
            pl.BlockSpec((tm, d), lambda i: (i, 0)),
            pl.BlockSpec((tm, d), lambda i: (i, 0)),
            pl.BlockSpec((1, d_attn, tm), lambda i: (i // tps, 0, i % tps)),
            pl.BlockSpec((tm, d_attn), lambda i: (i, 0)),
            pl.BlockSpec((1, tm // ATTN_KBLOCK, d_attn, ATTN_KBLOCK),
                         lambda i: (i // tps, i % tps, 0, 0)),
        ],
        out_shape=[
            jax.ShapeDtypeStruct((t, d), BF16),
            jax.ShapeDtypeStruct((t, d), BF16),
            jax.ShapeDtypeStruct((batch, d_attn, seq), BF16),
            jax.ShapeDtypeStruct((t, d_attn), BF16),
            jax.ShapeDtypeStruct((batch, nkb, d_attn, ATTN_KBLOCK), BF16),
        ],
        scratch_shapes=[pltpu.VMEM((8, d_conv), F32)],
        compiler_params=pltpu.CompilerParams(
            dimension_semantics=("arbitrary",), vmem_limit_bytes=V7X_VMEM_LIMIT),
        name="mixer_in",
    )(x2, mod3, g_mix, w_in_b, conv_w, w_out_a_b)


def _attn_kernel(slope_ref, qt_ref, k_ref, vt_ref, bias_ref, lam_ref, g_ref,
                 o_ref, acc_ref, s_ref, p_ref):
    h = pl.program_id(1)
    qi = pl.program_id(2)
    tq, tk = ATTN_QBLOCK, ATTN_KBLOCK
    ratio = tq // tk
    slope = slope_ref[h]

    qt = qt_ref[0]
    row = lax.broadcasted_iota(jnp.int32, qt.shape, 0)
    zero = jnp.zeros_like(qt)
    qmaps = (jnp.where(row < DH, qt, zero), jnp.where(row >= DH, qt, zero))
    ones = jnp.ones((ATTN_ONES_ROWS, tk), BF16)

    acc_ref[...] = jnp.zeros_like(acc_ref)

    def scores_into_scratch(kj):
        which = jnp.clip(kj - ratio * qi + 1, 0, ratio)
        kb = k_ref[0, pl.ds(pl.multiple_of(kj * tk, tk), tk), :]
        bias = bias_ref[0, which]
        col_max = []
        for c in range(2):
            s = _dot(kb, qmaps[c]) + bias
            s_ref[c] = s
            col_max.append(jnp.max(s, axis=0, keepdims=True))
        return tuple(col_max)

    def probabilities_into_scratch(kj, m_run, col_max):
        shift = -slope * (qi * tq - kj * tk).astype(F32)
        m_new, alpha = [], []
        for c in range(2):
            m_c = jnp.maximum(m_run[c], col_max[c] + shift)
            alpha.append(jnp.exp2(m_run[c] - m_c))
            p_ref[c] = jnp.exp2(s_ref[c] - (m_c - shift)).astype(BF16)
            m_new.append(m_c)
        return tuple(m_new), tuple(alpha)

    def accumulate(kj, alphas):
        vb = jnp.concatenate([vt_ref[0, kj], ones], axis=0)
        for c in range(2):
            acc_ref[c] = alphas[c] * acc_ref[c] + _dot(vb, p_ref[c])

    last = ratio * qi + ratio - 1
    m0 = jnp.full((1, tq), NEG_INF, F32)
    col_max = scores_into_scratch(0)
    m_run, alpha = probabilities_into_scratch(0, (m0, m0), col_max)
    col_max = scores_into_scratch(1)

    def body(kj, carry):
        m_run, col_max, alpha_prev = carry
        accumulate(kj - 1, alpha_prev)
        m_run, alpha = probabilities_into_scratch(kj, m_run, col_max)
        return m_run, scores_into_scratch(kj + 1), alpha

    m_run, col_max, alpha = lax.fori_loop(1, last, body, (m_run, col_max, alpha))
    accumulate(last - 1, alpha)
    _, alpha = probabilities_into_scratch(last, m_run, col_max)
    accumulate(last, alpha)

    lp = lam_ref[...]
    lam = (jnp.exp(jnp.sum(lp[0:1, :] * lp[1:2, :], axis=-1, keepdims=True))
           - jnp.exp(jnp.sum(lp[2:3, :] * lp[3:4, :], axis=-1, keepdims=True))
           + LAMBDA_INIT)
    a1 = acc_ref[0]
    a2 = acc_ref[1]
    ot = (a1[0:DV, :] * (1.0 / a1[DV:DV + 1, :])
          - (lam * a2[0:DV, :]) * (1.0 / a2[DV:DV + 1, :]))
    o = ot.T
    o = o * lax.rsqrt(jnp.mean(o * o, axis=-1, keepdims=True) + EPS)
    o_ref[0] = ((o * g_ref[...]) * (1.0 - LAMBDA_INIT)).astype(BF16)


def _attn_bias_tiles():
    tq, tk = ATTN_QBLOCK, ATTN_KBLOCK
    jj = np.arange(tk, dtype=np.int64)[:, None]
    ii = np.arange(tq, dtype=np.int64)[None, :]
    slopes = np.asarray(ALIBI_SLOPES, np.float64)[:, None, None]
    tiles = [-slopes * (ii - jj)[None].astype(np.float64)]
    for r in range(tq // tk):
        key = r * tk + jj
        allowed = (key // CHUNK) <= (ii // CHUNK)
        tiles.append(np.where(allowed[None], -slopes * (np.abs(ii - key)[None] + r * tk), NEG_INF))
    return (LOG2E * np.stack(tiles, axis=1)).astype(np.float32)


def _attn_call(qt, k3, vt, lam_params, subln_g, *, batch, seq):
    tq, tk = ATTN_QBLOCK, ATTN_KBLOCK
    nkb = seq // tk
    d_attn = N_HEADS * DV
    bias = jnp.asarray(_attn_bias_tiles())
    slopes = jnp.asarray([LOG2E * a for a in ALIBI_SLOPES], F32)
    return pl.pallas_call(
        _attn_kernel,
        grid=(batch, N_HEADS, seq // tq),
        in_specs=[
            pl.BlockSpec(memory_space=pltpu.SMEM),
            pl.BlockSpec((1, DV, tq), lambda b, h, q: (b, h, q)),
            pl.BlockSpec((1, seq, DV), lambda b, h, q: (b, 0, h)),
            pl.BlockSpec((1, nkb, DV, tk), lambda b, h, q: (b, 0, h, 0)),
            pl.BlockSpec((1,) + bias.shape[1:], lambda b, h, q: (h, 0, 0, 0)),
            pl.BlockSpec(lam_params.shape, lambda b, h, q: (0, 0)),
            pl.BlockSpec(subln_g.shape, lambda b, h, q: (0, 0)),
        ],
        out_specs=pl.BlockSpec((1, tq, DV), lambda b, h, q: (b, q, h)),
        out_shape=jax.ShapeDtypeStruct((batch, seq, d_attn), BF16),
        scratch_shapes=[pltpu.VMEM((2, DV + ATTN_ONES_ROWS, tq), F32),
                        pltpu.VMEM((2, tk, tq), F32),
                        pltpu.VMEM((2, tk, tq), BF16)],
        compiler_params=pltpu.CompilerParams(
            dimension_semantics=("arbitrary", "arbitrary", "arbitrary"),
            vmem_limit_bytes=V7X_VMEM_LIMIT),
        name="diff_attn",
    )(slopes, qt, k3, vt, bias, lam_params, subln_g)


def _post_kernel(x_ref, o_ref, ma_ref, sgb_ref, mod_ref, gf_ref, gfin_ref,
                 wob_ref, wo_ref, wg_ref, wu_ref, wd_ref, y_ref):
    mod = mod_ref[0]
    y_b = _dot(o_ref[...], wob_ref[...])
    merged = ma_ref[...].astype(F32) + sgb_ref[...].astype(F32) * y_b
    x1 = x_ref[...] + mod[2:3, :] * _dot(merged.astype(BF16), wo_ref[...])

    xn = x1 * lax.rsqrt(jnp.mean(x1 * x1, axis=-1, keepdims=True) + EPS)
    h2 = ((xn * gf_ref[...]) * (1.0 + mod[4:5, :]) + mod[3:4, :]).astype(BF16)
    gate = _dot(h2, wg_ref[...])
    up = _dot(h2, wu_ref[...])
    act = ((gate * _sigmoid(gate)) * up).astype(BF16)
    x2 = x1 + mod[5:6, :] * _dot(act, wd_ref[...])

    y_ref[...] = (x2 * lax.rsqrt(jnp.mean(x2 * x2, axis=-1, keepdims=True) + EPS)) * gfin_ref[...]


def _post_call(x2, o2, ma, sgb, mod3, g_ffn, g_final, wob, wo, wg, wu, wd, *, seq):
    t, d = x2.shape
    tm = POST_BLOCK_M
    tps = seq // tm
    full = lambda a: pl.BlockSpec(a.shape, lambda i: (0,) * a.ndim)
    row = lambda a: pl.BlockSpec((tm, a.shape[1]), lambda i: (i, 0))
    return pl.pallas_call(
        _post_kernel,
        grid=(t // tm,),
        in_specs=[
            row(x2), row(o2), row(ma), row(sgb),
            pl.BlockSpec((1,) + mod3.shape[1:], lambda i: (i // tps, 0, 0)),
            full(g_ffn), full(g_final), full(wob), full(wo), full(wg), full(wu), full(wd),
        ],
        out_specs=pl.BlockSpec((tm, d), lambda i: (i, 0)),
        out_shape=jax.ShapeDtypeStruct((t, d), F32),
        compiler_params=pltpu.CompilerParams(
            dimension_semantics=("arbitrary",), vmem_limit_bytes=V7X_VMEM_LIMIT),
        name="mixer_out_ffn",
    )(x2, o2, ma, sgb, mod3, g_ffn, g_final, wob, wo, wg, wu, wd)


def kernel(x, c, w_ada, b_ada, g_mix, w_in, conv_w, w_out_a, lambda_q1, lambda_k1, lambda_q2,
           lambda_k2, subln_g, w_out_b, w_out, g_ffn, w_gate, w_up, w_down, g_final):
    batch, seq, d = x.shape
    depth = w_ada.shape[0]
    assert depth == 1 and seq % PRE_BLOCK_M == 0 and seq % POST_BLOCK_M == 0
    assert PRE_BLOCK_M % ATTN_KBLOCK == 0 and ATTN_KBLOCK % CHUNK == 0
    assert ATTN_QBLOCK % (2 * ATTN_KBLOCK) == 0 and seq % ATTN_QBLOCK == 0
    l = 0
    x2 = x.reshape(batch * seq, d)

    c_pad = jnp.pad(c, ((0, 8 - batch), (0, 0)))
    mod = _ada_call(c_pad, w_ada[l], b_ada[l].reshape(1, -1))
    mod3 = mod[:batch].reshape(batch, 6, d)

    ma, sgb, qt, k2, vt = _pre_call(
        x2, mod3, g_mix[l].reshape(1, d), w_in[l].astype(BF16), conv_w[l],
        w_out_a[l].astype(BF16), batch=batch, seq=seq)

    lam_params = jnp.stack([lambda_q1[l], lambda_k1[l], lambda_q2[l], lambda_k2[l]], axis=0)
    o = _attn_call(qt, k2.reshape(batch, seq, -1), vt, lam_params,
                   subln_g[l].reshape(1, DV), batch=batch, seq=seq)

    y = _post_call(
        x2, o.reshape(batch * seq, -1), ma, sgb, mod3, g_ffn[l].reshape(1, d),
        g_final.reshape(1, d), w_out_b[l].astype(BF16), w_out[l].astype(BF16),
        w_gate[l].astype(BF16), w_up[l].astype(BF16), w_down[l].astype(BF16), seq=seq)
    return y.reshape(batch, seq, d)
```

```python
import functools
import math

import jax
import jax.numpy as jnp
import numpy as np
from jax import lax
from jax.experimental import pallas as pl
from jax.experimental.pallas import tpu as pltpu

CHUNK = 64
CONV_WIDTH = 3
N_HEADS = 4
DH = 64
DV = 2 * DH
EPS = 1e-6
NEG_INF = -1e30
LAMBDA_INIT = 0.8 - 0.6 * math.exp(-0.3 * 1)
LOG2E = math.log2(math.e)
ALIBI_SLOPES = tuple(2.0 ** (-8.0 * (i + 1) / N_HEADS) for i in range(N_HEADS))

ADA_BLOCK_N = 1536
PRE_BLOCK_M = 512
POST_BLOCK_M = 512
ATTN_KBLOCK = 512
ATTN_QBLOCK = 1024
ATTN_ONES_ROWS = 16
ATTN_BOUND_SLACK = 1.001
ATTN_BOUND_MARGIN = 0.01
ATTN_MIN_EXPONENT = -100.0
V7X_VMEM_LIMIT = 56 * 1024 * 1024

BF16 = jnp.bfloat16
F32 = jnp.float32


def _dot(a, b):
    return jnp.dot(a, b, preferred_element_type=F32)


def _sigmoid(x):
    return 1.0 / (1.0 + jnp.exp(-x))


def _ada_kernel(c_ref, w_ref, b_ref, o_ref):
    c = c_ref[...]
    sc = (c * _sigmoid(c)).astype(BF16)
    o_ref[...] = _dot(sc, w_ref[...].astype(BF16)) + b_ref[...]


def _ada_call(c_pad, w_ada, b_ada):
    rows, d = c_pad.shape
    n = w_ada.shape[1]
    return pl.pallas_call(
        _ada_kernel,
        grid=(n // ADA_BLOCK_N,),
        in_specs=[
            pl.BlockSpec((rows, d), lambda j: (0, 0)),
            pl.BlockSpec((d, ADA_BLOCK_N), lambda j: (0, j)),
            pl.BlockSpec((1, ADA_BLOCK_N), lambda j: (0, j)),
        ],
        out_specs=pl.BlockSpec((rows, ADA_BLOCK_N), lambda j: (0, j)),
        out_shape=jax.ShapeDtypeStruct((rows, n), F32),
        compiler_params=pltpu.CompilerParams(
            dimension_semantics=("arbitrary",), vmem_limit_bytes=V7X_VMEM_LIMIT),
        name="ada_mod",
    )(c_pad, w_ada, b_ada)


def _pre_kernel(x_ref, mod_ref, g_ref, win_ref, cw_ref, woa_ref,
                ma_ref, sgb_ref, qt_ref, k_ref, vt_ref, carry_ref,
                *, tiles_per_seq, d_conv, d_attn, d_model):
    i = pl.program_id(0)
    tm = x_ref.shape[0]

    x = x_ref[...]
    mod = mod_ref[0]
    xn = x * lax.rsqrt(jnp.mean(x * x, axis=-1, keepdims=True) + EPS)
    h = (xn * g_ref[...]) * (1.0 + mod[1:2, :]) + mod[0:1, :]
    hb = h.astype(BF16)

    c0, c1 = 0, 3 * d_conv
    pc = _dot(hb, win_ref[:, c0:c1])
    u = pc[:, 0:d_conv]
    gb = pc[:, d_conv:2 * d_conv]
    gc = pc[:, 2 * d_conv:3 * d_conv]
    cu = gc * u

    @pl.when(i % tiles_per_seq == 0)
    def _():
        carry_ref[...] = jnp.zeros_like(carry_ref)

    ext = jnp.concatenate([carry_ref[...], cu], axis=0)
    prev1 = pltpu.roll(ext, 1, axis=0)[8:, :]
    prev2 = pltpu.roll(ext, 2, axis=0)[8:, :]
    carry_ref[...] = cu[tm - 8:, :]
    cw = cw_ref[...]
    z = cw[0:1, :] * prev2 + cw[1:2, :] * prev1 + cw[2:3, :] * cu
    y_a = _dot((gb * z).astype(BF16), woa_ref[...])

    g0 = 3 * d_conv + 3 * d_attn
    pg = _dot(hb, win_ref[:, g0:g0 + 2 * d_model])
    ma_ref[...] = (_sigmoid(pg[:, 0:d_model]) * y_a).astype(BF16)
    sgb_ref[...] = _sigmoid(pg[:, d_model:2 * d_model]).astype(BF16)

    a0 = 3 * d_conv
    pa = _dot(hb, win_ref[:, a0:a0 + 3 * d_attn])
    q = pa[:, 0:d_attn] * (LOG2E / math.sqrt(DH))
    qt_ref[0] = q.T.astype(BF16)
    k_ref[...] = pa[:, d_attn:2 * d_attn].astype(BF16)
    v = pa[:, 2 * d_attn:3 * d_attn]
    for s in range(tm // ATTN_KBLOCK):
        vt_ref[0, s] = v[s * ATTN_KBLOCK:(s + 1) * ATTN_KBLOCK, :].T.astype(BF16)


def _pre_call(x2, mod3, g_mix, w_in_b, conv_w, w_out_a_b, *, batch, seq):
    t, d = x2.shape
    d_conv = conv_w.shape[1]
    d_attn = N_HEADS * DV
    tm = PRE_BLOCK_M
    tps = seq // tm
    nkb = seq // ATTN_KBLOCK
    kern = functools.partial(_pre_kernel, tiles_per_seq=tps, d_conv=d_conv,
                             d_attn=d_attn, d_model=d)
    full = lambda a: pl.BlockSpec(a.shape, lambda i: (0,) * a.ndim)
    return pl.pallas_call(
        kern,
        grid=(t // tm,),
        in_specs=[
            pl.BlockSpec((tm, d), lambda i: (i, 0)),
            pl.BlockSpec((1,) + mod3.shape[1:], lambda i: (i // tps, 0, 0)),
            full(g_mix), full(w_in_b), full(conv_w), full(w_out_a_b),
        ],
        out_specs=[
            pl.BlockSpec((tm, d), lambda i: (i, 0)),
            pl.BlockSpec((tm, d), lambda i: (i, 0)),
            pl.BlockSpec((1, d_attn, tm), lambda i: (i // tps, 0, i % tps)),
            pl.BlockSpec((tm, d_attn), lambda i: (i, 0)),
            pl.BlockSpec((1, tm // ATTN_KBLOCK, d_attn, ATTN_KBLOCK),
                         lambda i: (i // tps, i % tps, 0, 0)),
        ],
        out_shape=[
            jax.ShapeDtypeStruct((t, d), BF16),
            jax.ShapeDtypeStruct((t, d), BF16),
            jax.ShapeDtypeStruct((batch, d_attn, seq), BF16),
            jax.ShapeDtypeStruct((t, d_attn), BF16),
            jax.ShapeDtypeStruct((batch, nkb, d_attn, ATTN_KBLOCK), BF16),
        ],
        scratch_shapes=[pltpu.VMEM((8, d_conv), F32)],
        compiler_params=pltpu.CompilerParams(
            dimension_semantics=("arbitrary",), vmem_limit_bytes=V7X_VMEM_LIMIT),
        name="mixer_in",
    )(x2, mod3, g_mix, w_in_b, conv_w, w_out_a_b)


def _attn_kernel(slope_ref, qt_ref, k_ref, vt_ref, bias_ref, lam_ref, g_ref,
                 o_ref, acc_ref, s_ref, p_ref, ksq_ref):
    h = pl.program_id(1)
    qi = pl.program_id(2)
    tq, tk = ATTN_QBLOCK, ATTN_KBLOCK
    ratio = tq // tk
    slope = slope_ref[h]

    qt = qt_ref[0]
    row = lax.broadcasted_iota(jnp.int32, qt.shape, 0)
    zero = jnp.zeros_like(qt)
    qmaps = (jnp.where(row < DH, qt, zero), jnp.where(row >= DH, qt, zero))
    ones = jnp.ones((ATTN_ONES_ROWS, tk), BF16)

    last = ratio * qi + ratio - 1

    def key_block(kj):
        return k_ref[0, pl.ds(pl.multiple_of(kj * tk, tk), tk), :]

    def bias_tile(kj):
        return bias_ref[0, jnp.clip(kj - ratio * qi + 1, 0, ratio)]

    def bias_shift(kj):
        return -slope * (qi * tq - kj * tk).astype(F32)

    def accumulate(kj, alphas=None):
        vb = jnp.concatenate([vt_ref[0, kj], ones], axis=0)
        for c in range(2):
            old = acc_ref[c] if alphas is None else alphas[c] * acc_ref[c]
            acc_ref[c] = old + _dot(vb, p_ref[c])

    @pl.when(qi == 0)
    def _():
        lane = lax.broadcasted_iota(jnp.int32, (tk, 2 * DH), 1)

        def kbody(j, carry):
            kf = key_block(j).astype(F32)
            sq = kf * kf
            n1 = jnp.sum(jnp.where(lane < DH, sq, 0.0), axis=-1, keepdims=True)
            n2 = jnp.sum(jnp.where(lane >= DH, sq, 0.0), axis=-1, keepdims=True)
            return (jnp.maximum(carry[0], jnp.max(n1, axis=0, keepdims=True)),
                    jnp.maximum(carry[1], jnp.max(n2, axis=0, keepdims=True)))

        z = jnp.zeros((1, 1), F32)
        k_sq = lax.fori_loop(0, k_ref.shape[1] // tk, kbody, (z, z))
        for c in range(2):
            ksq_ref[c] = jnp.broadcast_to(k_sq[c], ksq_ref.shape[1:])

    stab = []
    for c in range(2):
        qf = qmaps[c].astype(F32)
        q_sq = jnp.sum(qf * qf, axis=0, keepdims=True)
        stab.append(jnp.sqrt(q_sq * ksq_ref[c][0:1, 0:1]) * ATTN_BOUND_SLACK + ATTN_BOUND_MARGIN)

    def exponentials_into_scratch(kj, e_max):
        kb = key_block(kj)
        bias = bias_tile(kj)
        shift = bias_shift(kj)
        out = []
        for c in range(2):
            e = _dot(kb, qmaps[c]) + bias + (shift - stab[c])
            p_ref[c] = jnp.exp2(e).astype(BF16)
            out.append(jnp.maximum(e_max[c], jnp.max(e, axis=0, keepdims=True)))
        return tuple(out)

    acc_ref[...] = jnp.zeros_like(acc_ref)
    e0 = jnp.full((1, tq), NEG_INF, F32)

    def fast_body(kj, e_max):
        accumulate(kj - 1)
        return exponentials_into_scratch(kj, e_max)

    e_max = lax.fori_loop(1, last + 1, fast_body, exponentials_into_scratch(0, (e0, e0)))
    accumulate(last)
    smallest = jnp.minimum(jnp.min(e_max[0]), jnp.min(e_max[1]))

    @pl.when(smallest < ATTN_MIN_EXPONENT)
    def _():
        def body(kj, m_run):
            kb = key_block(kj)
            bias = bias_tile(kj)
            shift = bias_shift(kj)
            m_new, alpha = [], []
            for c in range(2):
                s = _dot(kb, qmaps[c]) + bias
                s_ref[c] = s
                m_c = jnp.maximum(m_run[c], jnp.max(s, axis=0, keepdims=True) + shift)
                alpha.append(jnp.exp2(m_run[c] - m_c))
                p_ref[c] = jnp.exp2(s_ref[c] - (m_c - shift)).astype(BF16)
                m_new.append(m_c)
            accumulate(kj, alpha)
            return tuple(m_new)

        acc_ref[...] = jnp.zeros_like(acc_ref)
        lax.fori_loop(0, last + 1, body, (e0, e0))

    lp = lam_ref[...]
    lam = (jnp.exp(jnp.sum(lp[0:1, :] * lp[1:2, :], axis=-1, keepdims=True))
           - jnp.exp(jnp.sum(lp[2:3, :] * lp[3:4, :], axis=-1, keepdims=True))
           + LAMBDA_INIT)
    a1 = acc_ref[0]
    a2 = acc_ref[1]
    ot = (a1[0:DV, :] * (1.0 / a1[DV:DV + 1, :])
          - (lam * a2[0:DV, :]) * (1.0 / a2[DV:DV + 1, :]))
    o = ot.T
    o = o * lax.rsqrt(jnp.mean(o * o, axis=-1, keepdims=True) + EPS)
    o_ref[0] = ((o * g_ref[...]) * (1.0 - LAMBDA_INIT)).astype(BF16)


def _attn_bias_tiles():
    tq, tk = ATTN_QBLOCK, ATTN_KBLOCK
    jj = np.arange(tk, dtype=np.int64)[:, None]
    ii = np.arange(tq, dtype=np.int64)[None, :]
    slopes = np.asarray(ALIBI_SLOPES, np.float64)[:, None, None]
    tiles = [-slopes * (ii - jj)[None].astype(np.float64)]
    for r in range(tq // tk):
        key = r * tk + jj
        allowed = (key // CHUNK) <= (ii // CHUNK)
        tiles.append(np.where(allowed[None], -slopes * (np.abs(ii - key)[None] + r * tk), NEG_INF))
    return (LOG2E * np.stack(tiles, axis=1)).astype(np.float32)


def _attn_call(qt, k3, vt, lam_params, subln_g, *, batch, seq):
    tq, tk = ATTN_QBLOCK, ATTN_KBLOCK
    nkb = seq // tk
    d_attn = N_HEADS * DV
    bias = jnp.asarray(_attn_bias_tiles())
    slopes = jnp.asarray([LOG2E * a for a in ALIBI_SLOPES], F32)
    return pl.pallas_call(
        _attn_kernel,
        grid=(batch, N_HEADS, seq // tq),
        in_specs=[
            pl.BlockSpec(memory_space=pltpu.SMEM),
            pl.BlockSpec((1, DV, tq), lambda b, h, q: (b, h, q)),
            pl.BlockSpec((1, seq, DV), lambda b, h, q: (b, 0, h)),
            pl.BlockSpec((1, nkb, DV, tk), lambda b, h, q: (b, 0, h, 0)),
            pl.BlockSpec((1,) + bias.shape[1:], lambda b, h, q: (h, 0, 0, 0)),
            pl.BlockSpec(lam_params.shape, lambda b, h, q: (0, 0)),
            pl.BlockSpec(subln_g.shape, lambda b, h, q: (0, 0)),
        ],
        out_specs=pl.BlockSpec((1, tq, DV), lambda b, h, q: (b, q, h)),
        out_shape=jax.ShapeDtypeStruct((batch, seq, d_attn), BF16),
        scratch_shapes=[pltpu.VMEM((2, DV + ATTN_ONES_ROWS, tq), F32),
                        pltpu.VMEM((2, tk, tq), F32),
                        pltpu.VMEM((2, tk, tq), BF16),
                        pltpu.VMEM((2, 8, 128), F32)],
        compiler_params=pltpu.CompilerParams(
            dimension_semantics=("arbitrary", "arbitrary", "arbitrary"),
            vmem_limit_bytes=V7X_VMEM_LIMIT),
        name="diff_attn",
    )(slopes, qt, k3, vt, bias, lam_params, subln_g)


def _post_kernel(x_ref, o_ref, ma_ref, sgb_ref, mod_ref, gf_ref, gfin_ref,
                 wob_ref, wo_ref, wg_ref, wu_ref, wd_ref, y_ref):
    mod = mod_ref[0]
    y_b = _dot(o_ref[...], wob_ref[...])
    merged = ma_ref[...].astype(F32) + sgb_ref[...].astype(F32) * y_b
    x1 = x_ref[...] + mod[2:3, :] * _dot(merged.astype(BF16), wo_ref[...])

    xn = x1 * lax.rsqrt(jnp.mean(x1 * x1, axis=-1, keepdims=True) + EPS)
    h2 = ((xn * gf_ref[...]) * (1.0 + mod[4:5, :]) + mod[3:4, :]).astype(BF16)
    gate = _dot(h2, wg_ref[...])
    up = _dot(h2, wu_ref[...])
    act = ((gate * _sigmoid(gate)) * up).astype(BF16)
    x2 = x1 + mod[5:6, :] * _dot(act, wd_ref[...])

    y_ref[...] = (x2 * lax.rsqrt(jnp.mean(x2 * x2, axis=-1, keepdims=True) + EPS)) * gfin_ref[...]


def _post_call(x2, o2, ma, sgb, mod3, g_ffn, g_final, wob, wo, wg, wu, wd, *, seq):
    t, d = x2.shape
    tm = POST_BLOCK_M
    tps = seq // tm
    full = lambda a: pl.BlockSpec(a.shape, lambda i: (0,) * a.ndim)
    row = lambda a: pl.BlockSpec((tm, a.shape[1]), lambda i: (i, 0))
    return pl.pallas_call(
        _post_kernel,
        grid=(t // tm,),
        in_specs=[
            row(x2), row(o2), row(ma), row(sgb),
            pl.BlockSpec((1,) + mod3.shape[1:], lambda i: (i // tps, 0, 0)),
            full(g_ffn), full(g_final), full(wob), full(wo), full(wg), full(wu), full(wd),
        ],
        out_specs=pl.BlockSpec((tm, d), lambda i: (i, 0)),
        out_shape=jax.ShapeDtypeStruct((t, d), F32),
        compiler_params=pltpu.CompilerParams(
            dimension_semantics=("arbitrary",), vmem_limit_bytes=V7X_VMEM_LIMIT),
        name="mixer_out_ffn",
    )(x2, o2, ma, sgb, mod3, g_ffn, g_final, wob, wo, wg, wu, wd)


def kernel(x, c, w_ada, b_ada, g_mix, w_in, conv_w, w_out_a, lambda_q1, lambda_k1, lambda_q2,
           lambda_k2, subln_g, w_out_b, w_out, g_ffn, w_gate, w_up, w_down, g_final):
    batch, seq, d = x.shape
    depth = w_ada.shape[0]
    assert depth == 1 and seq % PRE_BLOCK_M == 0 and seq % POST_BLOCK_M == 0
    assert PRE_BLOCK_M % ATTN_KBLOCK == 0 and ATTN_KBLOCK % CHUNK == 0
    assert ATTN_QBLOCK % (2 * ATTN_KBLOCK) == 0 and seq % ATTN_QBLOCK == 0
    l = 0
    x2 = x.reshape(batch * seq, d)

    c_pad = jnp.pad(c, ((0, 8 - batch), (0, 0)))
    mod = _ada_call(c_pad, w_ada[l], b_ada[l].reshape(1, -1))
    mod3 = mod[:batch].reshape(batch, 6, d)

    ma, sgb, qt, k2, vt = _pre_call(
        x2, mod3, g_mix[l].reshape(1, d), w_in[l].astype(BF16), conv_w[l],
        w_out_a[l].astype(BF16), batch=batch, seq=seq)

    lam_params = jnp.stack([lambda_q1[l], lambda_k1[l], lambda_q2[l], lambda_k2[l]], axis=0)
    o = _attn_call(qt, k2.reshape(batch, seq, -1), vt, lam_params,
                   subln_g[l].reshape(1, DV), batch=batch, seq=seq)

    y = _post_call(
        x2, o.reshape(batch * seq, -1), ma, sgb, mod3, g_ffn[l].reshape(1, d),
        g_final.reshape(1, d), w_out_b[l].astype(BF16), w_out[l].astype(BF16),
        w_gate[l].astype(BF16), w_up[l].astype(BF16), w_down[l].astype(BF16), seq=seq)
    return y.reshape(batch, seq, d)
```

```python
import functools
import math

import jax
import jax.numpy as jnp
import numpy as np
from jax import lax
from jax.experimental import pallas as pl
from jax.experimental.pallas import tpu as pltpu

CHUNK = 64
CONV_WIDTH = 3
N_HEADS = 4
DH = 64
DV = 2 * DH
EPS = 1e-6
NEG_INF = -1e30
LAMBDA_INIT = 0.8 - 0.6 * math.exp(-0.3 * 1)
LOG2E = math.log2(math.e)
ALIBI_SLOPES = tuple(2.0 ** (-8.0 * (i + 1) / N_HEADS) for i in range(N_HEADS))

ADA_BLOCK_N = 1536
PRE_BLOCK_M = 512
POST_BLOCK_M = 512
ATTN_KBLOCK = 512
ATTN_QBLOCK = 1024
ATTN_ONES_ROWS = 16
ATTN_BOUND_SLACK = 1.01
ATTN_BOUND_MARGIN = 0.01
ATTN_MIN_EXPONENT = -100.0
V7X_VMEM_LIMIT = 56 * 1024 * 1024

BF16 = jnp.bfloat16
F32 = jnp.float32


def _dot(a, b):
    return jnp.dot(a, b, preferred_element_type=F32)


def _sigmoid(x):
    return 1.0 / (1.0 + jnp.exp(-x))


def _ada_kernel(c_ref, w_ref, b_ref, o_ref):
    c = c_ref[...]
    sc = (c * _sigmoid(c)).astype(BF16)
    o_ref[...] = _dot(sc, w_ref[...].astype(BF16)) + b_ref[...]


def _ada_call(c_pad, w_ada, b_ada):
    rows, d = c_pad.shape
    n = w_ada.shape[1]
    return pl.pallas_call(
        _ada_kernel,
        grid=(n // ADA_BLOCK_N,),
        in_specs=[
            pl.BlockSpec((rows, d), lambda j: (0, 0)),
            pl.BlockSpec((d, ADA_BLOCK_N), lambda j: (0, j)),
            pl.BlockSpec((1, ADA_BLOCK_N), lambda j: (0, j)),
        ],
        out_specs=pl.BlockSpec((rows, ADA_BLOCK_N), lambda j: (0, j)),
        out_shape=jax.ShapeDtypeStruct((rows, n), F32),
        compiler_params=pltpu.CompilerParams(
            dimension_semantics=("arbitrary",), vmem_limit_bytes=V7X_VMEM_LIMIT),
        name="ada_mod",
    )(c_pad, w_ada, b_ada)


def _pre_kernel(x_ref, mod_ref, g_ref, win_ref, cw_ref, woa_ref, sel_ref,
                ma_ref, sgb_ref, qt_ref, k_ref, vt_ref, ksq_ref, carry_ref,
                *, tiles_per_seq, d_conv, d_attn, d_model):
    i = pl.program_id(0)
    tm = x_ref.shape[0]

    x = x_ref[...]
    mod = mod_ref[0]
    xn = x * lax.rsqrt(jnp.mean(x * x, axis=-1, keepdims=True) + EPS)
    h = (xn * g_ref[...]) * (1.0 + mod[1:2, :]) + mod[0:1, :]
    hb = h.astype(BF16)

    c0, c1 = 0, 3 * d_conv
    pc = _dot(hb, win_ref[:, c0:c1])
    u = pc[:, 0:d_conv]
    gb = pc[:, d_conv:2 * d_conv]
    gc = pc[:, 2 * d_conv:3 * d_conv]
    cu = gc * u

    @pl.when(i % tiles_per_seq == 0)
    def _():
        carry_ref[...] = jnp.zeros_like(carry_ref)

    ext = jnp.concatenate([carry_ref[...], cu], axis=0)
    prev1 = pltpu.roll(ext, 1, axis=0)[8:, :]
    prev2 = pltpu.roll(ext, 2, axis=0)[8:, :]
    carry_ref[...] = cu[tm - 8:, :]
    cw = cw_ref[...]
    z = cw[0:1, :] * prev2 + cw[1:2, :] * prev1 + cw[2:3, :] * cu
    y_a = _dot((gb * z).astype(BF16), woa_ref[...])

    g0 = 3 * d_conv + 3 * d_attn
    pg = _dot(hb, win_ref[:, g0:g0 + 2 * d_model])
    ma_ref[...] = (_sigmoid(pg[:, 0:d_model]) * y_a).astype(BF16)
    sgb_ref[...] = _sigmoid(pg[:, d_model:2 * d_model]).astype(BF16)

    a0 = 3 * d_conv
    pa = _dot(hb, win_ref[:, a0:a0 + 3 * d_attn])
    q = pa[:, 0:d_attn] * (LOG2E / math.sqrt(DH))
    qt_ref[0] = q.T.astype(BF16)
    kb = pa[:, d_attn:2 * d_attn].astype(BF16)
    k_ref[...] = kb
    kf = kb.astype(F32)
    k_sq = _dot((kf * kf).astype(BF16), sel_ref[...])
    ksq_ref[0] = jnp.broadcast_to(jnp.max(k_sq, axis=0, keepdims=True), ksq_ref.shape[1:])
    v = pa[:, 2 * d_attn:3 * d_attn]
    for s in range(tm // ATTN_KBLOCK):
        vt_ref[0, s] = v[s * ATTN_KBLOCK:(s + 1) * ATTN_KBLOCK, :].T.astype(BF16)


def _pre_call(x2, mod3, g_mix, w_in_b, conv_w, w_out_a_b, *, batch, seq):
    t, d = x2.shape
    d_conv = conv_w.shape[1]
    d_attn = N_HEADS * DV
    tm = PRE_BLOCK_M
    tps = seq // tm
    nkb = seq // ATTN_KBLOCK
    kern = functools.partial(_pre_kernel, tiles_per_seq=tps, d_conv=d_conv,
                             d_attn=d_attn, d_model=d)
    full = lambda a: pl.BlockSpec(a.shape, lambda i: (0,) * a.ndim)
    sel = jnp.asarray(np.arange(d_attn)[:, None] // DH == np.arange(128)[None, :], BF16)
    return pl.pallas_call(
        kern,
        grid=(t // tm,),
        in_specs=[
            pl.BlockSpec((tm, d), lambda i: (i, 0)),
            pl.BlockSpec((1,) + mod3.shape[1:], lambda i: (i // tps, 0, 0)),
            full(g_mix), full(w_in_b), full(conv_w), full(w_out_a_b), full(sel),
        ],
        out_specs=[
            pl.BlockSpec((tm, d), lambda i: (i, 0)),
            pl.BlockSpec((tm, d), lambda i: (i, 0)),
            pl.BlockSpec((1, d_attn, tm), lambda i: (i // tps, 0, i % tps)),
            pl.BlockSpec((tm, d_attn), lambda i: (i, 0)),
            pl.BlockSpec((1, tm // ATTN_KBLOCK, d_attn, ATTN_KBLOCK),
                         lambda i: (i // tps, i % tps, 0, 0)),
            pl.BlockSpec((1, 8, 128), lambda i: (i, 0, 0)),
        ],
        out_shape=[
            jax.ShapeDtypeStruct((t, d), BF16),
            jax.ShapeDtypeStruct((t, d), BF16),
            jax.ShapeDtypeStruct((batch, d_attn, seq), BF16),
            jax.ShapeDtypeStruct((t, d_attn), BF16),
            jax.ShapeDtypeStruct((batch, nkb, d_attn, ATTN_KBLOCK), BF16),
            jax.ShapeDtypeStruct((t // tm, 8, 128), F32),
        ],
        scratch_shapes=[pltpu.VMEM((8, d_conv), F32)],
        compiler_params=pltpu.CompilerParams(
            dimension_semantics=("arbitrary",), vmem_limit_bytes=V7X_VMEM_LIMIT),
        name="mixer_in",
    )(x2, mod3, g_mix, w_in_b, conv_w, w_out_a_b, sel)


def _attn_kernel(slope_ref, qt_ref, k_ref, vt_ref, ksq_ref, bias_ref, lam_ref, g_ref,
                 o_ref, acc_ref, s_ref, p_ref):
    h = pl.program_id(1)
    qi = pl.program_id(2)
    tq, tk = ATTN_QBLOCK, ATTN_KBLOCK
    ratio = tq // tk
    slope = slope_ref[h]

    qt = qt_ref[0]
    row = lax.broadcasted_iota(jnp.int32, qt.shape, 0)
    zero = jnp.zeros_like(qt)
    qmaps = (jnp.where(row < DH, qt, zero), jnp.where(row >= DH, qt, zero))
    ones = jnp.ones((ATTN_ONES_ROWS, tk), BF16)

    last = ratio * qi + ratio - 1

    def key_block(kj):
        return k_ref[0, pl.ds(pl.multiple_of(kj * tk, tk), tk), :]

    def bias_tile(kj):
        return bias_ref[0, jnp.clip(kj - ratio * qi + 1, 0, ratio)]

    def bias_shift(kj):
        return -slope * (qi * tq - kj * tk).astype(F32)

    def accumulate(kj, slot, alphas=None):
        vb = jnp.concatenate([vt_ref[0, kj], ones], axis=0)
        for c in range(2):
            old = acc_ref[c] if alphas is None else alphas[c] * acc_ref[c]
            acc_ref[c] = old + _dot(vb, p_ref[slot, c])

    k_sq_all = jnp.max(ksq_ref[0], axis=0)
    lane = lax.broadcasted_iota(jnp.int32, k_sq_all.shape, 1)
    stab = []
    for c in range(2):
        k_sq = jnp.max(jnp.where(lane == 2 * h + c, k_sq_all, 0.0), axis=-1, keepdims=True)[0:1]
        qf = qmaps[c].astype(F32)
        q_sq = jnp.sum(qf * qf, axis=0, keepdims=True)
        stab.append(jnp.sqrt(q_sq * k_sq) * ATTN_BOUND_SLACK + ATTN_BOUND_MARGIN)

    def exponentials_into_scratch(kj, slot, e_max):
        kb = key_block(kj)
        bias = bias_tile(kj)
        shift = bias_shift(kj)
        out = []
        for c in range(2):
            e = _dot(kb, qmaps[c]) + bias + (shift - stab[c])
            p_ref[slot, c] = jnp.exp2(e).astype(BF16)
            out.append(jnp.maximum(e_max[c], jnp.max(e, axis=0, keepdims=True)))
        return tuple(out)

    acc_ref[...] = jnp.zeros_like(acc_ref)
    e0 = jnp.full((1, tq), NEG_INF, F32)

    def fast_body(t, e_max):
        e_max = exponentials_into_scratch(2 * t + 1, 1, e_max)
        accumulate(2 * t, 0)
        e_max = exponentials_into_scratch(2 * t + 2, 0, e_max)
        accumulate(2 * t + 1, 1)
        return e_max

    e_max = lax.fori_loop(0, (last - 1) // 2, fast_body,
                          exponentials_into_scratch(0, 0, (e0, e0)))
    e_max = exponentials_into_scratch(last, 1, e_max)
    accumulate(last - 1, 0)
    accumulate(last, 1)
    smallest = jnp.minimum(jnp.min(e_max[0]), jnp.min(e_max[1]))

    @pl.when(smallest < ATTN_MIN_EXPONENT)
    def _():
        def body(kj, m_run):
            kb = key_block(kj)
            bias = bias_tile(kj)
            shift = bias_shift(kj)
            m_new, alpha = [], []
            for c in range(2):
                s = _dot(kb, qmaps[c]) + bias
                s_ref[c] = s
                m_c = jnp.maximum(m_run[c], jnp.max(s, axis=0, keepdims=True) + shift)
                alpha.append(jnp.exp2(m_run[c] - m_c))
                p_ref[0, c] = jnp.exp2(s_ref[c] - (m_c - shift)).astype(BF16)
                m_new.append(m_c)
            accumulate(kj, 0, alpha)
            return tuple(m_new)

        acc_ref[...] = jnp.zeros_like(acc_ref)
        lax.fori_loop(0, last + 1, body, (e0, e0))

    lp = lam_ref[...]
    lam = (jnp.exp(jnp.sum(lp[0:1, :] * lp[1:2, :], axis=-1, keepdims=True))
           - jnp.exp(jnp.sum(lp[2:3, :] * lp[3:4, :], axis=-1, keepdims=True))
           + LAMBDA_INIT)
    a1 = acc_ref[0]
    a2 = acc_ref[1]
    ot = (a1[0:DV, :] * (1.0 / a1[DV:DV + 1, :])
          - (lam * a2[0:DV, :]) * (1.0 / a2[DV:DV + 1, :]))
    o = ot.T
    o = o * lax.rsqrt(jnp.mean(o * o, axis=-1, keepdims=True) + EPS)
    o_ref[0] = ((o * g_ref[...]) * (1.0 - LAMBDA_INIT)).astype(BF16)


def _attn_bias_tiles():
    tq, tk = ATTN_QBLOCK, ATTN_KBLOCK
    jj = np.arange(tk, dtype=np.int64)[:, None]
    ii = np.arange(tq, dtype=np.int64)[None, :]
    slopes = np.asarray(ALIBI_SLOPES, np.float64)[:, None, None]
    tiles = [-slopes * (ii - jj)[None].astype(np.float64)]
    for r in range(tq // tk):
        key = r * tk + jj
        allowed = (key // CHUNK) <= (ii // CHUNK)
        tiles.append(np.where(allowed[None], -slopes * (np.abs(ii - key)[None] + r * tk), NEG_INF))
    return (LOG2E * np.stack(tiles, axis=1)).astype(np.float32)


def _attn_call(qt, k3, vt, k_sq, lam_params, subln_g, *, batch, seq):
    tq, tk = ATTN_QBLOCK, ATTN_KBLOCK
    nkb = seq // tk
    d_attn = N_HEADS * DV
    bias = jnp.asarray(_attn_bias_tiles())
    slopes = jnp.asarray([LOG2E * a for a in ALIBI_SLOPES], F32)
    return pl.pallas_call(
        _attn_kernel,
        grid=(batch, N_HEADS, seq // tq),
        in_specs=[
            pl.BlockSpec(memory_space=pltpu.SMEM),
            pl.BlockSpec((1, DV, tq), lambda b, h, q: (b, h, q)),
            pl.BlockSpec((1, seq, DV), lambda b, h, q: (b, 0, h)),
            pl.BlockSpec((1, nkb, DV, tk), lambda b, h, q: (b, 0, h, 0)),
            pl.BlockSpec((1,) + k_sq.shape[1:], lambda b, h, q: (b, 0, 0, 0)),
            pl.BlockSpec((1,) + bias.shape[1:], lambda b, h, q: (h, 0, 0, 0)),
            pl.BlockSpec(lam_params.shape, lambda b, h, q: (0, 0)),
            pl.BlockSpec(subln_g.shape, lambda b, h, q: (0, 0)),
        ],
        out_specs=pl.BlockSpec((1, tq, DV), lambda b, h, q: (b, q, h)),
        out_shape=jax.ShapeDtypeStruct((batch, seq, d_attn), BF16),
        scratch_shapes=[pltpu.VMEM((2, DV + ATTN_ONES_ROWS, tq), F32),
                        pltpu.VMEM((2, tk, tq), F32),
                        pltpu.VMEM((2, 2, tk, tq), BF16)],
        compiler_params=pltpu.CompilerParams(
            dimension_semantics=("arbitrary", "arbitrary", "arbitrary"),
            vmem_limit_bytes=V7X_VMEM_LIMIT),
        name="diff_attn",
    )(slopes, qt, k3, vt, k_sq, bias, lam_params, subln_g)


def _post_kernel(x_ref, o_ref, ma_ref, sgb_ref, mod_ref, gf_ref, gfin_ref,
                 wob_ref, wo_ref, wg_ref, wu_ref, wd_ref, y_ref):
    mod = mod_ref[0]
    y_b = _dot(o_ref[...], wob_ref[...])
    merged = ma_ref[...].astype(F32) + sgb_ref[...].astype(F32) * y_b
    x1 = x_ref[...] + mod[2:3, :] * _dot(merged.astype(BF16), wo_ref[...])

    xn = x1 * lax.rsqrt(jnp.mean(x1 * x1, axis=-1, keepdims=True) + EPS)
    h2 = ((xn * gf_ref[...]) * (1.0 + mod[4:5, :]) + mod[3:4, :]).astype(BF16)
    gate = _dot(h2, wg_ref[...])
    up = _dot(h2, wu_ref[...])
    act = ((gate * _sigmoid(gate)) * up).astype(BF16)
    x2 = x1 + mod[5:6, :] * _dot(act, wd_ref[...])

    y_ref[...] = (x2 * lax.rsqrt(jnp.mean(x2 * x2, axis=-1, keepdims=True) + EPS)) * gfin_ref[...]


def _post_call(x2, o2, ma, sgb, mod3, g_ffn, g_final, wob, wo, wg, wu, wd, *, seq):
    t, d = x2.shape
    tm = POST_BLOCK_M
    tps = seq // tm
    full = lambda a: pl.BlockSpec(a.shape, lambda i: (0,) * a.ndim)
    row = lambda a: pl.BlockSpec((tm, a.shape[1]), lambda i: (i, 0))
    return pl.pallas_call(
        _post_kernel,
        grid=(t // tm,),
        in_specs=[
            row(x2), row(o2), row(ma), row(sgb),
            pl.BlockSpec((1,) + mod3.shape[1:], lambda i: (i // tps, 0, 0)),
            full(g_ffn), full(g_final), full(wob), full(wo), full(wg), full(wu), full(wd),
        ],
        out_specs=pl.BlockSpec((tm, d), lambda i: (i, 0)),
        out_shape=jax.ShapeDtypeStruct((t, d), F32),
        compiler_params=pltpu.CompilerParams(
            dimension_semantics=("arbitrary",), vmem_limit_bytes=V7X_VMEM_LIMIT),
        name="mixer_out_ffn",
    )(x2, o2, ma, sgb, mod3, g_ffn, g_final, wob, wo, wg, wu, wd)


def kernel(x, c, w_ada, b_ada, g_mix, w_in, conv_w, w_out_a, lambda_q1, lambda_k1, lambda_q2,
           lambda_k2, subln_g, w_out_b, w_out, g_ffn, w_gate, w_up, w_down, g_final):
    batch, seq, d = x.shape
    depth = w_ada.shape[0]
    assert depth == 1 and seq % PRE_BLOCK_M == 0 and seq % POST_BLOCK_M == 0
    assert PRE_BLOCK_M % ATTN_KBLOCK == 0 and ATTN_KBLOCK % CHUNK == 0
    assert ATTN_QBLOCK % (2 * ATTN_KBLOCK) == 0 and seq % ATTN_QBLOCK == 0
    l = 0
    x2 = x.reshape(batch * seq, d)

    c_pad = jnp.pad(c, ((0, 8 - batch), (0, 0)))
    mod = _ada_call(c_pad, w_ada[l], b_ada[l].reshape(1, -1))
    mod3 = mod[:batch].reshape(batch, 6, d)

    ma, sgb, qt, k2, vt, k_sq = _pre_call(
        x2, mod3, g_mix[l].reshape(1, d), w_in[l].astype(BF16), conv_w[l],
        w_out_a[l].astype(BF16), batch=batch, seq=seq)

    lam_params = jnp.stack([lambda_q1[l], lambda_k1[l], lambda_q2[l], lambda_k2[l]], axis=0)
    o = _attn_call(qt, k2.reshape(batch, seq, -1), vt, k_sq.reshape(batch, -1, 8, 128),
                   lam_params, subln_g[l].reshape(1, DV), batch=batch, seq=seq)

    y = _post_call(
        x2, o.reshape(batch * seq, -1), ma, sgb, mod3, g_ffn[l].reshape(1, d),
        g_final.reshape(1, d), w_out_b[l].astype(BF16), w_out[l].astype(BF16),
        w_gate[l].astype(BF16), w_up[l].astype(BF16), w_down[l].astype(BF16), seq=seq)
    return y.reshape(batch, seq, d)
```

```python
import functools
import math

import jax
import jax.numpy as jnp
import ml_dtypes
import numpy as np
from jax import lax
from jax.experimental import pallas as pl
from jax.experimental.pallas import tpu as pltpu

CHUNK = 64
CONV_WIDTH = 3
N_HEADS = 4
DH = 64
DV = 2 * DH
EPS = 1e-6
NEG_INF = -1e30
LAMBDA_INIT = 0.8 - 0.6 * math.exp(-0.3 * 1)
LOG2E = math.log2(math.e)
ALIBI_SLOPES = tuple(2.0 ** (-8.0 * (i + 1) / N_HEADS) for i in range(N_HEADS))

ADA_BLOCK_N = 1536
PRE_BLOCK_M = 512
POST_BLOCK_M = 512
ATTN_KBLOCK = 512
ATTN_QBLOCK = 1024
ATTN_ONES_ROWS = 16
ATTN_POS_COLS = 3
ATTN_BOUND_SLACK = 1.01
ATTN_BOUND_MARGIN = 0.01
ATTN_MIN_EXPONENT = -100.0
V7X_VMEM_LIMIT = 56 * 1024 * 1024

BF16 = jnp.bfloat16
F32 = jnp.float32


def _dot(a, b):
    return jnp.dot(a, b, preferred_element_type=F32)


def _sigmoid(x):
    return 1.0 / (1.0 + jnp.exp(-x))


def _ada_kernel(c_ref, w_ref, b_ref, o_ref):
    c = c_ref[...]
    sc = (c * _sigmoid(c)).astype(BF16)
    o_ref[...] = _dot(sc, w_ref[...].astype(BF16)) + b_ref[...]


def _ada_call(c_pad, w_ada, b_ada):
    rows, d = c_pad.shape
    n = w_ada.shape[1]
    return pl.pallas_call(
        _ada_kernel,
        grid=(n // ADA_BLOCK_N,),
        in_specs=[
            pl.BlockSpec((rows, d), lambda j: (0, 0)),
            pl.BlockSpec((d, ADA_BLOCK_N), lambda j: (0, j)),
            pl.BlockSpec((1, ADA_BLOCK_N), lambda j: (0, j)),
        ],
        out_specs=pl.BlockSpec((rows, ADA_BLOCK_N), lambda j: (0, j)),
        out_shape=jax.ShapeDtypeStruct((rows, n), F32),
        compiler_params=pltpu.CompilerParams(
            dimension_semantics=("arbitrary",), vmem_limit_bytes=V7X_VMEM_LIMIT),
        name="ada_mod",
    )(c_pad, w_ada, b_ada)


def _pre_kernel(x_ref, mod_ref, g_ref, win_ref, cw_ref, woa_ref, sel_ref,
                ma_ref, sgb_ref, qt_ref, k_ref, vt_ref, ksq_ref, carry_ref,
                *, tiles_per_seq, d_conv, d_attn, d_model):
    i = pl.program_id(0)
    tm = x_ref.shape[0]

    x = x_ref[...]
    mod = mod_ref[0]
    xn = x * lax.rsqrt(jnp.mean(x * x, axis=-1, keepdims=True) + EPS)
    h = (xn * g_ref[...]) * (1.0 + mod[1:2, :]) + mod[0:1, :]
    hb = h.astype(BF16)

    c0, c1 = 0, 3 * d_conv
    pc = _dot(hb, win_ref[:, c0:c1])
    u = pc[:, 0:d_conv]
    gb = pc[:, d_conv:2 * d_conv]
    gc = pc[:, 2 * d_conv:3 * d_conv]
    cu = gc * u

    @pl.when(i % tiles_per_seq == 0)
    def _():
        carry_ref[...] = jnp.zeros_like(carry_ref)

    ext = jnp.concatenate([carry_ref[...], cu], axis=0)
    prev1 = pltpu.roll(ext, 1, axis=0)[8:, :]
    prev2 = pltpu.roll(ext, 2, axis=0)[8:, :]
    carry_ref[...] = cu[tm - 8:, :]
    cw = cw_ref[...]
    z = cw[0:1, :] * prev2 + cw[1:2, :] * prev1 + cw[2:3, :] * cu
    y_a = _dot((gb * z).astype(BF16), woa_ref[...])

    g0 = 3 * d_conv + 3 * d_attn
    pg = _dot(hb, win_ref[:, g0:g0 + 2 * d_model])
    ma_ref[...] = (_sigmoid(pg[:, 0:d_model]) * y_a).astype(BF16)
    sgb_ref[...] = _sigmoid(pg[:, d_model:2 * d_model]).astype(BF16)

    a0 = 3 * d_conv
    pa = _dot(hb, win_ref[:, a0:a0 + 3 * d_attn])
    q = pa[:, 0:d_attn] * (LOG2E / math.sqrt(DH))
    qt_ref[0] = q.T.astype(BF16)
    kb = pa[:, d_attn:2 * d_attn].astype(BF16)
    k_ref[...] = kb
    kf = kb.astype(F32)
    k_sq = _dot((kf * kf).astype(BF16), sel_ref[...])
    ksq_ref[0] = jnp.broadcast_to(jnp.max(k_sq, axis=0, keepdims=True), ksq_ref.shape[1:])
    v = pa[:, 2 * d_attn:3 * d_attn]
    for s in range(tm // ATTN_KBLOCK):
        vt_ref[0, s] = v[s * ATTN_KBLOCK:(s + 1) * ATTN_KBLOCK, :].T.astype(BF16)


def _pre_call(x2, mod3, g_mix, w_in_b, conv_w, w_out_a_b, *, batch, seq):
    t, d = x2.shape
    d_conv = conv_w.shape[1]
    d_attn = N_HEADS * DV
    tm = PRE_BLOCK_M
    tps = seq // tm
    nkb = seq // ATTN_KBLOCK
    kern = functools.partial(_pre_kernel, tiles_per_seq=tps, d_conv=d_conv,
                             d_attn=d_attn, d_model=d)
    full = lambda a: pl.BlockSpec(a.shape, lambda i: (0,) * a.ndim)
    sel = jnp.asarray(np.arange(d_attn)[:, None] // DH == np.arange(128)[None, :], BF16)
    return pl.pallas_call(
        kern,
        grid=(t // tm,),
        in_specs=[
            pl.BlockSpec((tm, d), lambda i: (i, 0)),
            pl.BlockSpec((1,) + mod3.shape[1:], lambda i: (i // tps, 0, 0)),
            full(g_mix), full(w_in_b), full(conv_w), full(w_out_a_b), full(sel),
        ],
        out_specs=[
            pl.BlockSpec((tm, d), lambda i: (i, 0)),
            pl.BlockSpec((tm, d), lambda i: (i, 0)),
            pl.BlockSpec((1, d_attn, tm), lambda i: (i // tps, 0, i % tps)),
            pl.BlockSpec((tm, d_attn), lambda i: (i, 0)),
            pl.BlockSpec((1, tm // ATTN_KBLOCK, d_attn, ATTN_KBLOCK),
                         lambda i: (i // tps, i % tps, 0, 0)),
            pl.BlockSpec((1, 8, 128), lambda i: (i, 0, 0)),
        ],
        out_shape=[
            jax.ShapeDtypeStruct((t, d), BF16),
            jax.ShapeDtypeStruct((t, d), BF16),
            jax.ShapeDtypeStruct((batch, d_attn, seq), BF16),
            jax.ShapeDtypeStruct((t, d_attn), BF16),
            jax.ShapeDtypeStruct((batch, nkb, d_attn, ATTN_KBLOCK), BF16),
            jax.ShapeDtypeStruct((t // tm, 8, 128), F32),
        ],
        scratch_shapes=[pltpu.VMEM((8, d_conv), F32)],
        compiler_params=pltpu.CompilerParams(
            dimension_semantics=("arbitrary",), vmem_limit_bytes=V7X_VMEM_LIMIT),
        name="mixer_in",
    )(x2, mod3, g_mix, w_in_b, conv_w, w_out_a_b, sel)


def _attn_kernel(slope_ref, qt_ref, k_ref, vt_ref, ksq_ref, bias_ref, kpos_ref, lam_ref, g_ref,
                 o_ref, acc_ref, s_ref, p_ref):
    h = pl.program_id(1)
    qi = pl.program_id(2)
    tq, tk = ATTN_QBLOCK, ATTN_KBLOCK
    ratio = tq // tk
    slope = slope_ref[h]

    qt = qt_ref[0]
    row = lax.broadcasted_iota(jnp.int32, qt.shape, 0)
    zero = jnp.zeros_like(qt)
    qmaps = (jnp.where(row < DH, qt, zero), jnp.where(row >= DH, qt, zero))
    ones = jnp.ones((ATTN_ONES_ROWS, tk), BF16)

    last = ratio * qi + ratio - 1

    def key_block(kj):
        return k_ref[0, pl.ds(pl.multiple_of(kj * tk, tk), tk), :]

    def bias_tile(kj):
        return bias_ref[0, jnp.clip(kj - ratio * qi + 1, 0, ratio)]

    def bias_shift(kj):
        return -slope * (qi * tq - kj * tk).astype(F32)

    def accumulate(kj, slot, alphas=None):
        vb = jnp.concatenate([vt_ref[0, kj], ones], axis=0)
        for c in range(2):
            old = acc_ref[c] if alphas is None else alphas[c] * acc_ref[c]
            acc_ref[c] = old + _dot(vb, p_ref[slot, c])

    k_sq_all = jnp.max(ksq_ref[0], axis=0)
    lane = lax.broadcasted_iota(jnp.int32, k_sq_all.shape, 1)
    stab = []
    for c in range(2):
        k_sq = jnp.max(jnp.where(lane == 2 * h + c, k_sq_all, 0.0), axis=-1, keepdims=True)[0:1]
        qf = qmaps[c].astype(F32)
        q_sq = jnp.sum(qf * qf, axis=0, keepdims=True)
        stab.append(jnp.sqrt(q_sq * k_sq) * ATTN_BOUND_SLACK + ATTN_BOUND_MARGIN)

    def store_exponentials(e, slot, c, e_max):
        p_ref[slot, c] = jnp.exp2(e).astype(BF16)
        return jnp.maximum(e_max, jnp.max(e, axis=0, keepdims=True))

    def exponentials_diagonal(kj, slot, e_max):
        kb = key_block(kj)
        bias = bias_tile(kj)
        shift = bias_shift(kj)
        return tuple(store_exponentials(_dot(kb, qmaps[c]) + bias + (shift - stab[c]), slot, c,
                                        e_max[c]) for c in range(2))

    lane_k = lax.broadcasted_iota(jnp.int32, (tk, 2 * DH), 1)
    ones_rows = (((row >= DH) & (row < DH + ATTN_POS_COLS)).astype(F32).astype(BF16),
                 (row < ATTN_POS_COLS).astype(F32).astype(BF16))
    q_aug = tuple(qmaps[c] + ones_rows[c] for c in range(2))
    pos = lax.broadcasted_iota(jnp.int32, (1, tq), 1).astype(F32)
    base = tuple(stab[c] + slope * pos for c in range(2))

    def exponentials_below(kj, slot, e_max):
        kb = key_block(kj)
        lhs = (jnp.where(lane_k < DH, kb, kpos_ref[0, 0]), jnp.where(lane_k >= DH, kb, kpos_ref[0, 1]))
        shift = bias_shift(kj)
        return tuple(store_exponentials(_dot(lhs[c], q_aug[c]) - (base[c] - shift), slot, c,
                                        e_max[c]) for c in range(2))

    acc_ref[...] = jnp.zeros_like(acc_ref)
    e0 = jnp.full((1, tq), NEG_INF, F32)

    def first_query_block():
        e_max = exponentials_diagonal(0, 0, (e0, e0))
        e_max = exponentials_diagonal(1, 1, e_max)
        accumulate(0, 0)
        accumulate(1, 1)
        return e_max

    def later_query_block():
        def trip(t, e_max):
            e_max = exponentials_below(2 * t + 1, 1, e_max)
            accumulate(2 * t, 0)
            e_max = exponentials_below(2 * t + 2, 0, e_max)
            accumulate(2 * t + 1, 1)
            return e_max

        e_max = lax.fori_loop(0, qi - 1, trip, exponentials_below(0, 0, (e0, e0)))
        d0 = 2 * qi
        e_max = exponentials_below(d0 - 1, 1, e_max)
        accumulate(d0 - 2, 0)
        e_max = exponentials_diagonal(d0, 0, e_max)
        accumulate(d0 - 1, 1)
        e_max = exponentials_diagonal(d0 + 1, 1, e_max)
        accumulate(d0, 0)
        accumulate(d0 + 1, 1)
        return e_max

    e_max = lax.cond(qi == 0, first_query_block, later_query_block)
    smallest = jnp.minimum(jnp.min(e_max[0]), jnp.min(e_max[1]))

    @pl.when(smallest < ATTN_MIN_EXPONENT)
    def _():
        def body(kj, m_run):
            kb = key_block(kj)
            bias = bias_tile(kj)
            shift = bias_shift(kj)
            m_new, alpha = [], []
            for c in range(2):
                s = _dot(kb, qmaps[c]) + bias
                s_ref[c] = s
                m_c = jnp.maximum(m_run[c], jnp.max(s, axis=0, keepdims=True) + shift)
                alpha.append(jnp.exp2(m_run[c] - m_c))
                p_ref[0, c] = jnp.exp2(s_ref[c] - (m_c - shift)).astype(BF16)
                m_new.append(m_c)
            accumulate(kj, 0, alpha)
            return tuple(m_new)

        acc_ref[...] = jnp.zeros_like(acc_ref)
        lax.fori_loop(0, last + 1, body, (e0, e0))

    lp = lam_ref[...]
    lam = (jnp.exp(jnp.sum(lp[0:1, :] * lp[1:2, :], axis=-1, keepdims=True))
           - jnp.exp(jnp.sum(lp[2:3, :] * lp[3:4, :], axis=-1, keepdims=True))
           + LAMBDA_INIT)
    a1 = acc_ref[0]
    a2 = acc_ref[1]
    ot = (a1[0:DV, :] * (1.0 / a1[DV:DV + 1, :])
          - (lam * a2[0:DV, :]) * (1.0 / a2[DV:DV + 1, :]))
    o = ot.T
    o = o * lax.rsqrt(jnp.mean(o * o, axis=-1, keepdims=True) + EPS)
    o_ref[0] = ((o * g_ref[...]) * (1.0 - LAMBDA_INIT)).astype(BF16)


def _attn_bias_tiles():
    tq, tk = ATTN_QBLOCK, ATTN_KBLOCK
    jj = np.arange(tk, dtype=np.int64)[:, None]
    ii = np.arange(tq, dtype=np.int64)[None, :]
    slopes = np.asarray(ALIBI_SLOPES, np.float64)[:, None, None]
    tiles = [-slopes * (ii - jj)[None].astype(np.float64)]
    for r in range(tq // tk):
        key = r * tk + jj
        allowed = (key // CHUNK) <= (ii // CHUNK)
        tiles.append(np.where(allowed[None], -slopes * (np.abs(ii - key)[None] + r * tk), NEG_INF))
    return (LOG2E * np.stack(tiles, axis=1)).astype(np.float32)


def _attn_key_position_columns():
    tk = ATTN_KBLOCK
    rest = LOG2E * np.asarray(ALIBI_SLOPES, np.float64)[:, None] * np.arange(tk, dtype=np.float64)[None, :]
    out = np.zeros((N_HEADS, 2, tk, 2 * DH), np.float32)
    for n in range(ATTN_POS_COLS):
        part = rest.astype(ml_dtypes.bfloat16).astype(np.float64)
        out[:, 0, :, DH + n] = part
        out[:, 1, :, n] = part
        rest = rest - part
    return out.astype(ml_dtypes.bfloat16)


def _attn_call(qt, k3, vt, k_sq, lam_params, subln_g, *, batch, seq):
    tq, tk = ATTN_QBLOCK, ATTN_KBLOCK
    nkb = seq // tk
    d_attn = N_HEADS * DV
    bias = jnp.asarray(_attn_bias_tiles())
    kpos = jnp.asarray(_attn_key_position_columns())
    slopes = jnp.asarray([LOG2E * a for a in ALIBI_SLOPES], F32)
    return pl.pallas_call(
        _attn_kernel,
        grid=(batch, N_HEADS, seq // tq),
        in_specs=[
            pl.BlockSpec(memory_space=pltpu.SMEM),
            pl.BlockSpec((1, DV, tq), lambda b, h, q: (b, h, q)),
            pl.BlockSpec((1, seq, DV), lambda b, h, q: (b, 0, h)),
            pl.BlockSpec((1, nkb, DV, tk), lambda b, h, q: (b, 0, h, 0)),
            pl.BlockSpec((1,) + k_sq.shape[1:], lambda b, h, q: (b, 0, 0, 0)),
            pl.BlockSpec((1,) + bias.shape[1:], lambda b, h, q: (h, 0, 0, 0)),
            pl.BlockSpec((1,) + kpos.shape[1:], lambda b, h, q: (h, 0, 0, 0)),
            pl.BlockSpec(lam_params.shape, lambda b, h, q: (0, 0)),
            pl.BlockSpec(subln_g.shape, lambda b, h, q: (0, 0)),
        ],
        out_specs=pl.BlockSpec((1, tq, DV), lambda b, h, q: (b, q, h)),
        out_shape=jax.ShapeDtypeStruct((batch, seq, d_attn), BF16),
        scratch_shapes=[pltpu.VMEM((2, DV + ATTN_ONES_ROWS, tq), F32),
                        pltpu.VMEM((2, tk, tq), F32),
                        pltpu.VMEM((2, 2, tk, tq), BF16)],
        compiler_params=pltpu.CompilerParams(
            dimension_semantics=("arbitrary", "arbitrary", "arbitrary"),
            vmem_limit_bytes=V7X_VMEM_LIMIT),
        name="diff_attn",
    )(slopes, qt, k3, vt, k_sq, bias, kpos, lam_params, subln_g)


def _post_kernel(x_ref, o_ref, ma_ref, sgb_ref, mod_ref, gf_ref, gfin_ref,
                 wob_ref, wo_ref, wg_ref, wu_ref, wd_ref, y_ref):
    mod = mod_ref[0]
    y_b = _dot(o_ref[...], wob_ref[...])
    merged = ma_ref[...].astype(F32) + sgb_ref[...].astype(F32) * y_b
    x1 = x_ref[...] + mod[2:3, :] * _dot(merged.astype(BF16), wo_ref[...])

    xn = x1 * lax.rsqrt(jnp.mean(x1 * x1, axis=-1, keepdims=True) + EPS)
    h2 = ((xn * gf_ref[...]) * (1.0 + mod[4:5, :]) + mod[3:4, :]).astype(BF16)
    gate = _dot(h2, wg_ref[...])
    up = _dot(h2, wu_ref[...])
    act = ((gate * _sigmoid(gate)) * up).astype(BF16)
    x2 = x1 + mod[5:6, :] * _dot(act, wd_ref[...])

    y_ref[...] = (x2 * lax.rsqrt(jnp.mean(x2 * x2, axis=-1, keepdims=True) + EPS)) * gfin_ref[...]


def _post_call(x2, o2, ma, sgb, mod3, g_ffn, g_final, wob, wo, wg, wu, wd, *, seq):
    t, d = x2.shape
    tm = POST_BLOCK_M
    tps = seq // tm
    full = lambda a: pl.BlockSpec(a.shape, lambda i: (0,) * a.ndim)
    row = lambda a: pl.BlockSpec((tm, a.shape[1]), lambda i: (i, 0))
    return pl.pallas_call(
        _post_kernel,
        grid=(t // tm,),
        in_specs=[
            row(x2), row(o2), row(ma), row(sgb),
            pl.BlockSpec((1,) + mod3.shape[1:], lambda i: (i // tps, 0, 0)),
            full(g_ffn), full(g_final), full(wob), full(wo), full(wg), full(wu), full(wd),
        ],
        out_specs=pl.BlockSpec((tm, d), lambda i: (i, 0)),
        out_shape=jax.ShapeDtypeStruct((t, d), F32),
        compiler_params=pltpu.CompilerParams(
            dimension_semantics=("arbitrary",), vmem_limit_bytes=V7X_VMEM_LIMIT),
        name="mixer_out_ffn",
    )(x2, o2, ma, sgb, mod3, g_ffn, g_final, wob, wo, wg, wu, wd)


def kernel(x, c, w_ada, b_ada, g_mix, w_in, conv_w, w_out_a, lambda_q1, lambda_k1, lambda_q2,
           lambda_k2, subln_g, w_out_b, w_out, g_ffn, w_gate, w_up, w_down, g_final):
    batch, seq, d = x.shape
    depth = w_ada.shape[0]
    assert depth == 1 and seq % PRE_BLOCK_M == 0 and seq % POST_BLOCK_M == 0
    assert PRE_BLOCK_M % ATTN_KBLOCK == 0 and ATTN_KBLOCK % CHUNK == 0
    assert ATTN_QBLOCK == 2 * ATTN_KBLOCK and seq % ATTN_QBLOCK == 0
    l = 0
    x2 = x.reshape(batch * seq, d)

    c_pad = jnp.pad(c, ((0, 8 - batch), (0, 0)))
    mod = _ada_call(c_pad, w_ada[l], b_ada[l].reshape(1, -1))
    mod3 = mod[:batch].reshape(batch, 6, d)

    ma, sgb, qt, k2, vt, k_sq = _pre_call(
        x2, mod3, g_mix[l].reshape(1, d), w_in[l].astype(BF16), conv_w[l],
        w_out_a[l].astype(BF16), batch=batch, seq=seq)

    lam_params = jnp.stack([lambda_q1[l], lambda_k1[l], lambda_q2[l], lambda_k2[l]], axis=0)
    o = _attn_call(qt, k2.reshape(batch, seq, -1), vt, k_sq.reshape(batch, -1, 8, 128),
                   lam_params, subln_g[l].reshape(1, DV), batch=batch, seq=seq)

    y = _post_call(
        x2, o.reshape(batch * seq, -1), ma, sgb, mod3, g_ffn[l].reshape(1, d),
        g_final.reshape(1, d), w_out_b[l].astype(BF16), w_out[l].astype(BF16),
        w_gate[l].astype(BF16), w_up[l].astype(BF16), w_down[l].astype(BF16), seq=seq)
    return y.reshape(batch, seq, d)
```

```python
import functools
import math

import jax
import jax.numpy as jnp
import ml_dtypes
import numpy as np
from jax import lax
from jax.experimental import pallas as pl
from jax.experimental.pallas import tpu as pltpu

CHUNK = 64
CONV_WIDTH = 3
N_HEADS = 4
DH = 64
DV = 2 * DH
EPS = 1e-6
NEG_INF = -1e30
LAMBDA_INIT = 0.8 - 0.6 * math.exp(-0.3 * 1)
LOG2E = math.log2(math.e)
ALIBI_SLOPES = tuple(2.0 ** (-8.0 * (i + 1) / N_HEADS) for i in range(N_HEADS))

ADA_BLOCK_N = 1536
PRE_BLOCK_M = 512
POST_BLOCK_M = 512
ATTN_KBLOCK = 512
ATTN_QBLOCK = 1024
ATTN_ONES_ROWS = 16
ATTN_POS_COLS = 3
ATTN_BOUND_SLACK = 1.01
ATTN_BOUND_MARGIN = 0.01
ATTN_MIN_EXPONENT = -100.0
V7X_VMEM_LIMIT = 56 * 1024 * 1024

BF16 = jnp.bfloat16
F32 = jnp.float32


def _dot(a, b):
    return jnp.dot(a, b, preferred_element_type=F32)


def _sigmoid(x):
    return 1.0 / (1.0 + jnp.exp(-x))


def _ada_kernel(c_ref, w_ref, b_ref, o_ref):
    c = c_ref[...]
    sc = (c * _sigmoid(c)).astype(BF16)
    o_ref[...] = _dot(sc, w_ref[...].astype(BF16)) + b_ref[...]


def _ada_call(c_pad, w_ada, b_ada):
    rows, d = c_pad.shape
    n = w_ada.shape[1]
    return pl.pallas_call(
        _ada_kernel,
        grid=(n // ADA_BLOCK_N,),
        in_specs=[
            pl.BlockSpec((rows, d), lambda j: (0, 0)),
            pl.BlockSpec((d, ADA_BLOCK_N), lambda j: (0, j)),
            pl.BlockSpec((1, ADA_BLOCK_N), lambda j: (0, j)),
        ],
        out_specs=pl.BlockSpec((rows, ADA_BLOCK_N), lambda j: (0, j)),
        out_shape=jax.ShapeDtypeStruct((rows, n), F32),
        compiler_params=pltpu.CompilerParams(
            dimension_semantics=("arbitrary",), vmem_limit_bytes=V7X_VMEM_LIMIT),
        name="ada_mod",
    )(c_pad, w_ada, b_ada)


def _pre_kernel(x_ref, mod_ref, g_ref, win_ref, cw_ref, woa_ref, sel_ref,
                ma_ref, sgb_ref, qt_ref, k_ref, vt_ref, ksq_ref, carry_ref,
                *, tiles_per_seq, d_conv, d_attn, d_model):
    i = pl.program_id(0)
    tm = x_ref.shape[0]

    x = x_ref[...]
    mod = mod_ref[0]
    xn = x * lax.rsqrt(jnp.mean(x * x, axis=-1, keepdims=True) + EPS)
    h = (xn * g_ref[...]) * (1.0 + mod[1:2, :]) + mod[0:1, :]
    hb = h.astype(BF16)

    c0, c1 = 0, 3 * d_conv
    pc = _dot(hb, win_ref[:, c0:c1])
    u = pc[:, 0:d_conv]
    gb = pc[:, d_conv:2 * d_conv]
    gc = pc[:, 2 * d_conv:3 * d_conv]
    cu = gc * u

    @pl.when(i % tiles_per_seq == 0)
    def _():
        carry_ref[...] = jnp.zeros_like(carry_ref)

    ext = jnp.concatenate([carry_ref[...], cu], axis=0)
    prev1 = pltpu.roll(ext, 1, axis=0)[8:, :]
    prev2 = pltpu.roll(ext, 2, axis=0)[8:, :]
    carry_ref[...] = cu[tm - 8:, :]
    cw = cw_ref[...]
    z = cw[0:1, :] * prev2 + cw[1:2, :] * prev1 + cw[2:3, :] * cu
    y_a = _dot((gb * z).astype(BF16), woa_ref[...])

    g0 = 3 * d_conv + 3 * d_attn
    pg = _dot(hb, win_ref[:, g0:g0 + 2 * d_model])
    ma_ref[...] = (_sigmoid(pg[:, 0:d_model]) * y_a).astype(BF16)
    sgb_ref[...] = _sigmoid(pg[:, d_model:2 * d_model]).astype(BF16)

    a0 = 3 * d_conv
    pa = _dot(hb, win_ref[:, a0:a0 + 3 * d_attn])
    q = pa[:, 0:d_attn] * (LOG2E / math.sqrt(DH))
    qt_ref[0] = q.T.astype(BF16)
    kb = pa[:, d_attn:2 * d_attn].astype(BF16)
    k_ref[...] = kb
    kf = kb.astype(F32)
    k_sq = _dot((kf * kf).astype(BF16), sel_ref[...])
    ksq_ref[0] = jnp.broadcast_to(jnp.max(k_sq, axis=0, keepdims=True), ksq_ref.shape[1:])
    v = pa[:, 2 * d_attn:3 * d_attn]
    for s in range(tm // ATTN_KBLOCK):
        vt_ref[0, s] = v[s * ATTN_KBLOCK:(s + 1) * ATTN_KBLOCK, :].T.astype(BF16)


def _pre_call(x2, mod3, g_mix, w_in_b, conv_w, w_out_a_b, *, batch, seq):
    t, d = x2.shape
    d_conv = conv_w.shape[1]
    d_attn = N_HEADS * DV
    tm = PRE_BLOCK_M
    tps = seq // tm
    nkb = seq // ATTN_KBLOCK
    kern = functools.partial(_pre_kernel, tiles_per_seq=tps, d_conv=d_conv,
                             d_attn=d_attn, d_model=d)
    full = lambda a: pl.BlockSpec(a.shape, lambda i: (0,) * a.ndim)
    sel = jnp.asarray(np.arange(d_attn)[:, None] // DH == np.arange(128)[None, :], BF16)
    return pl.pallas_call(
        kern,
        grid=(t // tm,),
        in_specs=[
            pl.BlockSpec((tm, d), lambda i: (i, 0)),
            pl.BlockSpec((1,) + mod3.shape[1:], lambda i: (i // tps, 0, 0)),
            full(g_mix), full(w_in_b), full(conv_w), full(w_out_a_b), full(sel),
        ],
        out_specs=[
            pl.BlockSpec((tm, d), lambda i: (i, 0)),
            pl.BlockSpec((tm, d), lambda i: (i, 0)),
            pl.BlockSpec((1, d_attn, tm), lambda i: (i // tps, 0, i % tps)),
            pl.BlockSpec((tm, d_attn), lambda i: (i, 0)),
            pl.BlockSpec((1, tm // ATTN_KBLOCK, d_attn, ATTN_KBLOCK),
                         lambda i: (i // tps, i % tps, 0, 0)),
            pl.BlockSpec((1, 8, 128), lambda i: (i, 0, 0)),
        ],
        out_shape=[
            jax.ShapeDtypeStruct((t, d), BF16),
            jax.ShapeDtypeStruct((t, d), BF16),
            jax.ShapeDtypeStruct((batch, d_attn, seq), BF16),
            jax.ShapeDtypeStruct((t, d_attn), BF16),
            jax.ShapeDtypeStruct((batch, nkb, d_attn, ATTN_KBLOCK), BF16),
            jax.ShapeDtypeStruct((t // tm, 8, 128), F32),
        ],
        scratch_shapes=[pltpu.VMEM((8, d_conv), F32)],
        compiler_params=pltpu.CompilerParams(
            dimension_semantics=("arbitrary",), vmem_limit_bytes=V7X_VMEM_LIMIT),
        name="mixer_in",
    )(x2, mod3, g_mix, w_in_b, conv_w, w_out_a_b, sel)


def _attn_kernel(slope_ref, qt_ref, k_ref, vt_ref, ksq_ref, bias_ref, kpos_ref, lam_ref, g_ref,
                 o_ref, acc_ref, s_ref, p_ref):
    h = pl.program_id(1)
    qi = pl.program_id(2)
    tq, tk = ATTN_QBLOCK, ATTN_KBLOCK
    ratio = tq // tk
    slope = slope_ref[h]

    qt = qt_ref[0]
    row = lax.broadcasted_iota(jnp.int32, qt.shape, 0)
    zero = jnp.zeros_like(qt)
    qmaps = (jnp.where(row < DH, qt, zero), jnp.where(row >= DH, qt, zero))
    ones = jnp.ones((ATTN_ONES_ROWS, tk), BF16)

    last = ratio * qi + ratio - 1

    def key_block(kj):
        return k_ref[0, pl.ds(pl.multiple_of(kj * tk, tk), tk), :]

    def bias_tile(kj):
        return bias_ref[0, jnp.clip(kj - ratio * qi + 1, 0, ratio)]

    def bias_shift(kj):
        return -slope * (qi * tq - kj * tk).astype(F32)

    def accumulate(kj, slot, alphas=None):
        vb = jnp.concatenate([vt_ref[0, kj], ones], axis=0)
        for c in range(2):
            old = acc_ref[c] if alphas is None else alphas[c] * acc_ref[c]
            acc_ref[c] = old + _dot(vb, p_ref[slot, c])

    k_sq_all = jnp.max(ksq_ref[0], axis=0)
    lane = lax.broadcasted_iota(jnp.int32, k_sq_all.shape, 1)
    stab = []
    for c in range(2):
        k_sq = jnp.max(jnp.where(lane == 2 * h + c, k_sq_all, 0.0), axis=-1, keepdims=True)[0:1]
        qf = qmaps[c].astype(F32)
        q_sq = jnp.sum(qf * qf, axis=0, keepdims=True)
        stab.append(jnp.sqrt(q_sq * k_sq) * ATTN_BOUND_SLACK + ATTN_BOUND_MARGIN)

    def store_exponentials(e, slot, c, lo, hi, e_max):
        p_ref[slot, c, :, lo:hi] = jnp.exp2(e).astype(BF16)
        return jnp.maximum(e_max, jnp.max(e, axis=0, keepdims=True))

    def exponentials_diagonal(kj, tile, slot, lo, hi, e_max):
        kb = key_block(kj)
        bias = bias_ref[0, tile, :, lo:hi]
        shift = bias_shift(kj)
        return tuple(store_exponentials(
            _dot(kb, qmaps[c][:, lo:hi]) + bias + (shift - stab[c][:, lo:hi]), slot, c, lo, hi,
            e_max[c]) for c in range(2))

    lane_k = lax.broadcasted_iota(jnp.int32, (tk, 2 * DH), 1)
    ones_rows = (((row >= DH) & (row < DH + ATTN_POS_COLS)).astype(F32).astype(BF16),
                 (row < ATTN_POS_COLS).astype(F32).astype(BF16))
    q_aug = tuple(qmaps[c] + ones_rows[c] for c in range(2))
    pos = lax.broadcasted_iota(jnp.int32, (1, tq), 1).astype(F32)
    base = tuple(stab[c] + slope * pos for c in range(2))

    def exponentials_below(kj, slot, lo, hi, e_max):
        kb = key_block(kj)
        lhs = (jnp.where(lane_k < DH, kb, kpos_ref[0, 0]), jnp.where(lane_k >= DH, kb, kpos_ref[0, 1]))
        shift = bias_shift(kj)
        return tuple(store_exponentials(
            _dot(lhs[c], q_aug[c][:, lo:hi]) - (base[c][:, lo:hi] - shift), slot, c, lo, hi,
            e_max[c]) for c in range(2))

    def accumulate_columns(kj, slot, lo, hi):
        vb = jnp.concatenate([vt_ref[0, kj], ones], axis=0)
        for c in range(2):
            acc_ref[c, :, lo:hi] = acc_ref[c, :, lo:hi] + _dot(vb, p_ref[slot, c, :, lo:hi])

    acc_ref[...] = jnp.zeros_like(acc_ref)
    e0 = jnp.full((1, tq), NEG_INF, F32)
    half = tq // 2

    def diagonal_region(e_max, pending_pv=()):
        d0 = 2 * qi
        left = tuple(e[:, 0:half] for e in e_max)
        right = tuple(e[:, half:tq] for e in e_max)
        for stage in pending_pv[0:1]:
            stage()
        left = exponentials_diagonal(d0, 1, 0, 0, half, left)
        right = exponentials_below(d0, 0, half, tq, right)
        for stage in pending_pv[1:2]:
            stage()
        right = exponentials_diagonal(d0 + 1, 2, 1, half, tq, right)
        accumulate(d0, 0)
        accumulate_columns(d0 + 1, 1, half, tq)
        return left, right

    def first_query_block():
        return diagonal_region((e0, e0))

    def later_query_block():
        def trip(t, e_max):
            e_max = exponentials_below(2 * t + 1, 1, 0, tq, e_max)
            accumulate(2 * t, 0)
            e_max = exponentials_below(2 * t + 2, 0, 0, tq, e_max)
            accumulate(2 * t + 1, 1)
            return e_max

        e_max = lax.fori_loop(0, qi - 1, trip, exponentials_below(0, 0, 0, tq, (e0, e0)))
        d0 = 2 * qi
        e_max = exponentials_below(d0 - 1, 1, 0, tq, e_max)
        return diagonal_region(e_max, (lambda: accumulate(d0 - 2, 0), lambda: accumulate(d0 - 1, 1)))

    left, right = lax.cond(qi == 0, first_query_block, later_query_block)
    smallest = functools.reduce(jnp.minimum, [jnp.min(e) for e in left + right])

    @pl.when(smallest < ATTN_MIN_EXPONENT)
    def _():
        def body(kj, m_run):
            kb = key_block(kj)
            bias = bias_tile(kj)
            shift = bias_shift(kj)
            m_new, alpha = [], []
            for c in range(2):
                s = _dot(kb, qmaps[c]) + bias
                s_ref[c] = s
                m_c = jnp.maximum(m_run[c], jnp.max(s, axis=0, keepdims=True) + shift)
                alpha.append(jnp.exp2(m_run[c] - m_c))
                p_ref[0, c] = jnp.exp2(s_ref[c] - (m_c - shift)).astype(BF16)
                m_new.append(m_c)
            accumulate(kj, 0, alpha)
            return tuple(m_new)

        acc_ref[...] = jnp.zeros_like(acc_ref)
        lax.fori_loop(0, last + 1, body, (e0, e0))

    lp = lam_ref[...]
    lam = (jnp.exp(jnp.sum(lp[0:1, :] * lp[1:2, :], axis=-1, keepdims=True))
           - jnp.exp(jnp.sum(lp[2:3, :] * lp[3:4, :], axis=-1, keepdims=True))
           + LAMBDA_INIT)
    a1 = acc_ref[0]
    a2 = acc_ref[1]
    ot = (a1[0:DV, :] * (1.0 / a1[DV:DV + 1, :])
          - (lam * a2[0:DV, :]) * (1.0 / a2[DV:DV + 1, :]))
    o = ot.T
    o = o * lax.rsqrt(jnp.mean(o * o, axis=-1, keepdims=True) + EPS)
    o_ref[0] = ((o * g_ref[...]) * (1.0 - LAMBDA_INIT)).astype(BF16)


def _attn_bias_tiles():
    tq, tk = ATTN_QBLOCK, ATTN_KBLOCK
    jj = np.arange(tk, dtype=np.int64)[:, None]
    ii = np.arange(tq, dtype=np.int64)[None, :]
    slopes = np.asarray(ALIBI_SLOPES, np.float64)[:, None, None]
    tiles = [-slopes * (ii - jj)[None].astype(np.float64)]
    for r in range(tq // tk):
        key = r * tk + jj
        allowed = (key // CHUNK) <= (ii // CHUNK)
        tiles.append(np.where(allowed[None], -slopes * (np.abs(ii - key)[None] + r * tk), NEG_INF))
    return (LOG2E * np.stack(tiles, axis=1)).astype(np.float32)


def _attn_key_position_columns():
    tk = ATTN_KBLOCK
    rest = LOG2E * np.asarray(ALIBI_SLOPES, np.float64)[:, None] * np.arange(tk, dtype=np.float64)[None, :]
    out = np.zeros((N_HEADS, 2, tk, 2 * DH), np.float32)
    for n in range(ATTN_POS_COLS):
        part = rest.astype(ml_dtypes.bfloat16).astype(np.float64)
        out[:, 0, :, DH + n] = part
        out[:, 1, :, n] = part
        rest = rest - part
    return out.astype(ml_dtypes.bfloat16)


def _attn_call(qt, k3, vt, k_sq, lam_params, subln_g, *, batch, seq):
    tq, tk = ATTN_QBLOCK, ATTN_KBLOCK
    nkb = seq // tk
    d_attn = N_HEADS * DV
    bias = jnp.asarray(_attn_bias_tiles())
    kpos = jnp.asarray(_attn_key_position_columns())
    slopes = jnp.asarray([LOG2E * a for a in ALIBI_SLOPES], F32)
    return pl.pallas_call(
        _attn_kernel,
        grid=(batch, N_HEADS, seq // tq),
        in_specs=[
            pl.BlockSpec(memory_space=pltpu.SMEM),
            pl.BlockSpec((1, DV, tq), lambda b, h, q: (b, h, q)),
            pl.BlockSpec((1, seq, DV), lambda b, h, q: (b, 0, h)),
            pl.BlockSpec((1, nkb, DV, tk), lambda b, h, q: (b, 0, h, 0)),
            pl.BlockSpec((1,) + k_sq.shape[1:], lambda b, h, q: (b, 0, 0, 0)),
            pl.BlockSpec((1,) + bias.shape[1:], lambda b, h, q: (h, 0, 0, 0)),
            pl.BlockSpec((1,) + kpos.shape[1:], lambda b, h, q: (h, 0, 0, 0)),
            pl.BlockSpec(lam_params.shape, lambda b, h, q: (0, 0)),
            pl.BlockSpec(subln_g.shape, lambda b, h, q: (0, 0)),
        ],
        out_specs=pl.BlockSpec((1, tq, DV), lambda b, h, q: (b, q, h)),
        out_shape=jax.ShapeDtypeStruct((batch, seq, d_attn), BF16),
        scratch_shapes=[pltpu.VMEM((2, DV + ATTN_ONES_ROWS, tq), F32),
                        pltpu.VMEM((2, tk, tq), F32),
                        pltpu.VMEM((2, 2, tk, tq), BF16)],
        compiler_params=pltpu.CompilerParams(
            dimension_semantics=("arbitrary", "arbitrary", "arbitrary"),
            vmem_limit_bytes=V7X_VMEM_LIMIT),
        name="diff_attn",
    )(slopes, qt, k3, vt, k_sq, bias, kpos, lam_params, subln_g)


def _post_kernel(x_ref, o_ref, ma_ref, sgb_ref, mod_ref, gf_ref, gfin_ref,
                 wob_ref, wo_ref, wg_ref, wu_ref, wd_ref, y_ref):
    mod = mod_ref[0]
    y_b = _dot(o_ref[...], wob_ref[...])
    merged = ma_ref[...].astype(F32) + sgb_ref[...].astype(F32) * y_b
    x1 = x_ref[...] + mod[2:3, :] * _dot(merged.astype(BF16), wo_ref[...])

    xn = x1 * lax.rsqrt(jnp.mean(x1 * x1, axis=-1, keepdims=True) + EPS)
    h2 = ((xn * gf_ref[...]) * (1.0 + mod[4:5, :]) + mod[3:4, :]).astype(BF16)
    gate = _dot(h2, wg_ref[...])
    up = _dot(h2, wu_ref[...])
    act = ((gate * _sigmoid(gate)) * up).astype(BF16)
    x2 = x1 + mod[5:6, :] * _dot(act, wd_ref[...])

    y_ref[...] = (x2 * lax.rsqrt(jnp.mean(x2 * x2, axis=-1, keepdims=True) + EPS)) * gfin_ref[...]


def _post_call(x2, o2, ma, sgb, mod3, g_ffn, g_final, wob, wo, wg, wu, wd, *, seq):
    t, d = x2.shape
    tm = POST_BLOCK_M
    tps = seq // tm
    full = lambda a: pl.BlockSpec(a.shape, lambda i: (0,) * a.ndim)
    row = lambda a: pl.BlockSpec((tm, a.shape[1]), lambda i: (i, 0))
    return pl.pallas_call(
        _post_kernel,
        grid=(t // tm,),
        in_specs=[
            row(x2), row(o2), row(ma), row(sgb),
            pl.BlockSpec((1,) + mod3.shape[1:], lambda i: (i // tps, 0, 0)),
            full(g_ffn), full(g_final), full(wob), full(wo), full(wg), full(wu), full(wd),
        ],
        out_specs=pl.BlockSpec((tm, d), lambda i: (i, 0)),
        out_shape=jax.ShapeDtypeStruct((t, d), F32),
        compiler_params=pltpu.CompilerParams(
            dimension_semantics=("arbitrary",), vmem_limit_bytes=V7X_VMEM_LIMIT),
        name="mixer_out_ffn",
    )(x2, o2, ma, sgb, mod3, g_ffn, g_final, wob, wo, wg, wu, wd)


def kernel(x, c, w_ada, b_ada, g_mix, w_in, conv_w, w_out_a, lambda_q1, lambda_k1, lambda_q2,
           lambda_k2, subln_g, w_out_b, w_out, g_ffn, w_gate, w_up, w_down, g_final):
    batch, seq, d = x.shape
    depth = w_ada.shape[0]
    assert depth == 1 and seq % PRE_BLOCK_M == 0 and seq % POST_BLOCK_M == 0
    assert PRE_BLOCK_M % ATTN_KBLOCK == 0 and ATTN_KBLOCK % CHUNK == 0
    assert ATTN_QBLOCK == 2 * ATTN_KBLOCK and seq % ATTN_QBLOCK == 0
    l = 0
    x2 = x.reshape(batch * seq, d)

    c_pad = jnp.pad(c, ((0, 8 - batch), (0, 0)))
    mod = _ada_call(c_pad, w_ada[l], b_ada[l].reshape(1, -1))
    mod3 = mod[:batch].reshape(batch, 6, d)

    ma, sgb, qt, k2, vt, k_sq = _pre_call(
        x2, mod3, g_mix[l].reshape(1, d), w_in[l].astype(BF16), conv_w[l],
        w_out_a[l].astype(BF16), batch=batch, seq=seq)

    lam_params = jnp.stack([lambda_q1[l], lambda_k1[l], lambda_q2[l], lambda_k2[l]], axis=0)
    o = _attn_call(qt, k2.reshape(batch, seq, -1), vt, k_sq.reshape(batch, -1, 8, 128),
                   lam_params, subln_g[l].reshape(1, DV), batch=batch, seq=seq)

    y = _post_call(
        x2, o.reshape(batch * seq, -1), ma, sgb, mod3, g_ffn[l].reshape(1, d),
        g_final.reshape(1, d), w_out_b[l].astype(BF16), w_out[l].astype(BF16),
        w_gate[l].astype(BF16), w_up[l].astype(BF16), w_down[l].astype(BF16), seq=seq)
    return y.reshape(batch, seq, d)
```

```python
import functools
import math

import jax
import jax.numpy as jnp
import ml_dtypes
import numpy as np
from jax import lax
from jax.experimental import pallas as pl
from jax.experimental.pallas import tpu as pltpu

CHUNK = 64
CONV_WIDTH = 3
N_HEADS = 4
DH = 64
DV = 2 * DH
EPS = 1e-6
NEG_INF = -1e30
LAMBDA_INIT = 0.8 - 0.6 * math.exp(-0.3 * 1)
LOG2E = math.log2(math.e)
ALIBI_SLOPES = tuple(2.0 ** (-8.0 * (i + 1) / N_HEADS) for i in range(N_HEADS))

ADA_BLOCK_N = 1536
PRE_BLOCK_M = 512
POST_BLOCK_M = 512
ATTN_KBLOCK = 512
ATTN_QBLOCK = 1024
ATTN_ONES_ROWS = 16
ATTN_POS_COLS = 3
ATTN_BOUND_SLACK = 1.01
ATTN_BOUND_MARGIN = 0.01
ATTN_MIN_EXPONENT = -100.0
V7X_VMEM_LIMIT = 56 * 1024 * 1024

BF16 = jnp.bfloat16
F32 = jnp.float32


def _dot(a, b):
    return jnp.dot(a, b, preferred_element_type=F32)


def _sigmoid(x):
    return 1.0 / (1.0 + jnp.exp(-x))


def _ada_kernel(c_ref, w_ref, b_ref, o_ref):
    c = c_ref[...]
    sc = (c * _sigmoid(c)).astype(BF16)
    o_ref[...] = _dot(sc, w_ref[...].astype(BF16)) + b_ref[...]


def _ada_call(c_pad, w_ada, b_ada):
    rows, d = c_pad.shape
    n = w_ada.shape[1]
    return pl.pallas_call(
        _ada_kernel,
        grid=(n // ADA_BLOCK_N,),
        in_specs=[
            pl.BlockSpec((rows, d), lambda j: (0, 0)),
            pl.BlockSpec((d, ADA_BLOCK_N), lambda j: (0, j)),
            pl.BlockSpec((1, ADA_BLOCK_N), lambda j: (0, j)),
        ],
        out_specs=pl.BlockSpec((rows, ADA_BLOCK_N), lambda j: (0, j)),
        out_shape=jax.ShapeDtypeStruct((rows, n), F32),
        compiler_params=pltpu.CompilerParams(
            dimension_semantics=("arbitrary",), vmem_limit_bytes=V7X_VMEM_LIMIT),
        name="ada_mod",
    )(c_pad, w_ada, b_ada)


def _pre_kernel(x_ref, mod_ref, g_ref, win_ref, cw_ref, woa_ref,
                ma_ref, sgb_ref, qt_ref, k_ref, vt_ref, ksq_ref, carry_ref,
                *, tiles_per_seq, d_conv, d_attn, d_model):
    i = pl.program_id(0)
    tm = x_ref.shape[0]

    x = x_ref[...]
    mod = mod_ref[0]
    xn = x * lax.rsqrt(jnp.mean(x * x, axis=-1, keepdims=True) + EPS)
    h = (xn * g_ref[...]) * (1.0 + mod[1:2, :]) + mod[0:1, :]
    hb = h.astype(BF16)

    c0, c1 = 0, 3 * d_conv
    pc = _dot(hb, win_ref[:, c0:c1])
    u = pc[:, 0:d_conv]
    gb = pc[:, d_conv:2 * d_conv]
    gc = pc[:, 2 * d_conv:3 * d_conv]
    cu = gc * u

    @pl.when(i % tiles_per_seq == 0)
    def _():
        carry_ref[...] = jnp.zeros_like(carry_ref)

    ext = jnp.concatenate([carry_ref[...], cu], axis=0)
    prev1 = pltpu.roll(ext, 1, axis=0)[8:, :]
    prev2 = pltpu.roll(ext, 2, axis=0)[8:, :]
    carry_ref[...] = cu[tm - 8:, :]
    cw = cw_ref[...]
    z = cw[0:1, :] * prev2 + cw[1:2, :] * prev1 + cw[2:3, :] * cu
    y_a = _dot((gb * z).astype(BF16), woa_ref[...])

    g0 = 3 * d_conv + 3 * d_attn
    pg = _dot(hb, win_ref[:, g0:g0 + 2 * d_model])
    ma_ref[...] = (_sigmoid(pg[:, 0:d_model]) * y_a).astype(BF16)
    sgb_ref[...] = _sigmoid(pg[:, d_model:2 * d_model]).astype(BF16)

    a0 = 3 * d_conv
    pa = _dot(hb, win_ref[:, a0:a0 + 3 * d_attn])
    q = pa[:, 0:d_attn] * (LOG2E / math.sqrt(DH))
    qt_ref[0] = q.T.astype(BF16)
    kb = pa[:, d_attn:2 * d_attn].astype(BF16)
    k_ref[...] = kb
    kf = kb.astype(F32)
    sq = kf * kf
    lane = lax.broadcasted_iota(jnp.int32, ksq_ref.shape[1:], 1)
    tile = jnp.zeros(ksq_ref.shape[1:], F32)
    for hd in range(N_HEADS):
        norm_sq = jnp.sum(sq[:, hd * DV:(hd + 1) * DV], axis=-1, keepdims=True)
        tile = jnp.where(lane == hd, jnp.max(norm_sq, axis=0, keepdims=True), tile)
    ksq_ref[0] = tile
    v = pa[:, 2 * d_attn:3 * d_attn]
    for s in range(tm // ATTN_KBLOCK):
        vt_ref[0, s] = v[s * ATTN_KBLOCK:(s + 1) * ATTN_KBLOCK, :].T.astype(BF16)


def _pre_call(x2, mod3, g_mix, w_in_b, conv_w, w_out_a_b, *, batch, seq):
    t, d = x2.shape
    d_conv = conv_w.shape[1]
    d_attn = N_HEADS * DV
    tm = PRE_BLOCK_M
    tps = seq // tm
    nkb = seq // ATTN_KBLOCK
    kern = functools.partial(_pre_kernel, tiles_per_seq=tps, d_conv=d_conv,
                             d_attn=d_attn, d_model=d)
    full = lambda a: pl.BlockSpec(a.shape, lambda i: (0,) * a.ndim)
    return pl.pallas_call(
        kern,
        grid=(t // tm,),
        in_specs=[
            pl.BlockSpec((tm, d), lambda i: (i, 0)),
            pl.BlockSpec((1,) + mod3.shape[1:], lambda i: (i // tps, 0, 0)),
            full(g_mix), full(w_in_b), full(conv_w), full(w_out_a_b),
        ],
        out_specs=[
            pl.BlockSpec((tm, d), lambda i: (i, 0)),
            pl.BlockSpec((tm, d), lambda i: (i, 0)),
            pl.BlockSpec((1, d_attn, tm), lambda i: (i // tps, 0, i % tps)),
            pl.BlockSpec((tm, d_attn), lambda i: (i, 0)),
            pl.BlockSpec((1, tm // ATTN_KBLOCK, d_attn, ATTN_KBLOCK),
                         lambda i: (i // tps, i % tps, 0, 0)),
            pl.BlockSpec((1, 8, 128), lambda i: (i, 0, 0)),
        ],
        out_shape=[
            jax.ShapeDtypeStruct((t, d), BF16),
            jax.ShapeDtypeStruct((t, d), BF16),
            jax.ShapeDtypeStruct((batch, d_attn, seq), BF16),
            jax.ShapeDtypeStruct((t, d_attn), BF16),
            jax.ShapeDtypeStruct((batch, nkb, d_attn, ATTN_KBLOCK), BF16),
            jax.ShapeDtypeStruct((t // tm, 8, 128), F32),
        ],
        scratch_shapes=[pltpu.VMEM((8, d_conv), F32)],
        compiler_params=pltpu.CompilerParams(
            dimension_semantics=("arbitrary",), vmem_limit_bytes=V7X_VMEM_LIMIT),
        name="mixer_in",
    )(x2, mod3, g_mix, w_in_b, conv_w, w_out_a_b)


def _attn_kernel(slope_ref, qt_ref, k_ref, vt_ref, ksq_ref, bias_ref, kpos_ref, lam_ref, g_ref,
                 o_ref, acc_ref, s_ref, p_ref):
    h = pl.program_id(1)
    qi = pl.program_id(2)
    tq, tk = ATTN_QBLOCK, ATTN_KBLOCK
    ratio = tq // tk
    slope = slope_ref[h]

    qt = qt_ref[0]
    row = lax.broadcasted_iota(jnp.int32, qt.shape, 0)
    zero = jnp.zeros_like(qt)
    qmaps = (jnp.where(row < DH, qt, zero), jnp.where(row >= DH, qt, zero))
    ones = jnp.ones((ATTN_ONES_ROWS, tk), BF16)

    last = ratio * qi + ratio - 1

    def key_block(kj):
        return k_ref[0, pl.ds(pl.multiple_of(kj * tk, tk), tk), :]

    def bias_tile(kj):
        return bias_ref[0, jnp.clip(kj - ratio * qi + 1, 0, ratio)]

    def bias_shift(kj):
        return -slope * (qi * tq - kj * tk).astype(F32)

    def accumulate(kj, slot, alphas=None):
        vb = jnp.concatenate([vt_ref[0, kj], ones], axis=0)
        for c in range(2):
            old = acc_ref[c] if alphas is None else alphas[c] * acc_ref[c]
            acc_ref[c] = old + _dot(vb, p_ref[slot, c])

    k_sq_all = jnp.max(ksq_ref[0], axis=0)
    lane = lax.broadcasted_iota(jnp.int32, k_sq_all.shape, 1)
    k_sq = jnp.max(jnp.where(lane == h, k_sq_all, 0.0), axis=-1, keepdims=True)[0:1]
    stab = []
    for c in range(2):
        qf = qmaps[c].astype(F32)
        q_sq = jnp.sum(qf * qf, axis=0, keepdims=True)
        stab.append(jnp.sqrt(q_sq * k_sq) * ATTN_BOUND_SLACK + ATTN_BOUND_MARGIN)

    def store_exponentials(e, slot, c, lo, hi, e_max):
        p_ref[slot, c, :, lo:hi] = jnp.exp2(e).astype(BF16)
        return jnp.maximum(e_max, jnp.max(e, axis=0, keepdims=True))

    def exponentials_diagonal(kj, tile, slot, lo, hi, e_max):
        kb = key_block(kj)
        bias = bias_ref[0, tile, :, lo:hi]
        shift = bias_shift(kj)
        return tuple(store_exponentials(
            _dot(kb, qmaps[c][:, lo:hi]) + bias + (shift - stab[c][:, lo:hi]), slot, c, lo, hi,
            e_max[c]) for c in range(2))

    lane_k = lax.broadcasted_iota(jnp.int32, (tk, 2 * DH), 1)
    ones_rows = (((row >= DH) & (row < DH + ATTN_POS_COLS)).astype(F32).astype(BF16),
                 (row < ATTN_POS_COLS).astype(F32).astype(BF16))
    q_aug = tuple(qmaps[c] + ones_rows[c] for c in range(2))
    pos = lax.broadcasted_iota(jnp.int32, (1, tq), 1).astype(F32)
    base = tuple(stab[c] + slope * pos for c in range(2))

    def exponentials_below(kj, slot, lo, hi, e_max):
        kb = key_block(kj)
        lhs = (jnp.where(lane_k < DH, kb, kpos_ref[0, 0]), jnp.where(lane_k >= DH, kb, kpos_ref[0, 1]))
        shift = bias_shift(kj)
        return tuple(store_exponentials(
            _dot(lhs[c], q_aug[c][:, lo:hi]) - (base[c][:, lo:hi] - shift), slot, c, lo, hi,
            e_max[c]) for c in range(2))

    def accumulate_columns(kj, slot, lo, hi):
        vb = jnp.concatenate([vt_ref[0, kj], ones], axis=0)
        for c in range(2):
            acc_ref[c, :, lo:hi] = acc_ref[c, :, lo:hi] + _dot(vb, p_ref[slot, c, :, lo:hi])

    acc_ref[...] = jnp.zeros_like(acc_ref)
    e0 = jnp.full((1, tq), NEG_INF, F32)
    half = tq // 2

    def diagonal_region(e_max, pending_pv=()):
        d0 = 2 * qi
        left = tuple(e[:, 0:half] for e in e_max)
        right = tuple(e[:, half:tq] for e in e_max)
        for stage in pending_pv[0:1]:
            stage()
        left = exponentials_diagonal(d0, 1, 0, 0, half, left)
        right = exponentials_below(d0, 0, half, tq, right)
        for stage in pending_pv[1:2]:
            stage()
        right = exponentials_diagonal(d0 + 1, 2, 1, half, tq, right)
        accumulate(d0, 0)
        accumulate_columns(d0 + 1, 1, half, tq)
        return left, right

    def first_query_block():
        return diagonal_region((e0, e0))

    def later_query_block():
        def trip(t, e_max):
            e_max = exponentials_below(2 * t + 1, 1, 0, tq, e_max)
            accumulate(2 * t, 0)
            e_max = exponentials_below(2 * t + 2, 0, 0, tq, e_max)
            accumulate(2 * t + 1, 1)
            return e_max

        e_max = lax.fori_loop(0, qi - 1, trip, exponentials_below(0, 0, 0, tq, (e0, e0)))
        d0 = 2 * qi
        e_max = exponentials_below(d0 - 1, 1, 0, tq, e_max)
        return diagonal_region(e_max, (lambda: accumulate(d0 - 2, 0), lambda: accumulate(d0 - 1, 1)))

    left, right = lax.cond(qi == 0, first_query_block, later_query_block)
    smallest = functools.reduce(jnp.minimum, [jnp.min(e) for e in left + right])

    @pl.when(smallest < ATTN_MIN_EXPONENT)
    def _():
        def body(kj, m_run):
            kb = key_block(kj)
            bias = bias_tile(kj)
            shift = bias_shift(kj)
            m_new, alpha = [], []
            for c in range(2):
                s = _dot(kb, qmaps[c]) + bias
                s_ref[c] = s
                m_c = jnp.maximum(m_run[c], jnp.max(s, axis=0, keepdims=True) + shift)
                alpha.append(jnp.exp2(m_run[c] - m_c))
                p_ref[0, c] = jnp.exp2(s_ref[c] - (m_c - shift)).astype(BF16)
                m_new.append(m_c)
            accumulate(kj, 0, alpha)
            return tuple(m_new)

        acc_ref[...] = jnp.zeros_like(acc_ref)
        lax.fori_loop(0, last + 1, body, (e0, e0))

    lp = lam_ref[...]
    lam = (jnp.exp(jnp.sum(lp[0:1, :] * lp[1:2, :], axis=-1, keepdims=True))
           - jnp.exp(jnp.sum(lp[2:3, :] * lp[3:4, :], axis=-1, keepdims=True))
           + LAMBDA_INIT)
    a1 = acc_ref[0]
    a2 = acc_ref[1]
    ot = (a1[0:DV, :] * (1.0 / a1[DV:DV + 1, :])
          - (lam * a2[0:DV, :]) * (1.0 / a2[DV:DV + 1, :]))
    o = ot.T
    o = o * lax.rsqrt(jnp.mean(o * o, axis=-1, keepdims=True) + EPS)
    o_ref[0] = ((o * g_ref[...]) * (1.0 - LAMBDA_INIT)).astype(BF16)


def _attn_bias_tiles():
    tq, tk = ATTN_QBLOCK, ATTN_KBLOCK
    jj = np.arange(tk, dtype=np.int64)[:, None]
    ii = np.arange(tq, dtype=np.int64)[None, :]
    slopes = np.asarray(ALIBI_SLOPES, np.float64)[:, None, None]
    tiles = [-slopes * (ii - jj)[None].astype(np.float64)]
    for r in range(tq // tk):
        key = r * tk + jj
        allowed = (key // CHUNK) <= (ii // CHUNK)
        tiles.append(np.where(allowed[None], -slopes * (np.abs(ii - key)[None] + r * tk), NEG_INF))
    return (LOG2E * np.stack(tiles, axis=1)).astype(np.float32)


def _attn_key_position_columns():
    tk = ATTN_KBLOCK
    rest = LOG2E * np.asarray(ALIBI_SLOPES, np.float64)[:, None] * np.arange(tk, dtype=np.float64)[None, :]
    out = np.zeros((N_HEADS, 2, tk, 2 * DH), np.float32)
    for n in range(ATTN_POS_COLS):
        part = rest.astype(ml_dtypes.bfloat16).astype(np.float64)
        out[:, 0, :, DH + n] = part
        out[:, 1, :, n] = part
        rest = rest - part
    return out.astype(ml_dtypes.bfloat16)


def _attn_call(qt, k3, vt, k_sq, lam_params, subln_g, *, batch, seq):
    tq, tk = ATTN_QBLOCK, ATTN_KBLOCK
    nkb = seq // tk
    d_attn = N_HEADS * DV
    bias = jnp.asarray(_attn_bias_tiles())
    kpos = jnp.asarray(_attn_key_position_columns())
    slopes = jnp.asarray([LOG2E * a for a in ALIBI_SLOPES], F32)
    return pl.pallas_call(
        _attn_kernel,
        grid=(batch, N_HEADS, seq // tq),
        in_specs=[
            pl.BlockSpec(memory_space=pltpu.SMEM),
            pl.BlockSpec((1, DV, tq), lambda b, h, q: (b, h, q)),
            pl.BlockSpec((1, seq, DV), lambda b, h, q: (b, 0, h)),
            pl.BlockSpec((1, nkb, DV, tk), lambda b, h, q: (b, 0, h, 0)),
            pl.BlockSpec((1,) + k_sq.shape[1:], lambda b, h, q: (b, 0, 0, 0)),
            pl.BlockSpec((1,) + bias.shape[1:], lambda b, h, q: (h, 0, 0, 0)),
            pl.BlockSpec((1,) + kpos.shape[1:], lambda b, h, q: (h, 0, 0, 0)),
            pl.BlockSpec(lam_params.shape, lambda b, h, q: (0, 0)),
            pl.BlockSpec(subln_g.shape, lambda b, h, q: (0, 0)),
        ],
        out_specs=pl.BlockSpec((1, tq, DV), lambda b, h, q: (b, q, h)),
        out_shape=jax.ShapeDtypeStruct((batch, seq, d_attn), BF16),
        scratch_shapes=[pltpu.VMEM((2, DV + ATTN_ONES_ROWS, tq), F32),
                        pltpu.VMEM((2, tk, tq), F32),
                        pltpu.VMEM((2, 2, tk, tq), BF16)],
        compiler_params=pltpu.CompilerParams(
            dimension_semantics=("arbitrary", "arbitrary", "arbitrary"),
            vmem_limit_bytes=V7X_VMEM_LIMIT),
        name="diff_attn",
    )(slopes, qt, k3, vt, k_sq, bias, kpos, lam_params, subln_g)


def _post_kernel(x_ref, o_ref, ma_ref, sgb_ref, mod_ref, gf_ref, gfin_ref,
                 wob_ref, wo_ref, wg_ref, wu_ref, wd_ref, y_ref):
    mod = mod_ref[0]
    y_b = _dot(o_ref[...], wob_ref[...])
    merged = ma_ref[...].astype(F32) + sgb_ref[...].astype(F32) * y_b
    x1 = x_ref[...] + mod[2:3, :] * _dot(merged.astype(BF16), wo_ref[...])

    xn = x1 * lax.rsqrt(jnp.mean(x1 * x1, axis=-1, keepdims=True) + EPS)
    h2 = ((xn * gf_ref[...]) * (1.0 + mod[4:5, :]) + mod[3:4, :]).astype(BF16)
    gate = _dot(h2, wg_ref[...])
    up = _dot(h2, wu_ref[...])
    act = ((gate * _sigmoid(gate)) * up).astype(BF16)
    x2 = x1 + mod[5:6, :] * _dot(act, wd_ref[...])

    y_ref[...] = (x2 * lax.rsqrt(jnp.mean(x2 * x2, axis=-1, keepdims=True) + EPS)) * gfin_ref[...]


def _post_call(x2, o2, ma, sgb, mod3, g_ffn, g_final, wob, wo, wg, wu, wd, *, seq):
    t, d = x2.shape
    tm = POST_BLOCK_M
    tps = seq // tm
    full = lambda a: pl.BlockSpec(a.shape, lambda i: (0,) * a.ndim)
    row = lambda a: pl.BlockSpec((tm, a.shape[1]), lambda i: (i, 0))
    return pl.pallas_call(
        _post_kernel,
        grid=(t // tm,),
        in_specs=[
            row(x2), row(o2), row(ma), row(sgb),
            pl.BlockSpec((1,) + mod3.shape[1:], lambda i: (i // tps, 0, 0)),
            full(g_ffn), full(g_final), full(wob), full(wo), full(wg), full(wu), full(wd),
        ],
        out_specs=pl.BlockSpec((tm, d), lambda i: (i, 0)),
        out_shape=jax.ShapeDtypeStruct((t, d), F32),
        compiler_params=pltpu.CompilerParams(
            dimension_semantics=("arbitrary",), vmem_limit_bytes=V7X_VMEM_LIMIT),
        name="mixer_out_ffn",
    )(x2, o2, ma, sgb, mod3, g_ffn, g_final, wob, wo, wg, wu, wd)


def kernel(x, c, w_ada, b_ada, g_mix, w_in, conv_w, w_out_a, lambda_q1, lambda_k1, lambda_q2,
           lambda_k2, subln_g, w_out_b, w_out, g_ffn, w_gate, w_up, w_down, g_final):
    batch, seq, d = x.shape
    depth = w_ada.shape[0]
    assert depth == 1 and seq % PRE_BLOCK_M == 0 and seq % POST_BLOCK_M == 0
    assert PRE_BLOCK_M % ATTN_KBLOCK == 0 and ATTN_KBLOCK % CHUNK == 0
    assert ATTN_QBLOCK == 2 * ATTN_KBLOCK and seq % ATTN_QBLOCK == 0
    l = 0
    x2 = x.reshape(batch * seq, d)

    c_pad = jnp.pad(c, ((0, 8 - batch), (0, 0)))
    mod = _ada_call(c_pad, w_ada[l], b_ada[l].reshape(1, -1))
    mod3 = mod[:batch].reshape(batch, 6, d)

    ma, sgb, qt, k2, vt, k_sq = _pre_call(
        x2, mod3, g_mix[l].reshape(1, d), w_in[l].astype(BF16), conv_w[l],
        w_out_a[l].astype(BF16), batch=batch, seq=seq)

    lam_params = jnp.stack([lambda_q1[l], lambda_k1[l], lambda_q2[l], lambda_k2[l]], axis=0)
    o = _attn_call(qt, k2.reshape(batch, seq, -1), vt, k_sq.reshape(batch, -1, 8, 128),
                   lam_params, subln_g[l].reshape(1, DV), batch=batch, seq=seq)

    y = _post_call(
        x2, o.reshape(batch * seq, -1), ma, sgb, mod3, g_ffn[l].reshape(1, d),
        g_final.reshape(1, d), w_out_b[l].astype(BF16), w_out[l].astype(BF16),
        w_gate[l].astype(BF16), w_up[l].astype(BF16), w_down[l].astype(BF16), seq=seq)
    return y.reshape(batch, seq, d)
```

```python
import functools
import math

import jax
import jax.numpy as jnp
import ml_dtypes
import numpy as np
from jax import lax
from jax.experimental import pallas as pl
from jax.experimental.pallas import tpu as pltpu

CHUNK = 64
CONV_WIDTH = 3
N_HEADS = 4
DH = 64
DV = 2 * DH
EPS = 1e-6
NEG_INF = -1e30
LAMBDA_INIT = 0.8 - 0.6 * math.exp(-0.3 * 1)
LOG2E = math.log2(math.e)
ALIBI_SLOPES = tuple(2.0 ** (-8.0 * (i + 1) / N_HEADS) for i in range(N_HEADS))

ADA_BLOCK_N = 1536
PRE_BLOCK_M = 512
PRE_ROW_GROUPS = 2
POST_BLOCK_M = 512
POST_ROW_GROUPS = 2
ATTN_KBLOCK = 512
ATTN_QBLOCK = 1024
ATTN_ONES_ROWS = 16
ATTN_POS_COLS = 3
ATTN_BOUND_SLACK = 1.01
ATTN_BOUND_MARGIN = 0.01
ATTN_MIN_EXPONENT = -100.0
V7X_VMEM_LIMIT = 56 * 1024 * 1024

BF16 = jnp.bfloat16
F32 = jnp.float32


def _dot(a, b):
    return jnp.dot(a, b, preferred_element_type=F32)


def _sigmoid(x):
    return 1.0 / (1.0 + jnp.exp(-x))


def _ada_kernel(c_ref, w_ref, b_ref, o_ref):
    c = c_ref[...]
    sc = (c * _sigmoid(c)).astype(BF16)
    o_ref[...] = _dot(sc, w_ref[...].astype(BF16)) + b_ref[...]


def _ada_call(c_pad, w_ada, b_ada):
    rows, d = c_pad.shape
    n = w_ada.shape[1]
    return pl.pallas_call(
        _ada_kernel,
        grid=(n // ADA_BLOCK_N,),
        in_specs=[
            pl.BlockSpec((rows, d), lambda j: (0, 0)),
            pl.BlockSpec((d, ADA_BLOCK_N), lambda j: (0, j)),
            pl.BlockSpec((1, ADA_BLOCK_N), lambda j: (0, j)),
        ],
        out_specs=pl.BlockSpec((rows, ADA_BLOCK_N), lambda j: (0, j)),
        out_shape=jax.ShapeDtypeStruct((rows, n), F32),
        compiler_params=pltpu.CompilerParams(
            dimension_semantics=("arbitrary",), vmem_limit_bytes=V7X_VMEM_LIMIT),
        name="ada_mod",
    )(c_pad, w_ada, b_ada)


def _pre_kernel(x_ref, mod_ref, g_ref, win_ref, cw_ref, woa_ref,
                ma_ref, sgb_ref, qt_ref, k_ref, vt_ref, ksq_ref, carry_ref,
                *, tiles_per_seq, d_conv, d_attn, d_model):
    i = pl.program_id(0)
    tm = x_ref.shape[0]
    mod = mod_ref[0]

    @pl.when(i % tiles_per_seq == 0)
    def _():
        carry_ref[...] = jnp.zeros_like(carry_ref)

    rg = tm // PRE_ROW_GROUPS
    groups = [slice(g * rg, (g + 1) * rg) for g in range(PRE_ROW_GROUPS)]

    a0 = 3 * d_conv
    g0 = a0 + 3 * d_attn

    def normed(rows):
        x = x_ref[rows, :]
        xn = x * lax.rsqrt(jnp.mean(x * x, axis=-1, keepdims=True) + EPS)
        return ((xn * g_ref[...]) * (1.0 + mod[1:2, :]) + mod[0:1, :]).astype(BF16)

    hb = [normed(rows) for rows in groups]

    pc = [_dot(h, win_ref[:, 0:a0]) for h in hb]
    cw = cw_ref[...]
    before = carry_ref[...]
    y_a = []
    for p in pc:
        u = p[:, 0:d_conv]
        gb = p[:, d_conv:2 * d_conv]
        gc = p[:, 2 * d_conv:3 * d_conv]
        cu = gc * u
        ext = jnp.concatenate([before, cu], axis=0)
        prev1 = pltpu.roll(ext, 1, axis=0)[8:, :]
        prev2 = pltpu.roll(ext, 2, axis=0)[8:, :]
        before = cu[rg - 8:, :]
        z = cw[0:1, :] * prev2 + cw[1:2, :] * prev1 + cw[2:3, :] * cu
        y_a.append(_dot((gb * z).astype(BF16), woa_ref[...]))
    carry_ref[...] = before

    for rows, h, ya in zip(groups, hb, y_a):
        pg = _dot(h, win_ref[:, g0:g0 + 2 * d_model])
        ma_ref[rows, :] = (_sigmoid(pg[:, 0:d_model]) * ya).astype(BF16)
        sgb_ref[rows, :] = _sigmoid(pg[:, d_model:2 * d_model]).astype(BF16)

    lane = lax.broadcasted_iota(jnp.int32, ksq_ref.shape[1:], 1)
    tile = jnp.zeros(ksq_ref.shape[1:], F32)
    for g, (rows, h) in enumerate(zip(groups, hb)):
        pa = _dot(h, win_ref[:, a0:g0])
        q = pa[:, 0:d_attn] * (LOG2E / math.sqrt(DH))
        qt_ref[0, :, rows] = q.T.astype(BF16)
        kb = pa[:, d_attn:2 * d_attn].astype(BF16)
        k_ref[rows, :] = kb
        kf = kb.astype(F32)
        sq = kf * kf
        for hd in range(N_HEADS):
            norm_sq = jnp.sum(sq[:, hd * DV:(hd + 1) * DV], axis=-1, keepdims=True)
            tile = jnp.where(lane == hd,
                             jnp.maximum(tile, jnp.max(norm_sq, axis=0, keepdims=True)), tile)
        v = pa[:, 2 * d_attn:3 * d_attn]
        blk, off = divmod(g * rg, ATTN_KBLOCK)
        vt_ref[0, blk, :, off:off + rg] = v.T.astype(BF16)
    ksq_ref[0] = tile


def _pre_call(x2, mod3, g_mix, w_in_b, conv_w, w_out_a_b, *, batch, seq):
    t, d = x2.shape
    d_conv = conv_w.shape[1]
    d_attn = N_HEADS * DV
    tm = PRE_BLOCK_M
    tps = seq // tm
    nkb = seq // ATTN_KBLOCK
    kern = functools.partial(_pre_kernel, tiles_per_seq=tps, d_conv=d_conv,
                             d_attn=d_attn, d_model=d)
    full = lambda a: pl.BlockSpec(a.shape, lambda i: (0,) * a.ndim)
    return pl.pallas_call(
        kern,
        grid=(t // tm,),
        in_specs=[
            pl.BlockSpec((tm, d), lambda i: (i, 0)),
            pl.BlockSpec((1,) + mod3.shape[1:], lambda i: (i // tps, 0, 0)),
            full(g_mix), full(w_in_b), full(conv_w), full(w_out_a_b),
        ],
        out_specs=[
            pl.BlockSpec((tm, d), lambda i: (i, 0)),
            pl.BlockSpec((tm, d), lambda i: (i, 0)),
            pl.BlockSpec((1, d_attn, tm), lambda i: (i // tps, 0, i % tps)),
            pl.BlockSpec((tm, d_attn), lambda i: (i, 0)),
            pl.BlockSpec((1, tm // ATTN_KBLOCK, d_attn, ATTN_KBLOCK),
                         lambda i: (i // tps, i % tps, 0, 0)),
            pl.BlockSpec((1, 8, 128), lambda i: (i, 0, 0)),
        ],
        out_shape=[
            jax.ShapeDtypeStruct((t, d), BF16),
            jax.ShapeDtypeStruct((t, d), BF16),
            jax.ShapeDtypeStruct((batch, d_attn, seq), BF16),
            jax.ShapeDtypeStruct((t, d_attn), BF16),
            jax.ShapeDtypeStruct((batch, nkb, d_attn, ATTN_KBLOCK), BF16),
            jax.ShapeDtypeStruct((t // tm, 8, 128), F32),
        ],
        scratch_shapes=[pltpu.VMEM((8, d_conv), F32)],
        compiler_params=pltpu.CompilerParams(
            dimension_semantics=("arbitrary",), vmem_limit_bytes=V7X_VMEM_LIMIT),
        name="mixer_in",
    )(x2, mod3, g_mix, w_in_b, conv_w, w_out_a_b)


def _attn_kernel(slope_ref, qt_ref, k_ref, vt_ref, ksq_ref, bias_ref, kpos_ref, lam_ref, g_ref,
                 o_ref, acc_ref, s_ref, p_ref):
    h = pl.program_id(1)
    qi = pl.program_id(2)
    tq, tk = ATTN_QBLOCK, ATTN_KBLOCK
    ratio = tq // tk
    slope = slope_ref[h]

    qt = qt_ref[0]
    row = lax.broadcasted_iota(jnp.int32, qt.shape, 0)
    zero = jnp.zeros_like(qt)
    qmaps = (jnp.where(row < DH, qt, zero), jnp.where(row >= DH, qt, zero))
    ones = jnp.ones((ATTN_ONES_ROWS, tk), BF16)

    last = ratio * qi + ratio - 1

    def key_block(kj):
        return k_ref[0, pl.ds(pl.multiple_of(kj * tk, tk), tk), :]

    def bias_tile(kj):
        return bias_ref[0, jnp.clip(kj - ratio * qi + 1, 0, ratio)]

    def bias_shift(kj):
        return -slope * (qi * tq - kj * tk).astype(F32)

    def accumulate(kj, slot, alphas=None):
        vb = jnp.concatenate([vt_ref[0, kj], ones], axis=0)
        for c in range(2):
            old = acc_ref[c] if alphas is None else alphas[c] * acc_ref[c]
            acc_ref[c] = old + _dot(vb, p_ref[slot, c])

    k_sq_all = jnp.max(ksq_ref[0], axis=0)
    lane = lax.broadcasted_iota(jnp.int32, k_sq_all.shape, 1)
    k_sq = jnp.max(jnp.where(lane == h, k_sq_all, 0.0), axis=-1, keepdims=True)[0:1]
    stab = []
    for c in range(2):
        qf = qmaps[c].astype(F32)
        q_sq = jnp.sum(qf * qf, axis=0, keepdims=True)
        stab.append(jnp.sqrt(q_sq * k_sq) * ATTN_BOUND_SLACK + ATTN_BOUND_MARGIN)

    def store_exponentials(e, slot, c, lo, hi, e_max):
        p_ref[slot, c, :, lo:hi] = jnp.exp2(e).astype(BF16)
        return jnp.maximum(e_max, jnp.max(e, axis=0, keepdims=True))

    def exponentials_diagonal(kj, tile, slot, lo, hi, e_max):
        kb = key_block(kj)
        bias = bias_ref[0, tile, :, lo:hi]
        shift = bias_shift(kj)
        return tuple(store_exponentials(
            _dot(kb, qmaps[c][:, lo:hi]) + bias + (shift - stab[c][:, lo:hi]), slot, c, lo, hi,
            e_max[c]) for c in range(2))

    lane_k = lax.broadcasted_iota(jnp.int32, (tk, 2 * DH), 1)
    ones_rows = (((row >= DH) & (row < DH + ATTN_POS_COLS)).astype(F32).astype(BF16),
                 (row < ATTN_POS_COLS).astype(F32).astype(BF16))
    q_aug = tuple(qmaps[c] + ones_rows[c] for c in range(2))
    pos = lax.broadcasted_iota(jnp.int32, (1, tq), 1).astype(F32)
    base = tuple(stab[c] + slope * pos for c in range(2))

    def exponentials_below(kj, slot, lo, hi, e_max):
        kb = key_block(kj)
        lhs = (jnp.where(lane_k < DH, kb, kpos_ref[0, 0]), jnp.where(lane_k >= DH, kb, kpos_ref[0, 1]))
        shift = bias_shift(kj)
        return tuple(store_exponentials(
            _dot(lhs[c], q_aug[c][:, lo:hi]) - (base[c][:, lo:hi] - shift), slot, c, lo, hi,
            e_max[c]) for c in range(2))

    def accumulate_columns(kj, slot, lo, hi):
        vb = jnp.concatenate([vt_ref[0, kj], ones], axis=0)
        for c in range(2):
            acc_ref[c, :, lo:hi] = acc_ref[c, :, lo:hi] + _dot(vb, p_ref[slot, c, :, lo:hi])

    acc_ref[...] = jnp.zeros_like(acc_ref)
    e0 = jnp.full((1, tq), NEG_INF, F32)
    half = tq // 2

    def diagonal_region(e_max, pending_pv=()):
        d0 = 2 * qi
        left = tuple(e[:, 0:half] for e in e_max)
        right = tuple(e[:, half:tq] for e in e_max)
        for stage in pending_pv[0:1]:
            stage()
        left = exponentials_diagonal(d0, 1, 0, 0, half, left)
        right = exponentials_below(d0, 0, half, tq, right)
        for stage in pending_pv[1:2]:
            stage()
        right = exponentials_diagonal(d0 + 1, 2, 1, half, tq, right)
        accumulate(d0, 0)
        accumulate_columns(d0 + 1, 1, half, tq)
        return left, right

    def first_query_block():
        return diagonal_region((e0, e0))

    def later_query_block():
        def trip(t, e_max):
            e_max = exponentials_below(2 * t + 1, 1, 0, tq, e_max)
            accumulate(2 * t, 0)
            e_max = exponentials_below(2 * t + 2, 0, 0, tq, e_max)
            accumulate(2 * t + 1, 1)
            return e_max

        e_max = lax.fori_loop(0, qi - 1, trip, exponentials_below(0, 0, 0, tq, (e0, e0)))
        d0 = 2 * qi
        e_max = exponentials_below(d0 - 1, 1, 0, tq, e_max)
        return diagonal_region(e_max, (lambda: accumulate(d0 - 2, 0), lambda: accumulate(d0 - 1, 1)))

    left, right = lax.cond(qi == 0, first_query_block, later_query_block)
    smallest = functools.reduce(jnp.minimum, [jnp.min(e) for e in left + right])

    @pl.when(smallest < ATTN_MIN_EXPONENT)
    def _():
        def body(kj, m_run):
            kb = key_block(kj)
            bias = bias_tile(kj)
            shift = bias_shift(kj)
            m_new, alpha = [], []
            for c in range(2):
                s = _dot(kb, qmaps[c]) + bias
                s_ref[c] = s
                m_c = jnp.maximum(m_run[c], jnp.max(s, axis=0, keepdims=True) + shift)
                alpha.append(jnp.exp2(m_run[c] - m_c))
                p_ref[0, c] = jnp.exp2(s_ref[c] - (m_c - shift)).astype(BF16)
                m_new.append(m_c)
            accumulate(kj, 0, alpha)
            return tuple(m_new)

        acc_ref[...] = jnp.zeros_like(acc_ref)
        lax.fori_loop(0, last + 1, body, (e0, e0))

    lp = lam_ref[...]
    lam = (jnp.exp(jnp.sum(lp[0:1, :] * lp[1:2, :], axis=-1, keepdims=True))
           - jnp.exp(jnp.sum(lp[2:3, :] * lp[3:4, :], axis=-1, keepdims=True))
           + LAMBDA_INIT)
    a1 = acc_ref[0]
    a2 = acc_ref[1]
    ot = (a1[0:DV, :] * (1.0 / a1[DV:DV + 1, :])
          - (lam * a2[0:DV, :]) * (1.0 / a2[DV:DV + 1, :]))
    o = ot.T
    o = o * lax.rsqrt(jnp.mean(o * o, axis=-1, keepdims=True) + EPS)
    o_ref[0] = ((o * g_ref[...]) * (1.0 - LAMBDA_INIT)).astype(BF16)


def _attn_bias_tiles():
    tq, tk = ATTN_QBLOCK, ATTN_KBLOCK
    jj = np.arange(tk, dtype=np.int64)[:, None]
    ii = np.arange(tq, dtype=np.int64)[None, :]
    slopes = np.asarray(ALIBI_SLOPES, np.float64)[:, None, None]
    tiles = [-slopes * (ii - jj)[None].astype(np.float64)]
    for r in range(tq // tk):
        key = r * tk + jj
        allowed = (key // CHUNK) <= (ii // CHUNK)
        tiles.append(np.where(allowed[None], -slopes * (np.abs(ii - key)[None] + r * tk), NEG_INF))
    return (LOG2E * np.stack(tiles, axis=1)).astype(np.float32)


def _attn_key_position_columns():
    tk = ATTN_KBLOCK
    rest = LOG2E * np.asarray(ALIBI_SLOPES, np.float64)[:, None] * np.arange(tk, dtype=np.float64)[None, :]
    out = np.zeros((N_HEADS, 2, tk, 2 * DH), np.float32)
    for n in range(ATTN_POS_COLS):
        part = rest.astype(ml_dtypes.bfloat16).astype(np.float64)
        out[:, 0, :, DH + n] = part
        out[:, 1, :, n] = part
        rest = rest - part
    return out.astype(ml_dtypes.bfloat16)


def _attn_call(qt, k3, vt, k_sq, lam_params, subln_g, *, batch, seq):
    tq, tk = ATTN_QBLOCK, ATTN_KBLOCK
    nkb = seq // tk
    d_attn = N_HEADS * DV
    bias = jnp.asarray(_attn_bias_tiles())
    kpos = jnp.asarray(_attn_key_position_columns())
    slopes = jnp.asarray([LOG2E * a for a in ALIBI_SLOPES], F32)
    return pl.pallas_call(
        _attn_kernel,
        grid=(batch, N_HEADS, seq // tq),
        in_specs=[
            pl.BlockSpec(memory_space=pltpu.SMEM),
            pl.BlockSpec((1, DV, tq), lambda b, h, q: (b, h, q)),
            pl.BlockSpec((1, seq, DV), lambda b, h, q: (b, 0, h)),
            pl.BlockSpec((1, nkb, DV, tk), lambda b, h, q: (b, 0, h, 0)),
            pl.BlockSpec((1,) + k_sq.shape[1:], lambda b, h, q: (b, 0, 0, 0)),
            pl.BlockSpec((1,) + bias.shape[1:], lambda b, h, q: (h, 0, 0, 0)),
            pl.BlockSpec((1,) + kpos.shape[1:], lambda b, h, q: (h, 0, 0, 0)),
            pl.BlockSpec(lam_params.shape, lambda b, h, q: (0, 0)),
            pl.BlockSpec(subln_g.shape, lambda b, h, q: (0, 0)),
        ],
        out_specs=pl.BlockSpec((1, tq, DV), lambda b, h, q: (b, q, h)),
        out_shape=jax.ShapeDtypeStruct((batch, seq, d_attn), BF16),
        scratch_shapes=[pltpu.VMEM((2, DV + ATTN_ONES_ROWS, tq), F32),
                        pltpu.VMEM((2, tk, tq), F32),
                        pltpu.VMEM((2, 2, tk, tq), BF16)],
        compiler_params=pltpu.CompilerParams(
            dimension_semantics=("arbitrary", "arbitrary", "arbitrary"),
            vmem_limit_bytes=V7X_VMEM_LIMIT),
        name="diff_attn",
    )(slopes, qt, k3, vt, k_sq, bias, kpos, lam_params, subln_g)


def _post_kernel(x_ref, o_ref, ma_ref, sgb_ref, mod_ref, gf_ref, gfin_ref,
                 wob_ref, wo_ref, wg_ref, wu_ref, wd_ref, y_ref):
    mod = mod_ref[0]
    tm = x_ref.shape[0]
    groups = [slice(g * tm // POST_ROW_GROUPS, (g + 1) * tm // POST_ROW_GROUPS)
              for g in range(POST_ROW_GROUPS)]

    def mixer_residual(rows):
        y_b = _dot(o_ref[rows, :], wob_ref[...])
        merged = ma_ref[rows, :].astype(F32) + sgb_ref[rows, :].astype(F32) * y_b
        return x_ref[rows, :] + mod[2:3, :] * _dot(merged.astype(BF16), wo_ref[...])

    def ffn_input(x1):
        xn = x1 * lax.rsqrt(jnp.mean(x1 * x1, axis=-1, keepdims=True) + EPS)
        return ((xn * gf_ref[...]) * (1.0 + mod[4:5, :]) + mod[3:4, :]).astype(BF16)

    def ffn_hidden(h2):
        gate = _dot(h2, wg_ref[...])
        up = _dot(h2, wu_ref[...])
        return ((gate * _sigmoid(gate)) * up).astype(BF16)

    def finish(rows, x1, act):
        x2 = x1 + mod[5:6, :] * _dot(act, wd_ref[...])
        y_ref[rows, :] = (x2 * lax.rsqrt(jnp.mean(x2 * x2, axis=-1, keepdims=True) + EPS)) * gfin_ref[...]

    x1 = [mixer_residual(rows) for rows in groups]
    h2 = [ffn_input(v) for v in x1]
    act = [ffn_hidden(v) for v in h2]
    for rows, a, b in zip(groups, x1, act):
        finish(rows, a, b)


def _post_call(x2, o2, ma, sgb, mod3, g_ffn, g_final, wob, wo, wg, wu, wd, *, seq):
    t, d = x2.shape
    tm = POST_BLOCK_M
    tps = seq // tm
    full = lambda a: pl.BlockSpec(a.shape, lambda i: (0,) * a.ndim)
    row = lambda a: pl.BlockSpec((tm, a.shape[1]), lambda i: (i, 0))
    return pl.pallas_call(
        _post_kernel,
        grid=(t // tm,),
        in_specs=[
            row(x2), row(o2), row(ma), row(sgb),
            pl.BlockSpec((1,) + mod3.shape[1:], lambda i: (i // tps, 0, 0)),
            full(g_ffn), full(g_final), full(wob), full(wo), full(wg), full(wu), full(wd),
        ],
        out_specs=pl.BlockSpec((tm, d), lambda i: (i, 0)),
        out_shape=jax.ShapeDtypeStruct((t, d), F32),
        compiler_params=pltpu.CompilerParams(
            dimension_semantics=("arbitrary",), vmem_limit_bytes=V7X_VMEM_LIMIT),
        name="mixer_out_ffn",
    )(x2, o2, ma, sgb, mod3, g_ffn, g_final, wob, wo, wg, wu, wd)


def kernel(x, c, w_ada, b_ada, g_mix, w_in, conv_w, w_out_a, lambda_q1, lambda_k1, lambda_q2,
           lambda_k2, subln_g, w_out_b, w_out, g_ffn, w_gate, w_up, w_down, g_final):
    batch, seq, d = x.shape
    depth = w_ada.shape[0]
    assert depth == 1 and seq % PRE_BLOCK_M == 0 and seq % POST_BLOCK_M == 0
    assert PRE_BLOCK_M % ATTN_KBLOCK == 0 and ATTN_KBLOCK % CHUNK == 0
    assert ATTN_KBLOCK % (PRE_BLOCK_M // PRE_ROW_GROUPS) == 0 and PRE_BLOCK_M % (8 * PRE_ROW_GROUPS) == 0
    assert POST_BLOCK_M % (8 * POST_ROW_GROUPS) == 0
    assert ATTN_QBLOCK == 2 * ATTN_KBLOCK and seq % ATTN_QBLOCK == 0
    l = 0
    x2 = x.reshape(batch * seq, d)

    c_pad = jnp.pad(c, ((0, 8 - batch), (0, 0)))
    mod = _ada_call(c_pad, w_ada[l], b_ada[l].reshape(1, -1))
    mod3 = mod[:batch].reshape(batch, 6, d)

    ma, sgb, qt, k2, vt, k_sq = _pre_call(
        x2, mod3, g_mix[l].reshape(1, d), w_in[l].astype(BF16), conv_w[l],
        w_out_a[l].astype(BF16), batch=batch, seq=seq)

    lam_params = jnp.stack([lambda_q1[l], lambda_k1[l], lambda_q2[l], lambda_k2[l]], axis=0)
    o = _attn_call(qt, k2.reshape(batch, seq, -1), vt, k_sq.reshape(batch, -1, 8, 128),
                   lam_params, subln_g[l].reshape(1, DV), batch=batch, seq=seq)

    y = _post_call(
        x2, o.reshape(batch * seq, -1), ma, sgb, mod3, g_ffn[l].reshape(1, d),
        g_final.reshape(1, d), w_out_b[l].astype(BF16), w_out[l].astype(BF16),
        w_gate[l].astype(BF16), w_up[l].astype(BF16), w_down[l].astype(BF16), seq=seq)
    return y.reshape(batch, seq, d)
```

```python
import functools
import math

import jax
import jax.numpy as jnp
import ml_dtypes
import numpy as np
from jax import lax
from jax.experimental import pallas as pl
from jax.experimental.pallas import tpu as pltpu

CHUNK = 64
CONV_WIDTH = 3
N_HEADS = 4
DH = 64
DV = 2 * DH
EPS = 1e-6
NEG_INF = -1e30
LAMBDA_INIT = 0.8 - 0.6 * math.exp(-0.3 * 1)
LOG2E = math.log2(math.e)
ALIBI_SLOPES = tuple(2.0 ** (-8.0 * (i + 1) / N_HEADS) for i in range(N_HEADS))

ADA_BLOCK_N = 1536
PRE_BLOCK_M = 1024
PRE_ROW_GROUPS = 4
POST_BLOCK_M = 512
POST_ROW_GROUPS = 2
ATTN_KBLOCK = 512
ATTN_QBLOCK = 1024
ATTN_ONES_ROWS = 16
ATTN_POS_COLS = 3
ATTN_BOUND_SLACK = 1.01
ATTN_BOUND_MARGIN = 0.01
ATTN_MIN_EXPONENT = -100.0
V7X_VMEM_LIMIT = 56 * 1024 * 1024

BF16 = jnp.bfloat16
F32 = jnp.float32


def _dot(a, b):
    return jnp.dot(a, b, preferred_element_type=F32)


def _sigmoid(x):
    return 1.0 / (1.0 + jnp.exp(-x))


def _ada_kernel(c_ref, w_ref, b_ref, o_ref):
    c = c_ref[...]
    sc = (c * _sigmoid(c)).astype(BF16)
    o_ref[...] = _dot(sc, w_ref[...].astype(BF16)) + b_ref[...]


def _ada_call(c_pad, w_ada, b_ada):
    rows, d = c_pad.shape
    n = w_ada.shape[1]
    return pl.pallas_call(
        _ada_kernel,
        grid=(n // ADA_BLOCK_N,),
        in_specs=[
            pl.BlockSpec((rows, d), lambda j: (0, 0)),
            pl.BlockSpec((d, ADA_BLOCK_N), lambda j: (0, j)),
            pl.BlockSpec((1, ADA_BLOCK_N), lambda j: (0, j)),
        ],
        out_specs=pl.BlockSpec((rows, ADA_BLOCK_N), lambda j: (0, j)),
        out_shape=jax.ShapeDtypeStruct((rows, n), F32),
        compiler_params=pltpu.CompilerParams(
            dimension_semantics=("arbitrary",), vmem_limit_bytes=V7X_VMEM_LIMIT),
        name="ada_mod",
    )(c_pad, w_ada, b_ada)


def _pre_kernel(x_ref, mod_ref, g_ref, win_ref, cw_ref, woa_ref,
                ma_ref, sgb_ref, qt_ref, k_ref, vt_ref, ksq_ref, carry_ref,
                *, tiles_per_seq, d_conv, d_attn, d_model):
    i = pl.program_id(0)
    tm = x_ref.shape[0]
    mod = mod_ref[0]

    @pl.when(i % tiles_per_seq == 0)
    def _():
        carry_ref[...] = jnp.zeros_like(carry_ref)

    rg = tm // PRE_ROW_GROUPS
    groups = [slice(g * rg, (g + 1) * rg) for g in range(PRE_ROW_GROUPS)]

    a0 = 3 * d_conv
    g0 = a0 + 3 * d_attn

    def normed(rows):
        x = x_ref[rows, :]
        xn = x * lax.rsqrt(jnp.mean(x * x, axis=-1, keepdims=True) + EPS)
        return ((xn * g_ref[...]) * (1.0 + mod[1:2, :]) + mod[0:1, :]).astype(BF16)

    hb = [normed(rows) for rows in groups]

    pc = [_dot(h, win_ref[:, 0:a0]) for h in hb]
    cw = cw_ref[...]
    before = carry_ref[...]
    y_a = []
    for p in pc:
        u = p[:, 0:d_conv]
        gb = p[:, d_conv:2 * d_conv]
        gc = p[:, 2 * d_conv:3 * d_conv]
        cu = gc * u
        ext = jnp.concatenate([before, cu], axis=0)
        prev1 = pltpu.roll(ext, 1, axis=0)[8:, :]
        prev2 = pltpu.roll(ext, 2, axis=0)[8:, :]
        before = cu[rg - 8:, :]
        z = cw[0:1, :] * prev2 + cw[1:2, :] * prev1 + cw[2:3, :] * cu
        y_a.append(_dot((gb * z).astype(BF16), woa_ref[...]))
    carry_ref[...] = before

    for rows, h, ya in zip(groups, hb, y_a):
        pg = _dot(h, win_ref[:, g0:g0 + 2 * d_model])
        ma_ref[rows, :] = (_sigmoid(pg[:, 0:d_model]) * ya).astype(BF16)
        sgb_ref[rows, :] = _sigmoid(pg[:, d_model:2 * d_model]).astype(BF16)

    lane = lax.broadcasted_iota(jnp.int32, ksq_ref.shape[1:], 1)
    tile = jnp.zeros(ksq_ref.shape[1:], F32)
    for g, (rows, h) in enumerate(zip(groups, hb)):
        pa = _dot(h, win_ref[:, a0:g0])
        q = pa[:, 0:d_attn] * (LOG2E / math.sqrt(DH))
        qt_ref[0, :, rows] = q.T.astype(BF16)
        kb = pa[:, d_attn:2 * d_attn].astype(BF16)
        k_ref[rows, :] = kb
        kf = kb.astype(F32)
        sq = kf * kf
        for hd in range(N_HEADS):
            norm_sq = jnp.sum(sq[:, hd * DV:(hd + 1) * DV], axis=-1, keepdims=True)
            tile = jnp.where(lane == hd,
                             jnp.maximum(tile, jnp.max(norm_sq, axis=0, keepdims=True)), tile)
        v = pa[:, 2 * d_attn:3 * d_attn]
        blk, off = divmod(g * rg, ATTN_KBLOCK)
        vt_ref[0, blk, :, off:off + rg] = v.T.astype(BF16)
    ksq_ref[0] = tile


def _pre_call(x2, mod3, g_mix, w_in_b, conv_w, w_out_a_b, *, batch, seq):
    t, d = x2.shape
    d_conv = conv_w.shape[1]
    d_attn = N_HEADS * DV
    tm = PRE_BLOCK_M
    tps = seq // tm
    nkb = seq // ATTN_KBLOCK
    kern = functools.partial(_pre_kernel, tiles_per_seq=tps, d_conv=d_conv,
                             d_attn=d_attn, d_model=d)
    full = lambda a: pl.BlockSpec(a.shape, lambda i: (0,) * a.ndim)
    return pl.pallas_call(
        kern,
        grid=(t // tm,),
        in_specs=[
            pl.BlockSpec((tm, d), lambda i: (i, 0)),
            pl.BlockSpec((1,) + mod3.shape[1:], lambda i: (i // tps, 0, 0)),
            full(g_mix), full(w_in_b), full(conv_w), full(w_out_a_b),
        ],
        out_specs=[
            pl.BlockSpec((tm, d), lambda i: (i, 0)),
            pl.BlockSpec((tm, d), lambda i: (i, 0)),
            pl.BlockSpec((1, d_attn, tm), lambda i: (i // tps, 0, i % tps)),
            pl.BlockSpec((tm, d_attn), lambda i: (i, 0)),
            pl.BlockSpec((1, tm // ATTN_KBLOCK, d_attn, ATTN_KBLOCK),
                         lambda i: (i // tps, i % tps, 0, 0)),
            pl.BlockSpec((1, 8, 128), lambda i: (i, 0, 0)),
        ],
        out_shape=[
            jax.ShapeDtypeStruct((t, d), BF16),
            jax.ShapeDtypeStruct((t, d), BF16),
            jax.ShapeDtypeStruct((batch, d_attn, seq), BF16),
            jax.ShapeDtypeStruct((t, d_attn), BF16),
            jax.ShapeDtypeStruct((batch, nkb, d_attn, ATTN_KBLOCK), BF16),
            jax.ShapeDtypeStruct((t // tm, 8, 128), F32),
        ],
        scratch_shapes=[pltpu.VMEM((8, d_conv), F32)],
        compiler_params=pltpu.CompilerParams(
            dimension_semantics=("arbitrary",), vmem_limit_bytes=V7X_VMEM_LIMIT),
        name="mixer_in",
    )(x2, mod3, g_mix, w_in_b, conv_w, w_out_a_b)


def _attn_kernel(slope_ref, qt_ref, k_ref, vt_ref, ksq_ref, bias_ref, kpos_ref, lam_ref, g_ref,
                 o_ref, acc_ref, s_ref, p_ref):
    h = pl.program_id(1)
    qi = pl.program_id(2)
    tq, tk = ATTN_QBLOCK, ATTN_KBLOCK
    ratio = tq // tk
    slope = slope_ref[h]

    qt = qt_ref[0]
    row = lax.broadcasted_iota(jnp.int32, qt.shape, 0)
    zero = jnp.zeros_like(qt)
    qmaps = (jnp.where(row < DH, qt, zero), jnp.where(row >= DH, qt, zero))
    ones = jnp.ones((ATTN_ONES_ROWS, tk), BF16)

    last = ratio * qi + ratio - 1

    def key_block(kj):
        return k_ref[0, pl.ds(pl.multiple_of(kj * tk, tk), tk), :]

    def bias_tile(kj):
        return bias_ref[0, jnp.clip(kj - ratio * qi + 1, 0, ratio)]

    def bias_shift(kj):
        return -slope * (qi * tq - kj * tk).astype(F32)

    def accumulate(kj, slot, alphas=None):
        vb = jnp.concatenate([vt_ref[0, kj], ones], axis=0)
        for c in range(2):
            old = acc_ref[c] if alphas is None else alphas[c] * acc_ref[c]
            acc_ref[c] = old + _dot(vb, p_ref[slot, c])

    k_sq_all = jnp.max(ksq_ref[0], axis=0)
    lane = lax.broadcasted_iota(jnp.int32, k_sq_all.shape, 1)
    k_sq = jnp.max(jnp.where(lane == h, k_sq_all, 0.0), axis=-1, keepdims=True)[0:1]
    stab = []
    for c in range(2):
        qf = qmaps[c].astype(F32)
        q_sq = jnp.sum(qf * qf, axis=0, keepdims=True)
        stab.append(jnp.sqrt(q_sq * k_sq) * ATTN_BOUND_SLACK + ATTN_BOUND_MARGIN)

    def store_exponentials(e, slot, c, lo, hi, e_max):
        p_ref[slot, c, :, lo:hi] = jnp.exp2(e).astype(BF16)
        return jnp.maximum(e_max, jnp.max(e, axis=0, keepdims=True))

    def exponentials_diagonal(kj, tile, slot, lo, hi, e_max):
        kb = key_block(kj)
        bias = bias_ref[0, tile, :, lo:hi]
        shift = bias_shift(kj)
        return tuple(store_exponentials(
            _dot(kb, qmaps[c][:, lo:hi]) + bias + (shift - stab[c][:, lo:hi]), slot, c, lo, hi,
            e_max[c]) for c in range(2))

    lane_k = lax.broadcasted_iota(jnp.int32, (tk, 2 * DH), 1)
    ones_rows = (((row >= DH) & (row < DH + ATTN_POS_COLS)).astype(F32).astype(BF16),
                 (row < ATTN_POS_COLS).astype(F32).astype(BF16))
    q_aug = tuple(qmaps[c] + ones_rows[c] for c in range(2))
    pos = lax.broadcasted_iota(jnp.int32, (1, tq), 1).astype(F32)
    base = tuple(stab[c] + slope * pos for c in range(2))

    def exponentials_below(kj, slot, lo, hi, e_max):
        kb = key_block(kj)
        lhs = (jnp.where(lane_k < DH, kb, kpos_ref[0, 0]), jnp.where(lane_k >= DH, kb, kpos_ref[0, 1]))
        shift = bias_shift(kj)
        return tuple(store_exponentials(
            _dot(lhs[c], q_aug[c][:, lo:hi]) - (base[c][:, lo:hi] - shift), slot, c, lo, hi,
            e_max[c]) for c in range(2))

    def accumulate_columns(kj, slot, lo, hi):
        vb = jnp.concatenate([vt_ref[0, kj], ones], axis=0)
        for c in range(2):
            acc_ref[c, :, lo:hi] = acc_ref[c, :, lo:hi] + _dot(vb, p_ref[slot, c, :, lo:hi])

    acc_ref[...] = jnp.zeros_like(acc_ref)
    e0 = jnp.full((1, tq), NEG_INF, F32)
    half = tq // 2

    def diagonal_region(e_max, pending_pv=()):
        d0 = 2 * qi
        left = tuple(e[:, 0:half] for e in e_max)
        right = tuple(e[:, half:tq] for e in e_max)
        for stage in pending_pv[0:1]:
            stage()
        left = exponentials_diagonal(d0, 1, 0, 0, half, left)
        right = exponentials_below(d0, 0, half, tq, right)
        for stage in pending_pv[1:2]:
            stage()
        right = exponentials_diagonal(d0 + 1, 2, 1, half, tq, right)
        accumulate(d0, 0)
        accumulate_columns(d0 + 1, 1, half, tq)
        return left, right

    def first_query_block():
        return diagonal_region((e0, e0))

    def later_query_block():
        def trip(t, e_max):
            e_max = exponentials_below(2 * t + 1, 1, 0, tq, e_max)
            accumulate(2 * t, 0)
            e_max = exponentials_below(2 * t + 2, 0, 0, tq, e_max)
            accumulate(2 * t + 1, 1)
            return e_max

        e_max = lax.fori_loop(0, qi - 1, trip, exponentials_below(0, 0, 0, tq, (e0, e0)))
        d0 = 2 * qi
        e_max = exponentials_below(d0 - 1, 1, 0, tq, e_max)
        return diagonal_region(e_max, (lambda: accumulate(d0 - 2, 0), lambda: accumulate(d0 - 1, 1)))

    left, right = lax.cond(qi == 0, first_query_block, later_query_block)
    smallest = functools.reduce(jnp.minimum, [jnp.min(e) for e in left + right])

    @pl.when(smallest < ATTN_MIN_EXPONENT)
    def _():
        def body(kj, m_run):
            kb = key_block(kj)
            bias = bias_tile(kj)
            shift = bias_shift(kj)
            m_new, alpha = [], []
            for c in range(2):
                s = _dot(kb, qmaps[c]) + bias
                s_ref[c] = s
                m_c = jnp.maximum(m_run[c], jnp.max(s, axis=0, keepdims=True) + shift)
                alpha.append(jnp.exp2(m_run[c] - m_c))
                p_ref[0, c] = jnp.exp2(s_ref[c] - (m_c - shift)).astype(BF16)
                m_new.append(m_c)
            accumulate(kj, 0, alpha)
            return tuple(m_new)

        acc_ref[...] = jnp.zeros_like(acc_ref)
        lax.fori_loop(0, last + 1, body, (e0, e0))

    lp = lam_ref[...]
    lam = (jnp.exp(jnp.sum(lp[0:1, :] * lp[1:2, :], axis=-1, keepdims=True))
           - jnp.exp(jnp.sum(lp[2:3, :] * lp[3:4, :], axis=-1, keepdims=True))
           + LAMBDA_INIT)
    a1 = acc_ref[0]
    a2 = acc_ref[1]
    ot = (a1[0:DV, :] * (1.0 / a1[DV:DV + 1, :])
          - (lam * a2[0:DV, :]) * (1.0 / a2[DV:DV + 1, :]))
    o = ot.T
    o = o * lax.rsqrt(jnp.mean(o * o, axis=-1, keepdims=True) + EPS)
    o_ref[0] = ((o * g_ref[...]) * (1.0 - LAMBDA_INIT)).astype(BF16)


def _attn_bias_tiles():
    tq, tk = ATTN_QBLOCK, ATTN_KBLOCK
    jj = np.arange(tk, dtype=np.int64)[:, None]
    ii = np.arange(tq, dtype=np.int64)[None, :]
    slopes = np.asarray(ALIBI_SLOPES, np.float64)[:, None, None]
    tiles = [-slopes * (ii - jj)[None].astype(np.float64)]
    for r in range(tq // tk):
        key = r * tk + jj
        allowed = (key // CHUNK) <= (ii // CHUNK)
        tiles.append(np.where(allowed[None], -slopes * (np.abs(ii - key)[None] + r * tk), NEG_INF))
    return (LOG2E * np.stack(tiles, axis=1)).astype(np.float32)


def _attn_key_position_columns():
    tk = ATTN_KBLOCK
    rest = LOG2E * np.asarray(ALIBI_SLOPES, np.float64)[:, None] * np.arange(tk, dtype=np.float64)[None, :]
    out = np.zeros((N_HEADS, 2, tk, 2 * DH), np.float32)
    for n in range(ATTN_POS_COLS):
        part = rest.astype(ml_dtypes.bfloat16).astype(np.float64)
        out[:, 0, :, DH + n] = part
        out[:, 1, :, n] = part
        rest = rest - part
    return out.astype(ml_dtypes.bfloat16)


def _attn_call(qt, k3, vt, k_sq, lam_params, subln_g, *, batch, seq):
    tq, tk = ATTN_QBLOCK, ATTN_KBLOCK
    nkb = seq // tk
    d_attn = N_HEADS * DV
    bias = jnp.asarray(_attn_bias_tiles())
    kpos = jnp.asarray(_attn_key_position_columns())
    slopes = jnp.asarray([LOG2E * a for a in ALIBI_SLOPES], F32)
    return pl.pallas_call(
        _attn_kernel,
        grid=(batch, N_HEADS, seq // tq),
        in_specs=[
            pl.BlockSpec(memory_space=pltpu.SMEM),
            pl.BlockSpec((1, DV, tq), lambda b, h, q: (b, h, q)),
            pl.BlockSpec((1, seq, DV), lambda b, h, q: (b, 0, h)),
            pl.BlockSpec((1, nkb, DV, tk), lambda b, h, q: (b, 0, h, 0)),
            pl.BlockSpec((1,) + k_sq.shape[1:], lambda b, h, q: (b, 0, 0, 0)),
            pl.BlockSpec((1,) + bias.shape[1:], lambda b, h, q: (h, 0, 0, 0)),
            pl.BlockSpec((1,) + kpos.shape[1:], lambda b, h, q: (h, 0, 0, 0)),
            pl.BlockSpec(lam_params.shape, lambda b, h, q: (0, 0)),
            pl.BlockSpec(subln_g.shape, lambda b, h, q: (0, 0)),
        ],
        out_specs=pl.BlockSpec((1, tq, DV), lambda b, h, q: (b, q, h)),
        out_shape=jax.ShapeDtypeStruct((batch, seq, d_attn), BF16),
        scratch_shapes=[pltpu.VMEM((2, DV + ATTN_ONES_ROWS, tq), F32),
                        pltpu.VMEM((2, tk, tq), F32),
                        pltpu.VMEM((2, 2, tk, tq), BF16)],
        compiler_params=pltpu.CompilerParams(
            dimension_semantics=("arbitrary", "arbitrary", "arbitrary"),
            vmem_limit_bytes=V7X_VMEM_LIMIT),
        name="diff_attn",
    )(slopes, qt, k3, vt, k_sq, bias, kpos, lam_params, subln_g)


def _post_kernel(x_ref, o_ref, ma_ref, sgb_ref, mod_ref, gf_ref, gfin_ref,
                 wob_ref, wo_ref, wg_ref, wu_ref, wd_ref, y_ref):
    mod = mod_ref[0]
    tm = x_ref.shape[0]
    groups = [slice(g * tm // POST_ROW_GROUPS, (g + 1) * tm // POST_ROW_GROUPS)
              for g in range(POST_ROW_GROUPS)]

    def mixer_residual(rows):
        y_b = _dot(o_ref[rows, :], wob_ref[...])
        merged = ma_ref[rows, :].astype(F32) + sgb_ref[rows, :].astype(F32) * y_b
        return x_ref[rows, :] + mod[2:3, :] * _dot(merged.astype(BF16), wo_ref[...])

    def ffn_input(x1):
        xn = x1 * lax.rsqrt(jnp.mean(x1 * x1, axis=-1, keepdims=True) + EPS)
        return ((xn * gf_ref[...]) * (1.0 + mod[4:5, :]) + mod[3:4, :]).astype(BF16)

    def ffn_hidden(h2):
        gate = _dot(h2, wg_ref[...])
        up = _dot(h2, wu_ref[...])
        return ((gate * _sigmoid(gate)) * up).astype(BF16)

    def finish(rows, x1, act):
        x2 = x1 + mod[5:6, :] * _dot(act, wd_ref[...])
        y_ref[rows, :] = (x2 * lax.rsqrt(jnp.mean(x2 * x2, axis=-1, keepdims=True) + EPS)) * gfin_ref[...]

    x1 = [mixer_residual(rows) for rows in groups]
    h2 = [ffn_input(v) for v in x1]
    act = [ffn_hidden(v) for v in h2]
    for rows, a, b in zip(groups, x1, act):
        finish(rows, a, b)


def _post_call(x2, o2, ma, sgb, mod3, g_ffn, g_final, wob, wo, wg, wu, wd, *, seq):
    t, d = x2.shape
    tm = POST_BLOCK_M
    tps = seq // tm
    full = lambda a: pl.BlockSpec(a.shape, lambda i: (0,) * a.ndim)
    row = lambda a: pl.BlockSpec((tm, a.shape[1]), lambda i: (i, 0))
    return pl.pallas_call(
        _post_kernel,
        grid=(t // tm,),
        in_specs=[
            row(x2), row(o2), row(ma), row(sgb),
            pl.BlockSpec((1,) + mod3.shape[1:], lambda i: (i // tps, 0, 0)),
            full(g_ffn), full(g_final), full(wob), full(wo), full(wg), full(wu), full(wd),
        ],
        out_specs=pl.BlockSpec((tm, d), lambda i: (i, 0)),
        out_shape=jax.ShapeDtypeStruct((t, d), F32),
        compiler_params=pltpu.CompilerParams(
            dimension_semantics=("arbitrary",), vmem_limit_bytes=V7X_VMEM_LIMIT),
        name="mixer_out_ffn",
    )(x2, o2, ma, sgb, mod3, g_ffn, g_final, wob, wo, wg, wu, wd)


def kernel(x, c, w_ada, b_ada, g_mix, w_in, conv_w, w_out_a, lambda_q1, lambda_k1, lambda_q2,
           lambda_k2, subln_g, w_out_b, w_out, g_ffn, w_gate, w_up, w_down, g_final):
    batch, seq, d = x.shape
    depth = w_ada.shape[0]
    assert depth == 1 and seq % PRE_BLOCK_M == 0 and seq % POST_BLOCK_M == 0
    assert PRE_BLOCK_M % ATTN_KBLOCK == 0 and ATTN_KBLOCK % CHUNK == 0
    assert ATTN_KBLOCK % (PRE_BLOCK_M // PRE_ROW_GROUPS) == 0 and PRE_BLOCK_M % (8 * PRE_ROW_GROUPS) == 0
    assert POST_BLOCK_M % (8 * POST_ROW_GROUPS) == 0
    assert ATTN_QBLOCK == 2 * ATTN_KBLOCK and seq % ATTN_QBLOCK == 0
    l = 0
    x2 = x.reshape(batch * seq, d)

    c_pad = jnp.pad(c, ((0, 8 - batch), (0, 0)))
    mod = _ada_call(c_pad, w_ada[l], b_ada[l].reshape(1, -1))
    mod3 = mod[:batch].reshape(batch, 6, d)

    ma, sgb, qt, k2, vt, k_sq = _pre_call(
        x2, mod3, g_mix[l].reshape(1, d), w_in[l].astype(BF16), conv_w[l],
        w_out_a[l].astype(BF16), batch=batch, seq=seq)

    lam_params = jnp.stack([lambda_q1[l], lambda_k1[l], lambda_q2[l], lambda_k2[l]], axis=0)
    o = _attn_call(qt, k2.reshape(batch, seq, -1), vt, k_sq.reshape(batch, -1, 8, 128),
                   lam_params, subln_g[l].reshape(1, DV), batch=batch, seq=seq)

    y = _post_call(
        x2, o.reshape(batch * seq, -1), ma, sgb, mod3, g_ffn[l].reshape(1, d),
        g_final.reshape(1, d), w_out_b[l].astype(BF16), w_out[l].astype(BF16),
        w_gate[l].astype(BF16), w_up[l].astype(BF16), w_down[l].astype(BF16), seq=seq)
    return y.reshape(batch, seq, d)
```

```python
import functools
import math

import jax
import jax.numpy as jnp
import ml_dtypes
import numpy as np
from jax import lax
from jax.experimental import pallas as pl
from jax.experimental.pallas import tpu as pltpu

CHUNK = 64
CONV_WIDTH = 3
N_HEADS = 4
DH = 64
DV = 2 * DH
EPS = 1e-6
NEG_INF = -1e30
LAMBDA_INIT = 0.8 - 0.6 * math.exp(-0.3 * 1)
LOG2E = math.log2(math.e)
ALIBI_SLOPES = tuple(2.0 ** (-8.0 * (i + 1) / N_HEADS) for i in range(N_HEADS))

ADA_BLOCK_N = 1536
PRE_BLOCK_M = 1024
PRE_ROW_GROUPS = 4
POST_BLOCK_M = 1024
POST_ROW_GROUPS = 4
ATTN_KBLOCK = 512
ATTN_QBLOCK = 1024
ATTN_ONES_ROWS = 16
ATTN_POS_COLS = 3
ATTN_BOUND_SLACK = 1.01
ATTN_BOUND_MARGIN = 0.01
ATTN_MIN_EXPONENT = -100.0
V7X_VMEM_LIMIT = 58 * 1024 * 1024

BF16 = jnp.bfloat16
F32 = jnp.float32


def _dot(a, b):
    return jnp.dot(a, b, preferred_element_type=F32)


def _sigmoid(x):
    return 1.0 / (1.0 + jnp.exp(-x))


def _ada_kernel(c_ref, w_ref, b_ref, o_ref):
    c = c_ref[...]
    sc = (c * _sigmoid(c)).astype(BF16)
    o_ref[...] = _dot(sc, w_ref[...].astype(BF16)) + b_ref[...]


def _ada_call(c_pad, w_ada, b_ada):
    rows, d = c_pad.shape
    n = w_ada.shape[1]
    return pl.pallas_call(
        _ada_kernel,
        grid=(n // ADA_BLOCK_N,),
        in_specs=[
            pl.BlockSpec((rows, d), lambda j: (0, 0)),
            pl.BlockSpec((d, ADA_BLOCK_N), lambda j: (0, j)),
            pl.BlockSpec((1, ADA_BLOCK_N), lambda j: (0, j)),
        ],
        out_specs=pl.BlockSpec((rows, ADA_BLOCK_N), lambda j: (0, j)),
        out_shape=jax.ShapeDtypeStruct((rows, n), F32),
        compiler_params=pltpu.CompilerParams(
            dimension_semantics=("arbitrary",), vmem_limit_bytes=V7X_VMEM_LIMIT),
        name="ada_mod",
    )(c_pad, w_ada, b_ada)


def _pre_kernel(x_ref, mod_ref, g_ref, win_ref, cw_ref, woa_ref,
                ma_ref, sgb_ref, qt_ref, k_ref, vt_ref, ksq_ref, carry_ref,
                *, tiles_per_seq, d_conv, d_attn, d_model):
    i = pl.program_id(0)
    tm = x_ref.shape[0]
    mod = mod_ref[0]

    @pl.when(i % tiles_per_seq == 0)
    def _():
        carry_ref[...] = jnp.zeros_like(carry_ref)

    rg = tm // PRE_ROW_GROUPS
    groups = [slice(g * rg, (g + 1) * rg) for g in range(PRE_ROW_GROUPS)]

    a0 = 3 * d_conv
    g0 = a0 + 3 * d_attn

    def normed(rows):
        x = x_ref[rows, :]
        xn = x * lax.rsqrt(jnp.mean(x * x, axis=-1, keepdims=True) + EPS)
        return ((xn * g_ref[...]) * (1.0 + mod[1:2, :]) + mod[0:1, :]).astype(BF16)

    hb = [normed(rows) for rows in groups]

    pc = [_dot(h, win_ref[:, 0:a0]) for h in hb]
    cw = cw_ref[...]
    before = carry_ref[...]
    y_a = []
    for p in pc:
        u = p[:, 0:d_conv]
        gb = p[:, d_conv:2 * d_conv]
        gc = p[:, 2 * d_conv:3 * d_conv]
        cu = gc * u
        ext = jnp.concatenate([before, cu], axis=0)
        prev1 = pltpu.roll(ext, 1, axis=0)[8:, :]
        prev2 = pltpu.roll(ext, 2, axis=0)[8:, :]
        before = cu[rg - 8:, :]
        z = cw[0:1, :] * prev2 + cw[1:2, :] * prev1 + cw[2:3, :] * cu
        y_a.append(_dot((gb * z).astype(BF16), woa_ref[...]))
    carry_ref[...] = before

    for rows, h, ya in zip(groups, hb, y_a):
        pg = _dot(h, win_ref[:, g0:g0 + 2 * d_model])
        ma_ref[rows, :] = (_sigmoid(pg[:, 0:d_model]) * ya).astype(BF16)
        sgb_ref[rows, :] = _sigmoid(pg[:, d_model:2 * d_model]).astype(BF16)

    lane = lax.broadcasted_iota(jnp.int32, ksq_ref.shape[1:], 1)
    tile = jnp.zeros(ksq_ref.shape[1:], F32)
    for g, (rows, h) in enumerate(zip(groups, hb)):
        pa = _dot(h, win_ref[:, a0:g0])
        q = pa[:, 0:d_attn] * (LOG2E / math.sqrt(DH))
        qt_ref[0, :, rows] = q.T.astype(BF16)
        kb = pa[:, d_attn:2 * d_attn].astype(BF16)
        k_ref[rows, :] = kb
        kf = kb.astype(F32)
        sq = kf * kf
        for hd in range(N_HEADS):
            norm_sq = jnp.sum(sq[:, hd * DV:(hd + 1) * DV], axis=-1, keepdims=True)
            tile = jnp.where(lane == hd,
                             jnp.maximum(tile, jnp.max(norm_sq, axis=0, keepdims=True)), tile)
        v = pa[:, 2 * d_attn:3 * d_attn]
        blk, off = divmod(g * rg, ATTN_KBLOCK)
        vt_ref[0, blk, :, off:off + rg] = v.T.astype(BF16)
    ksq_ref[0] = tile


def _pre_call(x2, mod3, g_mix, w_in_b, conv_w, w_out_a_b, *, batch, seq):
    t, d = x2.shape
    d_conv = conv_w.shape[1]
    d_attn = N_HEADS * DV
    tm = PRE_BLOCK_M
    tps = seq // tm
    nkb = seq // ATTN_KBLOCK
    kern = functools.partial(_pre_kernel, tiles_per_seq=tps, d_conv=d_conv,
                             d_attn=d_attn, d_model=d)
    full = lambda a: pl.BlockSpec(a.shape, lambda i: (0,) * a.ndim)
    return pl.pallas_call(
        kern,
        grid=(t // tm,),
        in_specs=[
            pl.BlockSpec((tm, d), lambda i: (i, 0)),
            pl.BlockSpec((1,) + mod3.shape[1:], lambda i: (i // tps, 0, 0)),
            full(g_mix), full(w_in_b), full(conv_w), full(w_out_a_b),
        ],
        out_specs=[
            pl.BlockSpec((tm, d), lambda i: (i, 0)),
            pl.BlockSpec((tm, d), lambda i: (i, 0)),
            pl.BlockSpec((1, d_attn, tm), lambda i: (i // tps, 0, i % tps)),
            pl.BlockSpec((tm, d_attn), lambda i: (i, 0)),
            pl.BlockSpec((1, tm // ATTN_KBLOCK, d_attn, ATTN_KBLOCK),
                         lambda i: (i // tps, i % tps, 0, 0)),
            pl.BlockSpec((1, 8, 128), lambda i: (i, 0, 0)),
        ],
        out_shape=[
            jax.ShapeDtypeStruct((t, d), BF16),
            jax.ShapeDtypeStruct((t, d), BF16),
            jax.ShapeDtypeStruct((batch, d_attn, seq), BF16),
            jax.ShapeDtypeStruct((t, d_attn), BF16),
            jax.ShapeDtypeStruct((batch, nkb, d_attn, ATTN_KBLOCK), BF16),
            jax.ShapeDtypeStruct((t // tm, 8, 128), F32),
        ],
        scratch_shapes=[pltpu.VMEM((8, d_conv), F32)],
        compiler_params=pltpu.CompilerParams(
            dimension_semantics=("arbitrary",), vmem_limit_bytes=V7X_VMEM_LIMIT),
        name="mixer_in",
    )(x2, mod3, g_mix, w_in_b, conv_w, w_out_a_b)


def _attn_kernel(slope_ref, qt_ref, k_ref, vt_ref, ksq_ref, bias_ref, kpos_ref, lam_ref, g_ref,
                 o_ref, acc_ref, s_ref, p_ref):
    h = pl.program_id(1)
    qi = pl.program_id(2)
    tq, tk = ATTN_QBLOCK, ATTN_KBLOCK
    ratio = tq // tk
    slope = slope_ref[h]

    qt = qt_ref[0]
    row = lax.broadcasted_iota(jnp.int32, qt.shape, 0)
    zero = jnp.zeros_like(qt)
    qmaps = (jnp.where(row < DH, qt, zero), jnp.where(row >= DH, qt, zero))
    ones = jnp.ones((ATTN_ONES_ROWS, tk), BF16)

    last = ratio * qi + ratio - 1

    def key_block(kj):
        return k_ref[0, pl.ds(pl.multiple_of(kj * tk, tk), tk), :]

    def bias_tile(kj):
        return bias_ref[0, jnp.clip(kj - ratio * qi + 1, 0, ratio)]

    def bias_shift(kj):
        return -slope * (qi * tq - kj * tk).astype(F32)

    def accumulate(kj, slot, alphas=None):
        vb = jnp.concatenate([vt_ref[0, kj], ones], axis=0)
        for c in range(2):
            old = acc_ref[c] if alphas is None else alphas[c] * acc_ref[c]
            acc_ref[c] = old + _dot(vb, p_ref[slot, c])

    k_sq_all = jnp.max(ksq_ref[0], axis=0)
    lane = lax.broadcasted_iota(jnp.int32, k_sq_all.shape, 1)
    k_sq = jnp.max(jnp.where(lane == h, k_sq_all, 0.0), axis=-1, keepdims=True)[0:1]
    stab = []
    for c in range(2):
        qf = qmaps[c].astype(F32)
        q_sq = jnp.sum(qf * qf, axis=0, keepdims=True)
        stab.append(jnp.sqrt(q_sq * k_sq) * ATTN_BOUND_SLACK + ATTN_BOUND_MARGIN)

    def store_exponentials(e, slot, c, lo, hi, e_max):
        p_ref[slot, c, :, lo:hi] = jnp.exp2(e).astype(BF16)
        return jnp.maximum(e_max, jnp.max(e, axis=0, keepdims=True))

    def exponentials_diagonal(kj, tile, slot, lo, hi, e_max):
        kb = key_block(kj)
        bias = bias_ref[0, tile, :, lo:hi]
        shift = bias_shift(kj)
        return tuple(store_exponentials(
            _dot(kb, qmaps[c][:, lo:hi]) + bias + (shift - stab[c][:, lo:hi]), slot, c, lo, hi,
            e_max[c]) for c in range(2))

    lane_k = lax.broadcasted_iota(jnp.int32, (tk, 2 * DH), 1)
    ones_rows = (((row >= DH) & (row < DH + ATTN_POS_COLS)).astype(F32).astype(BF16),
                 (row < ATTN_POS_COLS).astype(F32).astype(BF16))
    q_aug = tuple(qmaps[c] + ones_rows[c] for c in range(2))
    pos = lax.broadcasted_iota(jnp.int32, (1, tq), 1).astype(F32)
    base = tuple(stab[c] + slope * pos for c in range(2))

    def exponentials_below(kj, slot, lo, hi, e_max):
        kb = key_block(kj)
        lhs = (jnp.where(lane_k < DH, kb, kpos_ref[0, 0]), jnp.where(lane_k >= DH, kb, kpos_ref[0, 1]))
        shift = bias_shift(kj)
        return tuple(store_exponentials(
            _dot(lhs[c], q_aug[c][:, lo:hi]) - (base[c][:, lo:hi] - shift), slot, c, lo, hi,
            e_max[c]) for c in range(2))

    def accumulate_columns(kj, slot, lo, hi):
        vb = jnp.concatenate([vt_ref[0, kj], ones], axis=0)
        for c in range(2):
            acc_ref[c, :, lo:hi] = acc_ref[c, :, lo:hi] + _dot(vb, p_ref[slot, c, :, lo:hi])

    acc_ref[...] = jnp.zeros_like(acc_ref)
    e0 = jnp.full((1, tq), NEG_INF, F32)
    half = tq // 2

    def diagonal_region(e_max, pending_pv=()):
        d0 = 2 * qi
        left = tuple(e[:, 0:half] for e in e_max)
        right = tuple(e[:, half:tq] for e in e_max)
        for stage in pending_pv[0:1]:
            stage()
        left = exponentials_diagonal(d0, 1, 0, 0, half, left)
        right = exponentials_below(d0, 0, half, tq, right)
        for stage in pending_pv[1:2]:
            stage()
        right = exponentials_diagonal(d0 + 1, 2, 1, half, tq, right)
        accumulate(d0, 0)
        accumulate_columns(d0 + 1, 1, half, tq)
        return left, right

    def first_query_block():
        return diagonal_region((e0, e0))

    def later_query_block():
        def trip(t, e_max):
            e_max = exponentials_below(2 * t + 1, 1, 0, tq, e_max)
            accumulate(2 * t, 0)
            e_max = exponentials_below(2 * t + 2, 0, 0, tq, e_max)
            accumulate(2 * t + 1, 1)
            return e_max

        e_max = lax.fori_loop(0, qi - 1, trip, exponentials_below(0, 0, 0, tq, (e0, e0)))
        d0 = 2 * qi
        e_max = exponentials_below(d0 - 1, 1, 0, tq, e_max)
        return diagonal_region(e_max, (lambda: accumulate(d0 - 2, 0), lambda: accumulate(d0 - 1, 1)))

    left, right = lax.cond(qi == 0, first_query_block, later_query_block)
    smallest = functools.reduce(jnp.minimum, [jnp.min(e) for e in left + right])

    @pl.when(smallest < ATTN_MIN_EXPONENT)
    def _():
        def body(kj, m_run):
            kb = key_block(kj)
            bias = bias_tile(kj)
            shift = bias_shift(kj)
            m_new, alpha = [], []
            for c in range(2):
                s = _dot(kb, qmaps[c]) + bias
                s_ref[c] = s
                m_c = jnp.maximum(m_run[c], jnp.max(s, axis=0, keepdims=True) + shift)
                alpha.append(jnp.exp2(m_run[c] - m_c))
                p_ref[0, c] = jnp.exp2(s_ref[c] - (m_c - shift)).astype(BF16)
                m_new.append(m_c)
            accumulate(kj, 0, alpha)
            return tuple(m_new)

        acc_ref[...] = jnp.zeros_like(acc_ref)
        lax.fori_loop(0, last + 1, body, (e0, e0))

    lp = lam_ref[...]
    lam = (jnp.exp(jnp.sum(lp[0:1, :] * lp[1:2, :], axis=-1, keepdims=True))
           - jnp.exp(jnp.sum(lp[2:3, :] * lp[3:4, :], axis=-1, keepdims=True))
           + LAMBDA_INIT)
    a1 = acc_ref[0]
    a2 = acc_ref[1]
    ot = (a1[0:DV, :] * (1.0 / a1[DV:DV + 1, :])
          - (lam * a2[0:DV, :]) * (1.0 / a2[DV:DV + 1, :]))
    o = ot.T
    o = o * lax.rsqrt(jnp.mean(o * o, axis=-1, keepdims=True) + EPS)
    o_ref[0] = ((o * g_ref[...]) * (1.0 - LAMBDA_INIT)).astype(BF16)


def _attn_bias_tiles():
    tq, tk = ATTN_QBLOCK, ATTN_KBLOCK
    jj = np.arange(tk, dtype=np.int64)[:, None]
    ii = np.arange(tq, dtype=np.int64)[None, :]
    slopes = np.asarray(ALIBI_SLOPES, np.float64)[:, None, None]
    tiles = [-slopes * (ii - jj)[None].astype(np.float64)]
    for r in range(tq // tk):
        key = r * tk + jj
        allowed = (key // CHUNK) <= (ii // CHUNK)
        tiles.append(np.where(allowed[None], -slopes * (np.abs(ii - key)[None] + r * tk), NEG_INF))
    return (LOG2E * np.stack(tiles, axis=1)).astype(np.float32)


def _attn_key_position_columns():
    tk = ATTN_KBLOCK
    rest = LOG2E * np.asarray(ALIBI_SLOPES, np.float64)[:, None] * np.arange(tk, dtype=np.float64)[None, :]
    out = np.zeros((N_HEADS, 2, tk, 2 * DH), np.float32)
    for n in range(ATTN_POS_COLS):
        part = rest.astype(ml_dtypes.bfloat16).astype(np.float64)
        out[:, 0, :, DH + n] = part
        out[:, 1, :, n] = part
        rest = rest - part
    return out.astype(ml_dtypes.bfloat16)


def _attn_call(qt, k3, vt, k_sq, lam_params, subln_g, *, batch, seq):
    tq, tk = ATTN_QBLOCK, ATTN_KBLOCK
    nkb = seq // tk
    d_attn = N_HEADS * DV
    bias = jnp.asarray(_attn_bias_tiles())
    kpos = jnp.asarray(_attn_key_position_columns())
    slopes = jnp.asarray([LOG2E * a for a in ALIBI_SLOPES], F32)
    return pl.pallas_call(
        _attn_kernel,
        grid=(batch, N_HEADS, seq // tq),
        in_specs=[
            pl.BlockSpec(memory_space=pltpu.SMEM),
            pl.BlockSpec((1, DV, tq), lambda b, h, q: (b, h, q)),
            pl.BlockSpec((1, seq, DV), lambda b, h, q: (b, 0, h)),
            pl.BlockSpec((1, nkb, DV, tk), lambda b, h, q: (b, 0, h, 0)),
            pl.BlockSpec((1,) + k_sq.shape[1:], lambda b, h, q: (b, 0, 0, 0)),
            pl.BlockSpec((1,) + bias.shape[1:], lambda b, h, q: (h, 0, 0, 0)),
            pl.BlockSpec((1,) + kpos.shape[1:], lambda b, h, q: (h, 0, 0, 0)),
            pl.BlockSpec(lam_params.shape, lambda b, h, q: (0, 0)),
            pl.BlockSpec(subln_g.shape, lambda b, h, q: (0, 0)),
        ],
        out_specs=pl.BlockSpec((1, tq, DV), lambda b, h, q: (b, q, h)),
        out_shape=jax.ShapeDtypeStruct((batch, seq, d_attn), BF16),
        scratch_shapes=[pltpu.VMEM((2, DV + ATTN_ONES_ROWS, tq), F32),
                        pltpu.VMEM((2, tk, tq), F32),
                        pltpu.VMEM((2, 2, tk, tq), BF16)],
        compiler_params=pltpu.CompilerParams(
            dimension_semantics=("arbitrary", "arbitrary", "arbitrary"),
            vmem_limit_bytes=V7X_VMEM_LIMIT),
        name="diff_attn",
    )(slopes, qt, k3, vt, k_sq, bias, kpos, lam_params, subln_g)


def _post_kernel(x_ref, o_ref, ma_ref, sgb_ref, mod_ref, gf_ref, gfin_ref,
                 wob_ref, wo_ref, wg_ref, wu_ref, wd_ref, y_ref):
    mod = mod_ref[0]
    tm = x_ref.shape[0]
    groups = [slice(g * tm // POST_ROW_GROUPS, (g + 1) * tm // POST_ROW_GROUPS)
              for g in range(POST_ROW_GROUPS)]

    def mixer_residual(rows):
        y_b = _dot(o_ref[rows, :], wob_ref[...])
        merged = ma_ref[rows, :].astype(F32) + sgb_ref[rows, :].astype(F32) * y_b
        return x_ref[rows, :] + mod[2:3, :] * _dot(merged.astype(BF16), wo_ref[...])

    def ffn_input(x1):
        xn = x1 * lax.rsqrt(jnp.mean(x1 * x1, axis=-1, keepdims=True) + EPS)
        return ((xn * gf_ref[...]) * (1.0 + mod[4:5, :]) + mod[3:4, :]).astype(BF16)

    def ffn_hidden(h2):
        gate = _dot(h2, wg_ref[...])
        up = _dot(h2, wu_ref[...])
        return ((gate * _sigmoid(gate)) * up).astype(BF16)

    def finish(rows, x1, act):
        x2 = x1 + mod[5:6, :] * _dot(act, wd_ref[...])
        y_ref[rows, :] = (x2 * lax.rsqrt(jnp.mean(x2 * x2, axis=-1, keepdims=True) + EPS)) * gfin_ref[...]

    x1 = [mixer_residual(rows) for rows in groups]
    h2 = [ffn_input(v) for v in x1]
    act = [ffn_hidden(v) for v in h2]
    for rows, a, b in zip(groups, x1, act):
        finish(rows, a, b)


def _post_call(x2, o2, ma, sgb, mod3, g_ffn, g_final, wob, wo, wg, wu, wd, *, seq):
    t, d = x2.shape
    tm = POST_BLOCK_M
    tps = seq // tm
    full = lambda a: pl.BlockSpec(a.shape, lambda i: (0,) * a.ndim)
    row = lambda a: pl.BlockSpec((tm, a.shape[1]), lambda i: (i, 0))
    return pl.pallas_call(
        _post_kernel,
        grid=(t // tm,),
        in_specs=[
            row(x2), row(o2), row(ma), row(sgb),
            pl.BlockSpec((1,) + mod3.shape[1:], lambda i: (i // tps, 0, 0)),
            full(g_ffn), full(g_final), full(wob), full(wo), full(wg), full(wu), full(wd),
        ],
        out_specs=pl.BlockSpec((tm, d), lambda i: (i, 0)),
        out_shape=jax.ShapeDtypeStruct((t, d), F32),
        compiler_params=pltpu.CompilerParams(
            dimension_semantics=("arbitrary",), vmem_limit_bytes=V7X_VMEM_LIMIT),
        name="mixer_out_ffn",
    )(x2, o2, ma, sgb, mod3, g_ffn, g_final, wob, wo, wg, wu, wd)


def kernel(x, c, w_ada, b_ada, g_mix, w_in, conv_w, w_out_a, lambda_q1, lambda_k1, lambda_q2,
           lambda_k2, subln_g, w_out_b, w_out, g_ffn, w_gate, w_up, w_down, g_final):
    batch, seq, d = x.shape
    depth = w_ada.shape[0]
    assert depth == 1 and seq % PRE_BLOCK_M == 0 and seq % POST_BLOCK_M == 0
    assert PRE_BLOCK_M % ATTN_KBLOCK == 0 and ATTN_KBLOCK % CHUNK == 0
    assert ATTN_KBLOCK % (PRE_BLOCK_M // PRE_ROW_GROUPS) == 0 and PRE_BLOCK_M % (8 * PRE_ROW_GROUPS) == 0
    assert POST_BLOCK_M % (8 * POST_ROW_GROUPS) == 0
    assert ATTN_QBLOCK == 2 * ATTN_KBLOCK and seq % ATTN_QBLOCK == 0
    l = 0
    x2 = x.reshape(batch * seq, d)

    c_pad = jnp.pad(c, ((0, 8 - batch), (0, 0)))
    mod = _ada_call(c_pad, w_ada[l], b_ada[l].reshape(1, -1))
    mod3 = mod[:batch].reshape(batch, 6, d)

    ma, sgb, qt, k2, vt, k_sq = _pre_call(
        x2, mod3, g_mix[l].reshape(1, d), w_in[l].astype(BF16), conv_w[l],
        w_out_a[l].astype(BF16), batch=batch, seq=seq)

    lam_params = jnp.stack([lambda_q1[l], lambda_k1[l], lambda_q2[l], lambda_k2[l]], axis=0)
    o = _attn_call(qt, k2.reshape(batch, seq, -1), vt, k_sq.reshape(batch, -1, 8, 128),
                   lam_params, subln_g[l].reshape(1, DV), batch=batch, seq=seq)

    y = _post_call(
        x2, o.reshape(batch * seq, -1), ma, sgb, mod3, g_ffn[l].reshape(1, d),
        g_final.reshape(1, d), w_out_b[l].astype(BF16), w_out[l].astype(BF16),
        w_gate[l].astype(BF16), w_up[l].astype(BF16), w_down[l].astype(BF16), seq=seq)
    return y.reshape(batch, seq, d)
```

```python
import functools
import math

import jax
import jax.numpy as jnp
import ml_dtypes
import numpy as np
from jax import lax
from jax.experimental import pallas as pl
from jax.experimental.pallas import tpu as pltpu

CHUNK = 64
CONV_WIDTH = 3
N_HEADS = 4
DH = 64
DV = 2 * DH
EPS = 1e-6
NEG_INF = -1e30
LAMBDA_INIT = 0.8 - 0.6 * math.exp(-0.3 * 1)
LOG2E = math.log2(math.e)
ALIBI_SLOPES = tuple(2.0 ** (-8.0 * (i + 1) / N_HEADS) for i in range(N_HEADS))

ADA_BLOCK_N = 1536
PRE_BLOCK_M = 1024
PRE_ROW_GROUPS = 4
POST_BLOCK_M = 512
POST_ROW_GROUPS = 2
ATTN_KBLOCK = 512
ATTN_QBLOCK = 1024
ATTN_ONES_ROWS = 16
ATTN_POS_COLS = 3
ATTN_BOUND_SLACK = 1.01
ATTN_BOUND_MARGIN = 0.01
ATTN_MIN_EXPONENT = -100.0
V7X_VMEM_LIMIT = 56 * 1024 * 1024

BF16 = jnp.bfloat16
F32 = jnp.float32


def _dot(a, b):
    return jnp.dot(a, b, preferred_element_type=F32)


def _sigmoid(x):
    return 1.0 / (1.0 + jnp.exp(-x))


def _ada_kernel(c_ref, w_ref, b_ref, o_ref):
    c = c_ref[...]
    sc = (c * _sigmoid(c)).astype(BF16)
    o_ref[...] = _dot(sc, w_ref[...].astype(BF16)) + b_ref[...]


def _ada_call(c_pad, w_ada, b_ada):
    rows, d = c_pad.shape
    n = w_ada.shape[1]
    return pl.pallas_call(
        _ada_kernel,
        grid=(n // ADA_BLOCK_N,),
        in_specs=[
            pl.BlockSpec((rows, d), lambda j: (0, 0)),
            pl.BlockSpec((d, ADA_BLOCK_N), lambda j: (0, j)),
            pl.BlockSpec((1, ADA_BLOCK_N), lambda j: (0, j)),
        ],
        out_specs=pl.BlockSpec((rows, ADA_BLOCK_N), lambda j: (0, j)),
        out_shape=jax.ShapeDtypeStruct((rows, n), F32),
        compiler_params=pltpu.CompilerParams(
            dimension_semantics=("arbitrary",), vmem_limit_bytes=V7X_VMEM_LIMIT),
        name="ada_mod",
    )(c_pad, w_ada, b_ada)


def _pre_kernel(x_ref, mod_ref, g_ref, win_ref, cw_ref, woa_ref,
                ma_ref, sgb_ref, qt_ref, k_ref, vt_ref, ksq_ref, carry_ref,
                *, tiles_per_seq, d_conv, d_attn, d_model):
    i = pl.program_id(0)
    tm = x_ref.shape[0]
    mod = mod_ref[0]

    @pl.when(i % tiles_per_seq == 0)
    def _():
        carry_ref[...] = jnp.zeros_like(carry_ref)

    rg = tm // PRE_ROW_GROUPS
    groups = [slice(g * rg, (g + 1) * rg) for g in range(PRE_ROW_GROUPS)]

    a0 = 3 * d_conv
    g0 = a0 + 3 * d_attn

    def normed(rows):
        x = x_ref[rows, :]
        xn = x * lax.rsqrt(jnp.mean(x * x, axis=-1, keepdims=True) + EPS)
        return ((xn * g_ref[...]) * (1.0 + mod[1:2, :]) + mod[0:1, :]).astype(BF16)

    hb = [normed(rows) for rows in groups]

    pc = [_dot(h, win_ref[:, 0:a0]) for h in hb]
    cw = cw_ref[...]
    before = carry_ref[...]
    y_a = []
    for p in pc:
        u = p[:, 0:d_conv]
        gb = p[:, d_conv:2 * d_conv]
        gc = p[:, 2 * d_conv:3 * d_conv]
        cu = gc * u
        ext = jnp.concatenate([before, cu], axis=0)
        prev1 = pltpu.roll(ext, 1, axis=0)[8:, :]
        prev2 = pltpu.roll(ext, 2, axis=0)[8:, :]
        before = cu[rg - 8:, :]
        z = cw[0:1, :] * prev2 + cw[1:2, :] * prev1 + cw[2:3, :] * cu
        y_a.append(_dot((gb * z).astype(BF16), woa_ref[...]))
    carry_ref[...] = before

    for rows, h, ya in zip(groups, hb, y_a):
        pg = _dot(h, win_ref[:, g0:g0 + 2 * d_model])
        ma_ref[rows, :] = (_sigmoid(pg[:, 0:d_model]) * ya).astype(BF16)
        sgb_ref[rows, :] = _sigmoid(pg[:, d_model:2 * d_model]).astype(BF16)

    lane = lax.broadcasted_iota(jnp.int32, ksq_ref.shape[1:], 1)
    tile = jnp.zeros(ksq_ref.shape[1:], F32)
    for g, (rows, h) in enumerate(zip(groups, hb)):
        pa = _dot(h, win_ref[:, a0:g0])
        q = pa[:, 0:d_attn] * (LOG2E / math.sqrt(DH))
        qt_ref[0, :, rows] = q.T.astype(BF16)
        kb = pa[:, d_attn:2 * d_attn].astype(BF16)
        k_ref[rows, :] = kb
        kf = kb.astype(F32)
        sq = kf * kf
        for hd in range(N_HEADS):
            norm_sq = jnp.sum(sq[:, hd * DV:(hd + 1) * DV], axis=-1, keepdims=True)
            tile = jnp.where(lane == hd,
                             jnp.maximum(tile, jnp.max(norm_sq, axis=0, keepdims=True)), tile)
        v = pa[:, 2 * d_attn:3 * d_attn]
        blk, off = divmod(g * rg, ATTN_KBLOCK)
        vt_ref[0, blk, :, off:off + rg] = v.T.astype(BF16)
    ksq_ref[0] = tile


def _pre_call(x2, mod3, g_mix, w_in_b, conv_w, w_out_a_b, *, batch, seq):
    t, d = x2.shape
    d_conv = conv_w.shape[1]
    d_attn = N_HEADS * DV
    tm = PRE_BLOCK_M
    tps = seq // tm
    nkb = seq // ATTN_KBLOCK
    kern = functools.partial(_pre_kernel, tiles_per_seq=tps, d_conv=d_conv,
                             d_attn=d_attn, d_model=d)
    full = lambda a: pl.BlockSpec(a.shape, lambda i: (0,) * a.ndim)
    return pl.pallas_call(
        kern,
        grid=(t // tm,),
        in_specs=[
            pl.BlockSpec((tm, d), lambda i: (i, 0)),
            pl.BlockSpec((1,) + mod3.shape[1:], lambda i: (i // tps, 0, 0)),
            full(g_mix), full(w_in_b), full(conv_w), full(w_out_a_b),
        ],
        out_specs=[
            pl.BlockSpec((tm, d), lambda i: (i, 0)),
            pl.BlockSpec((tm, d), lambda i: (i, 0)),
            pl.BlockSpec((1, d_attn, tm), lambda i: (i // tps, 0, i % tps)),
            pl.BlockSpec((tm, d_attn), lambda i: (i, 0)),
            pl.BlockSpec((1, tm // ATTN_KBLOCK, d_attn, ATTN_KBLOCK),
                         lambda i: (i // tps, i % tps, 0, 0)),
            pl.BlockSpec((1, 8, 128), lambda i: (i, 0, 0)),
        ],
        out_shape=[
            jax.ShapeDtypeStruct((t, d), BF16),
            jax.ShapeDtypeStruct((t, d), BF16),
            jax.ShapeDtypeStruct((batch, d_attn, seq), BF16),
            jax.ShapeDtypeStruct((t, d_attn), BF16),
            jax.ShapeDtypeStruct((batch, nkb, d_attn, ATTN_KBLOCK), BF16),
            jax.ShapeDtypeStruct((t // tm, 8, 128), F32),
        ],
        scratch_shapes=[pltpu.VMEM((8, d_conv), F32)],
        compiler_params=pltpu.CompilerParams(
            dimension_semantics=("arbitrary",), vmem_limit_bytes=V7X_VMEM_LIMIT),
        name="mixer_in",
    )(x2, mod3, g_mix, w_in_b, conv_w, w_out_a_b)


def _attn_kernel(slope_ref, qt_ref, k_ref, vt_ref, ksq_ref, bias_ref, kpos_ref, lam_ref, g_ref,
                 o_ref, acc_ref, s_ref, p_ref):
    h = pl.program_id(1)
    qi = pl.program_id(2)
    tq, tk = ATTN_QBLOCK, ATTN_KBLOCK
    ratio = tq // tk
    slope = slope_ref[h]

    qt = qt_ref[0]
    row = lax.broadcasted_iota(jnp.int32, qt.shape, 0)
    zero = jnp.zeros_like(qt)
    qmaps = (jnp.where(row < DH, qt, zero), jnp.where(row >= DH, qt, zero))
    ones = jnp.ones((ATTN_ONES_ROWS, tk), BF16)

    last = ratio * qi + ratio - 1

    def key_block(kj):
        return k_ref[0, pl.ds(pl.multiple_of(kj * tk, tk), tk), :]

    def bias_tile(kj):
        return bias_ref[0, jnp.clip(kj - ratio * qi + 1, 0, ratio)]

    def bias_shift(kj):
        return -slope * (qi * tq - kj * tk).astype(F32)

    def accumulate(kj, slot, alphas=None):
        vb = jnp.concatenate([vt_ref[0, kj], ones], axis=0)
        for c in range(2):
            old = acc_ref[c] if alphas is None else alphas[c] * acc_ref[c]
            acc_ref[c] = old + _dot(vb, p_ref[slot, c])

    k_sq_all = jnp.max(ksq_ref[0], axis=0)
    lane = lax.broadcasted_iota(jnp.int32, k_sq_all.shape, 1)
    k_sq = jnp.max(jnp.where(lane == h, k_sq_all, 0.0), axis=-1, keepdims=True)[0:1]
    stab = []
    for c in range(2):
        qf = qmaps[c].astype(F32)
        q_sq = jnp.sum(qf * qf, axis=0, keepdims=True)
        stab.append(jnp.sqrt(q_sq * k_sq) * ATTN_BOUND_SLACK + ATTN_BOUND_MARGIN)

    def store_exponentials(e, slot, c, lo, hi, e_max):
        p_ref[slot, c, :, lo:hi] = jnp.exp2(e).astype(BF16)
        return jnp.maximum(e_max, jnp.max(e, axis=0, keepdims=True))

    def exponentials_diagonal(kj, tile, slot, lo, hi, e_max):
        kb = key_block(kj)
        bias = bias_ref[0, tile, :, lo:hi]
        shift = bias_shift(kj)
        return tuple(store_exponentials(
            _dot(kb, qmaps[c][:, lo:hi]) + bias + (shift - stab[c][:, lo:hi]), slot, c, lo, hi,
            e_max[c]) for c in range(2))

    lane_k = lax.broadcasted_iota(jnp.int32, (tk, 2 * DH), 1)
    ones_rows = (((row >= DH) & (row < DH + ATTN_POS_COLS)).astype(F32).astype(BF16),
                 (row < ATTN_POS_COLS).astype(F32).astype(BF16))
    q_aug = tuple(qmaps[c] + ones_rows[c] for c in range(2))
    pos = lax.broadcasted_iota(jnp.int32, (1, tq), 1).astype(F32)
    base = tuple(stab[c] + slope * pos for c in range(2))

    def exponentials_below(kj, slot, lo, hi, e_max):
        kb = key_block(kj)
        lhs = (jnp.where(lane_k < DH, kb, kpos_ref[0, 0]), jnp.where(lane_k >= DH, kb, kpos_ref[0, 1]))
        shift = bias_shift(kj)
        return tuple(store_exponentials(
            _dot(lhs[c], q_aug[c][:, lo:hi]) - (base[c][:, lo:hi] - shift), slot, c, lo, hi,
            e_max[c]) for c in range(2))

    def accumulate_columns(kj, slot, lo, hi):
        vb = jnp.concatenate([vt_ref[0, kj], ones], axis=0)
        for c in range(2):
            acc_ref[c, :, lo:hi] = acc_ref[c, :, lo:hi] + _dot(vb, p_ref[slot, c, :, lo:hi])

    lp = lam_ref[...]
    lam = (jnp.exp(jnp.sum(lp[0:1, :] * lp[1:2, :], axis=-1, keepdims=True))
           - jnp.exp(jnp.sum(lp[2:3, :] * lp[3:4, :], axis=-1, keepdims=True))
           + LAMBDA_INIT)

    def write_output(lo, hi):
        a1 = acc_ref[0, :, lo:hi]
        a2 = acc_ref[1, :, lo:hi]
        ot = (a1[0:DV, :] * (1.0 / a1[DV:DV + 1, :])
              - (lam * a2[0:DV, :]) * (1.0 / a2[DV:DV + 1, :]))
        o = ot.T
        o = o * lax.rsqrt(jnp.mean(o * o, axis=-1, keepdims=True) + EPS)
        o_ref[0, lo:hi, :] = ((o * g_ref[...]) * (1.0 - LAMBDA_INIT)).astype(BF16)

    acc_ref[...] = jnp.zeros_like(acc_ref)
    e0 = jnp.full((1, tq), NEG_INF, F32)
    half = tq // 2

    def diagonal_region(e_max, pending_pv=()):
        d0 = 2 * qi
        left = tuple(e[:, 0:half] for e in e_max)
        right = tuple(e[:, half:tq] for e in e_max)
        for stage in pending_pv[0:1]:
            stage()
        left = exponentials_diagonal(d0, 1, 0, 0, half, left)
        right = exponentials_below(d0, 0, half, tq, right)
        for stage in pending_pv[1:2]:
            stage()
        right = exponentials_diagonal(d0 + 1, 2, 1, half, tq, right)
        accumulate_columns(d0, 0, 0, half)
        write_output(0, half)
        accumulate_columns(d0, 0, half, tq)
        accumulate_columns(d0 + 1, 1, half, tq)
        write_output(half, tq)
        return left, right

    def first_query_block():
        return diagonal_region((e0, e0))

    def later_query_block():
        def trip(t, e_max):
            e_max = exponentials_below(2 * t + 1, 1, 0, tq, e_max)
            accumulate(2 * t, 0)
            e_max = exponentials_below(2 * t + 2, 0, 0, tq, e_max)
            accumulate(2 * t + 1, 1)
            return e_max

        e_max = lax.fori_loop(0, qi - 1, trip, exponentials_below(0, 0, 0, tq, (e0, e0)))
        d0 = 2 * qi
        e_max = exponentials_below(d0 - 1, 1, 0, tq, e_max)
        return diagonal_region(e_max, (lambda: accumulate(d0 - 2, 0), lambda: accumulate(d0 - 1, 1)))

    left, right = lax.cond(qi == 0, first_query_block, later_query_block)
    smallest = functools.reduce(jnp.minimum, [jnp.min(e) for e in left + right])

    @pl.when(smallest < ATTN_MIN_EXPONENT)
    def _():
        def body(kj, m_run):
            kb = key_block(kj)
            bias = bias_tile(kj)
            shift = bias_shift(kj)
            m_new, alpha = [], []
            for c in range(2):
                s = _dot(kb, qmaps[c]) + bias
                s_ref[c] = s
                m_c = jnp.maximum(m_run[c], jnp.max(s, axis=0, keepdims=True) + shift)
                alpha.append(jnp.exp2(m_run[c] - m_c))
                p_ref[0, c] = jnp.exp2(s_ref[c] - (m_c - shift)).astype(BF16)
                m_new.append(m_c)
            accumulate(kj, 0, alpha)
            return tuple(m_new)

        acc_ref[...] = jnp.zeros_like(acc_ref)
        lax.fori_loop(0, last + 1, body, (e0, e0))
        write_output(0, tq)


def _attn_bias_tiles():
    tq, tk = ATTN_QBLOCK, ATTN_KBLOCK
    jj = np.arange(tk, dtype=np.int64)[:, None]
    ii = np.arange(tq, dtype=np.int64)[None, :]
    slopes = np.asarray(ALIBI_SLOPES, np.float64)[:, None, None]
    tiles = [-slopes * (ii - jj)[None].astype(np.float64)]
    for r in range(tq // tk):
        key = r * tk + jj
        allowed = (key // CHUNK) <= (ii // CHUNK)
        tiles.append(np.where(allowed[None], -slopes * (np.abs(ii - key)[None] + r * tk), NEG_INF))
    return (LOG2E * np.stack(tiles, axis=1)).astype(np.float32)


def _attn_key_position_columns():
    tk = ATTN_KBLOCK
    rest = LOG2E * np.asarray(ALIBI_SLOPES, np.float64)[:, None] * np.arange(tk, dtype=np.float64)[None, :]
    out = np.zeros((N_HEADS, 2, tk, 2 * DH), np.float32)
    for n in range(ATTN_POS_COLS):
        part = rest.astype(ml_dtypes.bfloat16).astype(np.float64)
        out[:, 0, :, DH + n] = part
        out[:, 1, :, n] = part
        rest = rest - part
    return out.astype(ml_dtypes.bfloat16)


def _attn_call(qt, k3, vt, k_sq, lam_params, subln_g, *, batch, seq):
    tq, tk = ATTN_QBLOCK, ATTN_KBLOCK
    nkb = seq // tk
    d_attn = N_HEADS * DV
    bias = jnp.asarray(_attn_bias_tiles())
    kpos = jnp.asarray(_attn_key_position_columns())
    slopes = jnp.asarray([LOG2E * a for a in ALIBI_SLOPES], F32)
    return pl.pallas_call(
        _attn_kernel,
        grid=(batch, N_HEADS, seq // tq),
        in_specs=[
            pl.BlockSpec(memory_space=pltpu.SMEM),
            pl.BlockSpec((1, DV, tq), lambda b, h, q: (b, h, q)),
            pl.BlockSpec((1, seq, DV), lambda b, h, q: (b, 0, h)),
            pl.BlockSpec((1, nkb, DV, tk), lambda b, h, q: (b, 0, h, 0)),
            pl.BlockSpec((1,) + k_sq.shape[1:], lambda b, h, q: (b, 0, 0, 0)),
            pl.BlockSpec((1,) + bias.shape[1:], lambda b, h, q: (h, 0, 0, 0)),
            pl.BlockSpec((1,) + kpos.shape[1:], lambda b, h, q: (h, 0, 0, 0)),
            pl.BlockSpec(lam_params.shape, lambda b, h, q: (0, 0)),
            pl.BlockSpec(subln_g.shape, lambda b, h, q: (0, 0)),
        ],
        out_specs=pl.BlockSpec((1, tq, DV), lambda b, h, q: (b, q, h)),
        out_shape=jax.ShapeDtypeStruct((batch, seq, d_attn), BF16),
        scratch_shapes=[pltpu.VMEM((2, DV + ATTN_ONES_ROWS, tq), F32),
                        pltpu.VMEM((2, tk, tq), F32),
                        pltpu.VMEM((2, 2, tk, tq), BF16)],
        compiler_params=pltpu.CompilerParams(
            dimension_semantics=("arbitrary", "arbitrary", "arbitrary"),
            vmem_limit_bytes=V7X_VMEM_LIMIT),
        name="diff_attn",
    )(slopes, qt, k3, vt, k_sq, bias, kpos, lam_params, subln_g)


def _post_kernel(x_ref, o_ref, ma_ref, sgb_ref, mod_ref, gf_ref, gfin_ref,
                 wob_ref, wo_ref, wg_ref, wu_ref, wd_ref, y_ref):
    mod = mod_ref[0]
    tm = x_ref.shape[0]
    groups = [slice(g * tm // POST_ROW_GROUPS, (g + 1) * tm // POST_ROW_GROUPS)
              for g in range(POST_ROW_GROUPS)]

    def mixer_residual(rows):
        y_b = _dot(o_ref[rows, :], wob_ref[...])
        merged = ma_ref[rows, :].astype(F32) + sgb_ref[rows, :].astype(F32) * y_b
        return x_ref[rows, :] + mod[2:3, :] * _dot(merged.astype(BF16), wo_ref[...])

    def ffn_input(x1):
        xn = x1 * lax.rsqrt(jnp.mean(x1 * x1, axis=-1, keepdims=True) + EPS)
        return ((xn * gf_ref[...]) * (1.0 + mod[4:5, :]) + mod[3:4, :]).astype(BF16)

    def ffn_hidden(h2):
        gate = _dot(h2, wg_ref[...])
        up = _dot(h2, wu_ref[...])
        return ((gate * _sigmoid(gate)) * up).astype(BF16)

    def finish(rows, x1, act):
        x2 = x1 + mod[5:6, :] * _dot(act, wd_ref[...])
        y_ref[rows, :] = (x2 * lax.rsqrt(jnp.mean(x2 * x2, axis=-1, keepdims=True) + EPS)) * gfin_ref[...]

    x1 = [mixer_residual(rows) for rows in groups]
    h2 = [ffn_input(v) for v in x1]
    act = [ffn_hidden(v) for v in h2]
    for rows, a, b in zip(groups, x1, act):
        finish(rows, a, b)


def _post_call(x2, o2, ma, sgb, mod3, g_ffn, g_final, wob, wo, wg, wu, wd, *, seq):
    t, d = x2.shape
    tm = POST_BLOCK_M
    tps = seq // tm
    full = lambda a: pl.BlockSpec(a.shape, lambda i: (0,) * a.ndim)
    row = lambda a: pl.BlockSpec((tm, a.shape[1]), lambda i: (i, 0))
    return pl.pallas_call(
        _post_kernel,
        grid=(t // tm,),
        in_specs=[
            row(x2), row(o2), row(ma), row(sgb),
            pl.BlockSpec((1,) + mod3.shape[1:], lambda i: (i // tps, 0, 0)),
            full(g_ffn), full(g_final), full(wob), full(wo), full(wg), full(wu), full(wd),
        ],
        out_specs=pl.BlockSpec((tm, d), lambda i: (i, 0)),
        out_shape=jax.ShapeDtypeStruct((t, d), F32),
        compiler_params=pltpu.CompilerParams(
            dimension_semantics=("arbitrary",), vmem_limit_bytes=V7X_VMEM_LIMIT),
        name="mixer_out_ffn",
    )(x2, o2, ma, sgb, mod3, g_ffn, g_final, wob, wo, wg, wu, wd)


def kernel(x, c, w_ada, b_ada, g_mix, w_in, conv_w, w_out_a, lambda_q1, lambda_k1, lambda_q2,
           lambda_k2, subln_g, w_out_b, w_out, g_ffn, w_gate, w_up, w_down, g_final):
    batch, seq, d = x.shape
    depth = w_ada.shape[0]
    assert depth == 1 and seq % PRE_BLOCK_M == 0 and seq % POST_BLOCK_M == 0
    assert PRE_BLOCK_M % ATTN_KBLOCK == 0 and ATTN_KBLOCK % CHUNK == 0
    assert ATTN_KBLOCK % (PRE_BLOCK_M // PRE_ROW_GROUPS) == 0 and PRE_BLOCK_M % (8 * PRE_ROW_GROUPS) == 0
    assert POST_BLOCK_M % (8 * POST_ROW_GROUPS) == 0
    assert ATTN_QBLOCK == 2 * ATTN_KBLOCK and seq % ATTN_QBLOCK == 0
    l = 0
    x2 = x.reshape(batch * seq, d)

    c_pad = jnp.pad(c, ((0, 8 - batch), (0, 0)))
    mod = _ada_call(c_pad, w_ada[l], b_ada[l].reshape(1, -1))
    mod3 = mod[:batch].reshape(batch, 6, d)

    ma, sgb, qt, k2, vt, k_sq = _pre_call(
        x2, mod3, g_mix[l].reshape(1, d), w_in[l].astype(BF16), conv_w[l],
        w_out_a[l].astype(BF16), batch=batch, seq=seq)

    lam_params = jnp.stack([lambda_q1[l], lambda_k1[l], lambda_q2[l], lambda_k2[l]], axis=0)
    o = _attn_call(qt, k2.reshape(batch, seq, -1), vt, k_sq.reshape(batch, -1, 8, 128),
                   lam_params, subln_g[l].reshape(1, DV), batch=batch, seq=seq)

    y = _post_call(
        x2, o.reshape(batch * seq, -1), ma, sgb, mod3, g_ffn[l].reshape(1, d),
        g_final.reshape(1, d), w_out_b[l].astype(BF16), w_out[l].astype(BF16),
        w_gate[l].astype(BF16), w_up[l].astype(BF16), w_down[l].astype(BF16), seq=seq)
    return y.reshape(batch, seq, d)
```

```python
import functools
import math

import jax
import jax.numpy as jnp
import ml_dtypes
import numpy as np
from jax import lax
from jax.experimental import pallas as pl
from jax.experimental.pallas import tpu as pltpu

CHUNK = 64
CONV_WIDTH = 3
N_HEADS = 4
DH = 64
DV = 2 * DH
EPS = 1e-6
NEG_INF = -1e30
LAMBDA_INIT = 0.8 - 0.6 * math.exp(-0.3 * 1)
LOG2E = math.log2(math.e)
ALIBI_SLOPES = tuple(2.0 ** (-8.0 * (i + 1) / N_HEADS) for i in range(N_HEADS))

ADA_BLOCK_N = 1536
PRE_BLOCK_M = 1024
PRE_ROW_GROUPS = 4
POST_BLOCK_M = 512
POST_ROW_GROUPS = 2
ATTN_KBLOCK = 512
ATTN_QBLOCK = 1024
ATTN_ONES_ROWS = 16
ATTN_POS_COLS = 3
ATTN_BOUND_SLACK = 1.01
ATTN_BOUND_MARGIN = 0.01
ATTN_MIN_EXPONENT = -100.0
V7X_VMEM_LIMIT = 56 * 1024 * 1024

BF16 = jnp.bfloat16
F32 = jnp.float32


def _dot(a, b):
    return jnp.dot(a, b, preferred_element_type=F32)


def _sigmoid(x):
    return 1.0 / (1.0 + jnp.exp(-x))


def _ada_kernel(c_ref, w_ref, b_ref, o_ref):
    c = c_ref[...]
    sc = (c * _sigmoid(c)).astype(BF16)
    o_ref[...] = _dot(sc, w_ref[...].astype(BF16)) + b_ref[...]


def _ada_call(c_pad, w_ada, b_ada):
    rows, d = c_pad.shape
    n = w_ada.shape[1]
    return pl.pallas_call(
        _ada_kernel,
        grid=(n // ADA_BLOCK_N,),
        in_specs=[
            pl.BlockSpec((rows, d), lambda j: (0, 0)),
            pl.BlockSpec((d, ADA_BLOCK_N), lambda j: (0, j)),
            pl.BlockSpec((1, ADA_BLOCK_N), lambda j: (0, j)),
        ],
        out_specs=pl.BlockSpec((rows, ADA_BLOCK_N), lambda j: (0, j)),
        out_shape=jax.ShapeDtypeStruct((rows, n), F32),
        compiler_params=pltpu.CompilerParams(
            dimension_semantics=("arbitrary",), vmem_limit_bytes=V7X_VMEM_LIMIT),
        name="ada_mod",
    )(c_pad, w_ada, b_ada)


def _pre_kernel(x_ref, mod_ref, g_ref, win_ref, cw_ref, woa_ref,
                ma_ref, sgb_ref, qt_ref, k_ref, vt_ref, ksq_ref, carry_ref,
                *, tiles_per_seq, d_conv, d_attn, d_model):
    i = pl.program_id(0)
    tm = x_ref.shape[0]
    mod = mod_ref[0]

    @pl.when(i % tiles_per_seq == 0)
    def _():
        carry_ref[...] = jnp.zeros_like(carry_ref)

    rg = tm // PRE_ROW_GROUPS
    groups = [slice(g * rg, (g + 1) * rg) for g in range(PRE_ROW_GROUPS)]

    a0 = 3 * d_conv
    g0 = a0 + 3 * d_attn

    def normed(rows):
        x = x_ref[rows, :]
        xn = x * lax.rsqrt(jnp.mean(x * x, axis=-1, keepdims=True) + EPS)
        return ((xn * g_ref[...]) * (1.0 + mod[1:2, :]) + mod[0:1, :]).astype(BF16)

    hb = [normed(rows) for rows in groups]

    pc = [_dot(h, win_ref[:, 0:a0]) for h in hb]
    cw = cw_ref[...]
    before = carry_ref[...]
    y_a = []
    for p in pc:
        u = p[:, 0:d_conv]
        gb = p[:, d_conv:2 * d_conv]
        gc = p[:, 2 * d_conv:3 * d_conv]
        cu = gc * u
        ext = jnp.concatenate([before, cu], axis=0)
        prev1 = pltpu.roll(ext, 1, axis=0)[8:, :]
        prev2 = pltpu.roll(ext, 2, axis=0)[8:, :]
        before = cu[rg - 8:, :]
        z = cw[0:1, :] * prev2 + cw[1:2, :] * prev1 + cw[2:3, :] * cu
        y_a.append(_dot((gb * z).astype(BF16), woa_ref[...]))
    carry_ref[...] = before

    for rows, h, ya in zip(groups, hb, y_a):
        pg = _dot(h, win_ref[:, g0:g0 + 2 * d_model])
        ma_ref[rows, :] = (_sigmoid(pg[:, 0:d_model]) * ya).astype(BF16)
        sgb_ref[rows, :] = _sigmoid(pg[:, d_model:2 * d_model]).astype(BF16)

    lane = lax.broadcasted_iota(jnp.int32, ksq_ref.shape[1:], 1)
    tile = jnp.zeros(ksq_ref.shape[1:], F32)
    for g, (rows, h) in enumerate(zip(groups, hb)):
        pa = _dot(h, win_ref[:, a0:g0])
        q = pa[:, 0:d_attn] * (LOG2E / math.sqrt(DH))
        blk, off = divmod(g * rg, ATTN_QBLOCK)
        qt_ref[0, blk, :, off:off + rg] = q.T.astype(BF16)
        kb = pa[:, d_attn:2 * d_attn].astype(BF16)
        k_ref[rows, :] = kb
        kf = kb.astype(F32)
        sq = kf * kf
        for hd in range(N_HEADS):
            norm_sq = jnp.sum(sq[:, hd * DV:(hd + 1) * DV], axis=-1, keepdims=True)
            tile = jnp.where(lane == hd,
                             jnp.maximum(tile, jnp.max(norm_sq, axis=0, keepdims=True)), tile)
        v = pa[:, 2 * d_attn:3 * d_attn]
        blk, off = divmod(g * rg, ATTN_KBLOCK)
        vt_ref[0, blk, :, off:off + rg] = v.T.astype(BF16)
    ksq_ref[0] = tile


def _pre_call(x2, mod3, g_mix, w_in_b, conv_w, w_out_a_b, *, batch, seq):
    t, d = x2.shape
    d_conv = conv_w.shape[1]
    d_attn = N_HEADS * DV
    tm = PRE_BLOCK_M
    tps = seq // tm
    nkb = seq // ATTN_KBLOCK
    kern = functools.partial(_pre_kernel, tiles_per_seq=tps, d_conv=d_conv,
                             d_attn=d_attn, d_model=d)
    full = lambda a: pl.BlockSpec(a.shape, lambda i: (0,) * a.ndim)
    return pl.pallas_call(
        kern,
        grid=(t // tm,),
        in_specs=[
            pl.BlockSpec((tm, d), lambda i: (i, 0)),
            pl.BlockSpec((1,) + mod3.shape[1:], lambda i: (i // tps, 0, 0)),
            full(g_mix), full(w_in_b), full(conv_w), full(w_out_a_b),
        ],
        out_specs=[
            pl.BlockSpec((tm, d), lambda i: (i, 0)),
            pl.BlockSpec((tm, d), lambda i: (i, 0)),
            pl.BlockSpec((1, tm // ATTN_QBLOCK, d_attn, ATTN_QBLOCK),
                         lambda i: (i // tps, i % tps, 0, 0)),
            pl.BlockSpec((tm, d_attn), lambda i: (i, 0)),
            pl.BlockSpec((1, tm // ATTN_KBLOCK, d_attn, ATTN_KBLOCK),
                         lambda i: (i // tps, i % tps, 0, 0)),
            pl.BlockSpec((1, 8, 128), lambda i: (i, 0, 0)),
        ],
        out_shape=[
            jax.ShapeDtypeStruct((t, d), BF16),
            jax.ShapeDtypeStruct((t, d), BF16),
            jax.ShapeDtypeStruct((batch, seq // ATTN_QBLOCK, d_attn, ATTN_QBLOCK), BF16),
            jax.ShapeDtypeStruct((t, d_attn), BF16),
            jax.ShapeDtypeStruct((batch, nkb, d_attn, ATTN_KBLOCK), BF16),
            jax.ShapeDtypeStruct((t // tm, 8, 128), F32),
        ],
        scratch_shapes=[pltpu.VMEM((8, d_conv), F32)],
        compiler_params=pltpu.CompilerParams(
            dimension_semantics=("arbitrary",), vmem_limit_bytes=V7X_VMEM_LIMIT),
        name="mixer_in",
    )(x2, mod3, g_mix, w_in_b, conv_w, w_out_a_b)


def _attn_kernel(slope_ref, qt_ref, *refs):
    h = pl.program_id(1)

    def query_block(qi, carry):
        _attn_query_block(qi, h, slope_ref, qt_ref, *refs)
        return carry

    lax.fori_loop(0, qt_ref.shape[1], query_block, 0)


def _attn_query_block(qi, h, slope_ref, qt_ref, k_ref, vt_ref, ksq_ref, bias_ref, kpos_ref, lam_ref,
                      g_ref, o_ref, acc_ref, s_ref, p_ref):
    tq, tk = ATTN_QBLOCK, ATTN_KBLOCK
    ratio = tq // tk
    slope = slope_ref[h]

    qt = qt_ref[0, qi]
    row = lax.broadcasted_iota(jnp.int32, qt.shape, 0)
    zero = jnp.zeros_like(qt)
    qmaps = (jnp.where(row < DH, qt, zero), jnp.where(row >= DH, qt, zero))
    ones = jnp.ones((ATTN_ONES_ROWS, tk), BF16)

    last = ratio * qi + ratio - 1

    def key_block(kj):
        return k_ref[0, pl.ds(pl.multiple_of(kj * tk, tk), tk), :]

    def bias_tile(kj):
        return bias_ref[0, jnp.clip(kj - ratio * qi + 1, 0, ratio)]

    def bias_shift(kj):
        return -slope * (qi * tq - kj * tk).astype(F32)

    def accumulate(kj, slot, alphas=None):
        vb = jnp.concatenate([vt_ref[0, kj], ones], axis=0)
        for c in range(2):
            old = acc_ref[c] if alphas is None else alphas[c] * acc_ref[c]
            acc_ref[c] = old + _dot(vb, p_ref[slot, c])

    k_sq_all = jnp.max(ksq_ref[0], axis=0)
    lane = lax.broadcasted_iota(jnp.int32, k_sq_all.shape, 1)
    k_sq = jnp.max(jnp.where(lane == h, k_sq_all, 0.0), axis=-1, keepdims=True)[0:1]
    stab = []
    for c in range(2):
        qf = qmaps[c].astype(F32)
        q_sq = jnp.sum(qf * qf, axis=0, keepdims=True)
        stab.append(jnp.sqrt(q_sq * k_sq) * ATTN_BOUND_SLACK + ATTN_BOUND_MARGIN)

    def store_exponentials(e, slot, c, lo, hi, e_max):
        p_ref[slot, c, :, lo:hi] = jnp.exp2(e).astype(BF16)
        return jnp.maximum(e_max, jnp.max(e, axis=0, keepdims=True))

    def exponentials_diagonal(kj, tile, slot, lo, hi, e_max):
        kb = key_block(kj)
        bias = bias_ref[0, tile, :, lo:hi]
        shift = bias_shift(kj)
        return tuple(store_exponentials(
            _dot(kb, qmaps[c][:, lo:hi]) + bias + (shift - stab[c][:, lo:hi]), slot, c, lo, hi,
            e_max[c]) for c in range(2))

    lane_k = lax.broadcasted_iota(jnp.int32, (tk, 2 * DH), 1)
    ones_rows = (((row >= DH) & (row < DH + ATTN_POS_COLS)).astype(F32).astype(BF16),
                 (row < ATTN_POS_COLS).astype(F32).astype(BF16))
    q_aug = tuple(qmaps[c] + ones_rows[c] for c in range(2))
    pos = lax.broadcasted_iota(jnp.int32, (1, tq), 1).astype(F32)
    base = tuple(stab[c] + slope * pos for c in range(2))

    def exponentials_below(kj, slot, lo, hi, e_max):
        kb = key_block(kj)
        lhs = (jnp.where(lane_k < DH, kb, kpos_ref[0, 0]), jnp.where(lane_k >= DH, kb, kpos_ref[0, 1]))
        shift = bias_shift(kj)
        return tuple(store_exponentials(
            _dot(lhs[c], q_aug[c][:, lo:hi]) - (base[c][:, lo:hi] - shift), slot, c, lo, hi,
            e_max[c]) for c in range(2))

    def accumulate_columns(kj, slot, lo, hi):
        vb = jnp.concatenate([vt_ref[0, kj], ones], axis=0)
        for c in range(2):
            acc_ref[c, :, lo:hi] = acc_ref[c, :, lo:hi] + _dot(vb, p_ref[slot, c, :, lo:hi])

    lp = lam_ref[...]
    lam = (jnp.exp(jnp.sum(lp[0:1, :] * lp[1:2, :], axis=-1, keepdims=True))
           - jnp.exp(jnp.sum(lp[2:3, :] * lp[3:4, :], axis=-1, keepdims=True))
           + LAMBDA_INIT)

    def write_output(lo, hi):
        a1 = acc_ref[0, :, lo:hi]
        a2 = acc_ref[1, :, lo:hi]
        ot = (a1[0:DV, :] * (1.0 / a1[DV:DV + 1, :])
              - (lam * a2[0:DV, :]) * (1.0 / a2[DV:DV + 1, :]))
        o = ot.T
        o = o * lax.rsqrt(jnp.mean(o * o, axis=-1, keepdims=True) + EPS)
        rows = pl.ds(pl.multiple_of(qi * tq + lo, hi - lo), hi - lo)
        o_ref[0, rows, :] = ((o * g_ref[...]) * (1.0 - LAMBDA_INIT)).astype(BF16)

    acc_ref[...] = jnp.zeros_like(acc_ref)
    e0 = jnp.full((1, tq), NEG_INF, F32)
    half = tq // 2

    def diagonal_region(e_max, pending_pv=()):
        d0 = 2 * qi
        left = tuple(e[:, 0:half] for e in e_max)
        right = tuple(e[:, half:tq] for e in e_max)
        for stage in pending_pv[0:1]:
            stage()
        left = exponentials_diagonal(d0, 1, 0, 0, half, left)
        right = exponentials_below(d0, 0, half, tq, right)
        for stage in pending_pv[1:2]:
            stage()
        right = exponentials_diagonal(d0 + 1, 2, 1, half, tq, right)
        accumulate_columns(d0, 0, 0, half)
        write_output(0, half)
        accumulate_columns(d0, 0, half, tq)
        accumulate_columns(d0 + 1, 1, half, tq)
        write_output(half, tq)
        return left, right

    def first_query_block():
        return diagonal_region((e0, e0))

    def later_query_block():
        def trip(t, e_max):
            e_max = exponentials_below(2 * t + 1, 1, 0, tq, e_max)
            accumulate(2 * t, 0)
            e_max = exponentials_below(2 * t + 2, 0, 0, tq, e_max)
            accumulate(2 * t + 1, 1)
            return e_max

        e_max = lax.fori_loop(0, qi - 1, trip, exponentials_below(0, 0, 0, tq, (e0, e0)))
        d0 = 2 * qi
        e_max = exponentials_below(d0 - 1, 1, 0, tq, e_max)
        return diagonal_region(e_max, (lambda: accumulate(d0 - 2, 0), lambda: accumulate(d0 - 1, 1)))

    left, right = lax.cond(qi == 0, first_query_block, later_query_block)
    smallest = functools.reduce(jnp.minimum, [jnp.min(e) for e in left + right])

    @pl.when(smallest < ATTN_MIN_EXPONENT)
    def _():
        def body(kj, m_run):
            kb = key_block(kj)
            bias = bias_tile(kj)
            shift = bias_shift(kj)
            m_new, alpha = [], []
            for c in range(2):
                s = _dot(kb, qmaps[c]) + bias
                s_ref[c] = s
                m_c = jnp.maximum(m_run[c], jnp.max(s, axis=0, keepdims=True) + shift)
                alpha.append(jnp.exp2(m_run[c] - m_c))
                p_ref[0, c] = jnp.exp2(s_ref[c] - (m_c - shift)).astype(BF16)
                m_new.append(m_c)
            accumulate(kj, 0, alpha)
            return tuple(m_new)

        acc_ref[...] = jnp.zeros_like(acc_ref)
        lax.fori_loop(0, last + 1, body, (e0, e0))
        write_output(0, tq)


def _attn_bias_tiles():
    tq, tk = ATTN_QBLOCK, ATTN_KBLOCK
    jj = np.arange(tk, dtype=np.int64)[:, None]
    ii = np.arange(tq, dtype=np.int64)[None, :]
    slopes = np.asarray(ALIBI_SLOPES, np.float64)[:, None, None]
    tiles = [-slopes * (ii - jj)[None].astype(np.float64)]
    for r in range(tq // tk):
        key = r * tk + jj
        allowed = (key // CHUNK) <= (ii // CHUNK)
        tiles.append(np.where(allowed[None], -slopes * (np.abs(ii - key)[None] + r * tk), NEG_INF))
    return (LOG2E * np.stack(tiles, axis=1)).astype(np.float32)


def _attn_key_position_columns():
    tk = ATTN_KBLOCK
    rest = LOG2E * np.asarray(ALIBI_SLOPES, np.float64)[:, None] * np.arange(tk, dtype=np.float64)[None, :]
    out = np.zeros((N_HEADS, 2, tk, 2 * DH), np.float32)
    for n in range(ATTN_POS_COLS):
        part = rest.astype(ml_dtypes.bfloat16).astype(np.float64)
        out[:, 0, :, DH + n] = part
        out[:, 1, :, n] = part
        rest = rest - part
    return out.astype(ml_dtypes.bfloat16)


def _attn_call(qt, k3, vt, k_sq, lam_params, subln_g, *, batch, seq):
    tq, tk = ATTN_QBLOCK, ATTN_KBLOCK
    nkb = seq // tk
    d_attn = N_HEADS * DV
    bias = jnp.asarray(_attn_bias_tiles())
    kpos = jnp.asarray(_attn_key_position_columns())
    slopes = jnp.asarray([LOG2E * a for a in ALIBI_SLOPES], F32)
    return pl.pallas_call(
        _attn_kernel,
        grid=(batch, N_HEADS),
        in_specs=[
            pl.BlockSpec(memory_space=pltpu.SMEM),
            pl.BlockSpec((1, seq // tq, DV, tq), lambda b, h: (b, 0, h, 0)),
            pl.BlockSpec((1, seq, DV), lambda b, h: (b, 0, h)),
            pl.BlockSpec((1, nkb, DV, tk), lambda b, h: (b, 0, h, 0)),
            pl.BlockSpec((1,) + k_sq.shape[1:], lambda b, h: (b, 0, 0, 0)),
            pl.BlockSpec((1,) + bias.shape[1:], lambda b, h: (h, 0, 0, 0)),
            pl.BlockSpec((1,) + kpos.shape[1:], lambda b, h: (h, 0, 0, 0)),
            pl.BlockSpec(lam_params.shape, lambda b, h: (0, 0)),
            pl.BlockSpec(subln_g.shape, lambda b, h: (0, 0)),
        ],
        out_specs=pl.BlockSpec((1, seq, DV), lambda b, h: (b, 0, h)),
        out_shape=jax.ShapeDtypeStruct((batch, seq, d_attn), BF16),
        scratch_shapes=[pltpu.VMEM((2, DV + ATTN_ONES_ROWS, tq), F32),
                        pltpu.VMEM((2, tk, tq), F32),
                        pltpu.VMEM((2, 2, tk, tq), BF16)],
        compiler_params=pltpu.CompilerParams(
            dimension_semantics=("arbitrary", "arbitrary"),
            vmem_limit_bytes=V7X_VMEM_LIMIT),
        name="diff_attn",
    )(slopes, qt, k3, vt, k_sq, bias, kpos, lam_params, subln_g)


def _post_kernel(x_ref, o_ref, ma_ref, sgb_ref, mod_ref, gf_ref, gfin_ref,
                 wob_ref, wo_ref, wg_ref, wu_ref, wd_ref, y_ref):
    mod = mod_ref[0]
    tm = x_ref.shape[0]
    groups = [slice(g * tm // POST_ROW_GROUPS, (g + 1) * tm // POST_ROW_GROUPS)
              for g in range(POST_ROW_GROUPS)]

    def mixer_residual(rows):
        y_b = _dot(o_ref[rows, :], wob_ref[...])
        merged = ma_ref[rows, :].astype(F32) + sgb_ref[rows, :].astype(F32) * y_b
        return x_ref[rows, :] + mod[2:3, :] * _dot(merged.astype(BF16), wo_ref[...])

    def ffn_input(x1):
        xn = x1 * lax.rsqrt(jnp.mean(x1 * x1, axis=-1, keepdims=True) + EPS)
        return ((xn * gf_ref[...]) * (1.0 + mod[4:5, :]) + mod[3:4, :]).astype(BF16)

    def ffn_hidden(h2):
        gate = _dot(h2, wg_ref[...])
        up = _dot(h2, wu_ref[...])
        return ((gate * _sigmoid(gate)) * up).astype(BF16)

    def finish(rows, x1, act):
        x2 = x1 + mod[5:6, :] * _dot(act, wd_ref[...])
        y_ref[rows, :] = (x2 * lax.rsqrt(jnp.mean(x2 * x2, axis=-1, keepdims=True) + EPS)) * gfin_ref[...]

    x1 = [mixer_residual(rows) for rows in groups]
    h2 = [ffn_input(v) for v in x1]
    act = [ffn_hidden(v) for v in h2]
    for rows, a, b in zip(groups, x1, act):
        finish(rows, a, b)


def _post_call(x2, o2, ma, sgb, mod3, g_ffn, g_final, wob, wo, wg, wu, wd, *, seq):
    t, d = x2.shape
    tm = POST_BLOCK_M
    tps = seq // tm
    full = lambda a: pl.BlockSpec(a.shape, lambda i: (0,) * a.ndim)
    row = lambda a: pl.BlockSpec((tm, a.shape[1]), lambda i: (i, 0))
    return pl.pallas_call(
        _post_kernel,
        grid=(t // tm,),
        in_specs=[
            row(x2), row(o2), row(ma), row(sgb),
            pl.BlockSpec((1,) + mod3.shape[1:], lambda i: (i // tps, 0, 0)),
            full(g_ffn), full(g_final), full(wob), full(wo), full(wg), full(wu), full(wd),
        ],
        out_specs=pl.BlockSpec((tm, d), lambda i: (i, 0)),
        out_shape=jax.ShapeDtypeStruct((t, d), F32),
        compiler_params=pltpu.CompilerParams(
            dimension_semantics=("arbitrary",), vmem_limit_bytes=V7X_VMEM_LIMIT),
        name="mixer_out_ffn",
    )(x2, o2, ma, sgb, mod3, g_ffn, g_final, wob, wo, wg, wu, wd)


def kernel(x, c, w_ada, b_ada, g_mix, w_in, conv_w, w_out_a, lambda_q1, lambda_k1, lambda_q2,
           lambda_k2, subln_g, w_out_b, w_out, g_ffn, w_gate, w_up, w_down, g_final):
    batch, seq, d = x.shape
    depth = w_ada.shape[0]
    assert depth == 1 and seq % PRE_BLOCK_M == 0 and seq % POST_BLOCK_M == 0
    assert PRE_BLOCK_M % ATTN_KBLOCK == 0 and ATTN_KBLOCK % CHUNK == 0
    assert ATTN_KBLOCK % (PRE_BLOCK_M // PRE_ROW_GROUPS) == 0 and PRE_BLOCK_M % (8 * PRE_ROW_GROUPS) == 0
    assert POST_BLOCK_M % (8 * POST_ROW_GROUPS) == 0
    assert ATTN_QBLOCK == 2 * ATTN_KBLOCK and seq % ATTN_QBLOCK == 0
    assert PRE_BLOCK_M % ATTN_QBLOCK == 0 and ATTN_QBLOCK % (PRE_BLOCK_M // PRE_ROW_GROUPS) == 0
    l = 0
    x2 = x.reshape(batch * seq, d)

    c_pad = jnp.pad(c, ((0, 8 - batch), (0, 0)))
    mod = _ada_call(c_pad, w_ada[l], b_ada[l].reshape(1, -1))
    mod3 = mod[:batch].reshape(batch, 6, d)

    ma, sgb, qt, k2, vt, k_sq = _pre_call(
        x2, mod3, g_mix[l].reshape(1, d), w_in[l].astype(BF16), conv_w[l],
        w_out_a[l].astype(BF16), batch=batch, seq=seq)

    lam_params = jnp.stack([lambda_q1[l], lambda_k1[l], lambda_q2[l], lambda_k2[l]], axis=0)
    o = _attn_call(qt, k2.reshape(batch, seq, -1), vt, k_sq.reshape(batch, -1, 8, 128),
                   lam_params, subln_g[l].reshape(1, DV), batch=batch, seq=seq)

    y = _post_call(
        x2, o.reshape(batch * seq, -1), ma, sgb, mod3, g_ffn[l].reshape(1, d),
        g_final.reshape(1, d), w_out_b[l].astype(BF16), w_out[l].astype(BF16),
        w_gate[l].astype(BF16), w_up[l].astype(BF16), w_down[l].astype(BF16), seq=seq)
    return y.reshape(batch, seq, d)
```

```python
import functools
import math

import jax
import jax.numpy as jnp
import ml_dtypes
import numpy as np
from jax import lax
from jax.experimental import pallas as pl
from jax.experimental.pallas import tpu as pltpu

CHUNK = 64
CONV_WIDTH = 3
N_HEADS = 4
DH = 64
DV = 2 * DH
EPS = 1e-6
NEG_INF = -1e30
LAMBDA_INIT = 0.8 - 0.6 * math.exp(-0.3 * 1)
LOG2E = math.log2(math.e)
ALIBI_SLOPES = tuple(2.0 ** (-8.0 * (i + 1) / N_HEADS) for i in range(N_HEADS))

ADA_BLOCK_N = 1536
PRE_BLOCK_M = 1024
PRE_ROW_GROUPS = 4
POST_BLOCK_M = 512
POST_ROW_GROUPS = 2
ATTN_KBLOCK = 512
ATTN_QBLOCK = 1024
ATTN_ONES_ROWS = 16
ATTN_POS_COLS = 3
ATTN_BOUND_SLACK = 1.01
ATTN_BOUND_MARGIN = 0.01
ATTN_MIN_EXPONENT = -100.0
V7X_VMEM_LIMIT = 56 * 1024 * 1024

BF16 = jnp.bfloat16
F32 = jnp.float32


def _dot(a, b):
    return jnp.dot(a, b, preferred_element_type=F32)


def _sigmoid(x):
    return 1.0 / (1.0 + jnp.exp(-x))


def _ada_kernel(c_ref, w_ref, b_ref, o_ref):
    c = c_ref[...]
    sc = (c * _sigmoid(c)).astype(BF16)
    o_ref[...] = _dot(sc, w_ref[...].astype(BF16)) + b_ref[...]


def _ada_call(c_pad, w_ada, b_ada):
    rows, d = c_pad.shape
    n = w_ada.shape[1]
    return pl.pallas_call(
        _ada_kernel,
        grid=(n // ADA_BLOCK_N,),
        in_specs=[
            pl.BlockSpec((rows, d), lambda j: (0, 0)),
            pl.BlockSpec((d, ADA_BLOCK_N), lambda j: (0, j)),
            pl.BlockSpec((1, ADA_BLOCK_N), lambda j: (0, j)),
        ],
        out_specs=pl.BlockSpec((rows, ADA_BLOCK_N), lambda j: (0, j)),
        out_shape=jax.ShapeDtypeStruct((rows, n), F32),
        compiler_params=pltpu.CompilerParams(
            dimension_semantics=("arbitrary",), vmem_limit_bytes=V7X_VMEM_LIMIT),
        name="ada_mod",
    )(c_pad, w_ada, b_ada)


def _pre_kernel(x_ref, mod_ref, g_ref, win_ref, cw_ref, woa_ref,
                ma_ref, sgb_ref, qt_ref, k_ref, vt_ref, ksq_ref, carry_ref,
                *, tiles_per_seq, d_conv, d_attn, d_model):
    i = pl.program_id(0)
    tm = x_ref.shape[0]
    mod = mod_ref[0]

    @pl.when(i % tiles_per_seq == 0)
    def _():
        carry_ref[...] = jnp.zeros_like(carry_ref)

    rg = tm // PRE_ROW_GROUPS
    groups = [slice(g * rg, (g + 1) * rg) for g in range(PRE_ROW_GROUPS)]

    a0 = 3 * d_conv
    g0 = a0 + 3 * d_attn

    def normed(rows):
        x = x_ref[rows, :]
        xn = x * lax.rsqrt(jnp.mean(x * x, axis=-1, keepdims=True) + EPS)
        return ((xn * g_ref[...]) * (1.0 + mod[1:2, :]) + mod[0:1, :]).astype(BF16)

    hb = [normed(rows) for rows in groups]

    pc = [_dot(h, win_ref[:, 0:a0]) for h in hb]
    cw = cw_ref[...]
    before = carry_ref[...]
    y_a = []
    for p in pc:
        u = p[:, 0:d_conv]
        gb = p[:, d_conv:2 * d_conv]
        gc = p[:, 2 * d_conv:3 * d_conv]
        cu = gc * u
        ext = jnp.concatenate([before, cu], axis=0)
        prev1 = pltpu.roll(ext, 1, axis=0)[8:, :]
        prev2 = pltpu.roll(ext, 2, axis=0)[8:, :]
        before = cu[rg - 8:, :]
        z = cw[0:1, :] * prev2 + cw[1:2, :] * prev1 + cw[2:3, :] * cu
        y_a.append(_dot((gb * z).astype(BF16), woa_ref[...]))
    carry_ref[...] = before

    for rows, h, ya in zip(groups, hb, y_a):
        pg = _dot(h, win_ref[:, g0:g0 + 2 * d_model])
        ma_ref[rows, :] = (_sigmoid(pg[:, 0:d_model]) * ya).astype(BF16)
        sgb_ref[rows, :] = _sigmoid(pg[:, d_model:2 * d_model]).astype(BF16)

    lane = lax.broadcasted_iota(jnp.int32, ksq_ref.shape[1:], 1)
    tile = jnp.zeros(ksq_ref.shape[1:], F32)
    for g, (rows, h) in enumerate(zip(groups, hb)):
        pa = _dot(h, win_ref[:, a0:g0])
        q = pa[:, 0:d_attn] * (LOG2E / math.sqrt(DH))
        blk, off = divmod(g * rg, ATTN_QBLOCK)
        qt_ref[0, blk, :, off:off + rg] = q.T.astype(BF16)
        kb = pa[:, d_attn:2 * d_attn].astype(BF16)
        k_ref[rows, :] = kb
        kf = kb.astype(F32)
        sq = kf * kf
        for hd in range(N_HEADS):
            norm_sq = jnp.sum(sq[:, hd * DV:(hd + 1) * DV], axis=-1, keepdims=True)
            tile = jnp.where(lane == hd,
                             jnp.maximum(tile, jnp.max(norm_sq, axis=0, keepdims=True)), tile)
        v = pa[:, 2 * d_attn:3 * d_attn]
        blk, off = divmod(g * rg, ATTN_KBLOCK)
        vt_ref[0, blk, :, off:off + rg] = v.T.astype(BF16)
    ksq_ref[0] = tile


def _pre_call(x2, mod3, g_mix, w_in_b, conv_w, w_out_a_b, *, batch, seq):
    t, d = x2.shape
    d_conv = conv_w.shape[1]
    d_attn = N_HEADS * DV
    tm = PRE_BLOCK_M
    tps = seq // tm
    nkb = seq // ATTN_KBLOCK
    kern = functools.partial(_pre_kernel, tiles_per_seq=tps, d_conv=d_conv,
                             d_attn=d_attn, d_model=d)
    full = lambda a: pl.BlockSpec(a.shape, lambda i: (0,) * a.ndim)
    return pl.pallas_call(
        kern,
        grid=(t // tm,),
        in_specs=[
            pl.BlockSpec((tm, d), lambda i: (i, 0)),
            pl.BlockSpec((1,) + mod3.shape[1:], lambda i: (i // tps, 0, 0)),
            full(g_mix), full(w_in_b), full(conv_w), full(w_out_a_b),
        ],
        out_specs=[
            pl.BlockSpec((tm, d), lambda i: (i, 0)),
            pl.BlockSpec((tm, d), lambda i: (i, 0)),
            pl.BlockSpec((1, tm // ATTN_QBLOCK, d_attn, ATTN_QBLOCK),
                         lambda i: (i // tps, i % tps, 0, 0)),
            pl.BlockSpec((tm, d_attn), lambda i: (i, 0)),
            pl.BlockSpec((1, tm // ATTN_KBLOCK, d_attn, ATTN_KBLOCK),
                         lambda i: (i // tps, i % tps, 0, 0)),
            pl.BlockSpec((1, 8, 128), lambda i: (i, 0, 0)),
        ],
        out_shape=[
            jax.ShapeDtypeStruct((t, d), BF16),
            jax.ShapeDtypeStruct((t, d), BF16),
            jax.ShapeDtypeStruct((batch, seq // ATTN_QBLOCK, d_attn, ATTN_QBLOCK), BF16),
            jax.ShapeDtypeStruct((t, d_attn), BF16),
            jax.ShapeDtypeStruct((batch, nkb, d_attn, ATTN_KBLOCK), BF16),
            jax.ShapeDtypeStruct((t // tm, 8, 128), F32),
        ],
        scratch_shapes=[pltpu.VMEM((8, d_conv), F32)],
        compiler_params=pltpu.CompilerParams(
            dimension_semantics=("arbitrary",), vmem_limit_bytes=V7X_VMEM_LIMIT),
        name="mixer_in",
    )(x2, mod3, g_mix, w_in_b, conv_w, w_out_a_b)


def _attn_kernel(slope_ref, qt_ref, *refs):
    h = pl.program_id(1)

    def query_block(qi, carry):
        _attn_query_block(qi, h, slope_ref, qt_ref, *refs)
        return carry

    lax.fori_loop(0, qt_ref.shape[1], query_block, 0)


def _attn_query_block(qi, h, slope_ref, qt_ref, k_ref, vt_ref, ksq_ref, bias_ref, kpos_ref, lam_ref,
                      g_ref, o_ref, acc_ref, s_ref, p_ref):
    tq, tk = ATTN_QBLOCK, ATTN_KBLOCK
    ratio = tq // tk
    slope = slope_ref[h]

    qt = qt_ref[0, qi]
    row = lax.broadcasted_iota(jnp.int32, qt.shape, 0)
    zero = jnp.zeros_like(qt)
    qmaps = (jnp.where(row < DH, qt, zero), jnp.where(row >= DH, qt, zero))
    ones = jnp.ones((ATTN_ONES_ROWS, tk), BF16)

    last = ratio * qi + ratio - 1

    def key_block(kj):
        return k_ref[0, pl.ds(pl.multiple_of(kj * tk, tk), tk), :]

    def bias_tile(kj):
        return bias_ref[0, jnp.clip(kj - ratio * qi + 1, 0, ratio)]

    def bias_shift(kj):
        return -slope * jnp.asarray(qi * tq - kj * tk, F32)

    def accumulate(kj, slot, alphas=None):
        vb = jnp.concatenate([vt_ref[0, kj], ones], axis=0)
        for c in range(2):
            old = acc_ref[c] if alphas is None else alphas[c] * acc_ref[c]
            acc_ref[c] = old + _dot(vb, p_ref[slot, c])

    k_sq_all = jnp.max(ksq_ref[0], axis=0)
    lane = lax.broadcasted_iota(jnp.int32, k_sq_all.shape, 1)
    k_sq = jnp.max(jnp.where(lane == h, k_sq_all, 0.0), axis=-1, keepdims=True)[0:1]
    stab = []
    for c in range(2):
        qf = qmaps[c].astype(F32)
        q_sq = jnp.sum(qf * qf, axis=0, keepdims=True)
        stab.append(jnp.sqrt(q_sq * k_sq) * ATTN_BOUND_SLACK + ATTN_BOUND_MARGIN)

    def store_exponentials(e, slot, c, lo, hi, e_max):
        p_ref[slot, c, :, lo:hi] = jnp.exp2(e).astype(BF16)
        return jnp.maximum(e_max, jnp.max(e, axis=0, keepdims=True))

    def exponentials_diagonal(kj, tile, slot, lo, hi, e_max):
        kb = key_block(kj)
        bias = bias_ref[0, tile, :, lo:hi]
        shift = bias_shift(kj)
        return tuple(store_exponentials(
            _dot(kb, qmaps[c][:, lo:hi]) + bias + (shift - stab[c][:, lo:hi]), slot, c, lo, hi,
            e_max[c]) for c in range(2))

    lane_k = lax.broadcasted_iota(jnp.int32, (tk, 2 * DH), 1)
    ones_rows = (((row >= DH) & (row < DH + ATTN_POS_COLS)).astype(F32).astype(BF16),
                 (row < ATTN_POS_COLS).astype(F32).astype(BF16))
    q_aug = tuple(qmaps[c] + ones_rows[c] for c in range(2))
    pos = lax.broadcasted_iota(jnp.int32, (1, tq), 1).astype(F32)
    base = tuple(stab[c] + slope * pos for c in range(2))

    def exponentials_below(kj, slot, lo, hi, e_max):
        kb = key_block(kj)
        lhs = (jnp.where(lane_k < DH, kb, kpos_ref[0, 0]), jnp.where(lane_k >= DH, kb, kpos_ref[0, 1]))
        shift = bias_shift(kj)
        return tuple(store_exponentials(
            _dot(lhs[c], q_aug[c][:, lo:hi]) - (base[c][:, lo:hi] - shift), slot, c, lo, hi,
            e_max[c]) for c in range(2))

    def accumulate_columns(kj, slot, lo, hi):
        vb = jnp.concatenate([vt_ref[0, kj], ones], axis=0)
        for c in range(2):
            acc_ref[c, :, lo:hi] = acc_ref[c, :, lo:hi] + _dot(vb, p_ref[slot, c, :, lo:hi])

    lp = lam_ref[...]
    lam = (jnp.exp(jnp.sum(lp[0:1, :] * lp[1:2, :], axis=-1, keepdims=True))
           - jnp.exp(jnp.sum(lp[2:3, :] * lp[3:4, :], axis=-1, keepdims=True))
           + LAMBDA_INIT)

    def write_output(lo, hi):
        a1 = acc_ref[0, :, lo:hi]
        a2 = acc_ref[1, :, lo:hi]
        ot = (a1[0:DV, :] * (1.0 / a1[DV:DV + 1, :])
              - (lam * a2[0:DV, :]) * (1.0 / a2[DV:DV + 1, :]))
        o = ot.T
        o = o * lax.rsqrt(jnp.mean(o * o, axis=-1, keepdims=True) + EPS)
        rows = pl.ds(pl.multiple_of(qi * tq + lo, hi - lo), hi - lo)
        o_ref[0, rows, :] = ((o * g_ref[...]) * (1.0 - LAMBDA_INIT)).astype(BF16)

    acc_ref[...] = jnp.zeros_like(acc_ref)
    e0 = jnp.full((1, tq), NEG_INF, F32)
    half = tq // 2

    def diagonal_region(e_max, pending_pv=()):
        d0 = 2 * qi
        left = tuple(e[:, 0:half] for e in e_max)
        right = tuple(e[:, half:tq] for e in e_max)
        for stage in pending_pv[0:1]:
            stage()
        left = exponentials_diagonal(d0, 1, 0, 0, half, left)
        right = exponentials_below(d0, 0, half, tq, right)
        for stage in pending_pv[1:2]:
            stage()
        right = exponentials_diagonal(d0 + 1, 2, 1, half, tq, right)
        accumulate_columns(d0, 0, 0, half)
        write_output(0, half)
        accumulate_columns(d0, 0, half, tq)
        accumulate_columns(d0 + 1, 1, half, tq)
        write_output(half, tq)
        return left, right

    def first_query_block():
        return diagonal_region((e0, e0))

    def later_query_block():
        def trip(t, e_max):
            e_max = exponentials_below(2 * t + 1, 1, 0, tq, e_max)
            accumulate(2 * t, 0)
            e_max = exponentials_below(2 * t + 2, 0, 0, tq, e_max)
            accumulate(2 * t + 1, 1)
            return e_max

        e_max = lax.fori_loop(0, qi - 1, trip, exponentials_below(0, 0, 0, tq, (e0, e0)))
        d0 = 2 * qi
        e_max = exponentials_below(d0 - 1, 1, 0, tq, e_max)
        return diagonal_region(e_max, (lambda: accumulate(d0 - 2, 0), lambda: accumulate(d0 - 1, 1)))

    left, right = lax.cond(qi == 0, first_query_block, later_query_block)
    smallest = functools.reduce(jnp.minimum, [jnp.min(e) for e in left + right])

    @pl.when(smallest < ATTN_MIN_EXPONENT)
    def _():
        def body(kj, m_run):
            kb = key_block(kj)
            bias = bias_tile(kj)
            shift = bias_shift(kj)
            m_new, alpha = [], []
            for c in range(2):
                s = _dot(kb, qmaps[c]) + bias
                s_ref[c] = s
                m_c = jnp.maximum(m_run[c], jnp.max(s, axis=0, keepdims=True) + shift)
                alpha.append(jnp.exp2(m_run[c] - m_c))
                p_ref[0, c] = jnp.exp2(s_ref[c] - (m_c - shift)).astype(BF16)
                m_new.append(m_c)
            accumulate(kj, 0, alpha)
            return tuple(m_new)

        acc_ref[...] = jnp.zeros_like(acc_ref)
        lax.fori_loop(0, last + 1, body, (e0, e0))
        write_output(0, tq)


def _attn_bias_tiles():
    tq, tk = ATTN_QBLOCK, ATTN_KBLOCK
    jj = np.arange(tk, dtype=np.int64)[:, None]
    ii = np.arange(tq, dtype=np.int64)[None, :]
    slopes = np.asarray(ALIBI_SLOPES, np.float64)[:, None, None]
    tiles = [-slopes * (ii - jj)[None].astype(np.float64)]
    for r in range(tq // tk):
        key = r * tk + jj
        allowed = (key // CHUNK) <= (ii // CHUNK)
        tiles.append(np.where(allowed[None], -slopes * (np.abs(ii - key)[None] + r * tk), NEG_INF))
    return (LOG2E * np.stack(tiles, axis=1)).astype(np.float32)


def _attn_key_position_columns():
    tk = ATTN_KBLOCK
    rest = LOG2E * np.asarray(ALIBI_SLOPES, np.float64)[:, None] * np.arange(tk, dtype=np.float64)[None, :]
    out = np.zeros((N_HEADS, 2, tk, 2 * DH), np.float32)
    for n in range(ATTN_POS_COLS):
        part = rest.astype(ml_dtypes.bfloat16).astype(np.float64)
        out[:, 0, :, DH + n] = part
        out[:, 1, :, n] = part
        rest = rest - part
    return out.astype(ml_dtypes.bfloat16)


def _attn_call(qt, k3, vt, k_sq, lam_params, subln_g, *, batch, seq):
    tq, tk = ATTN_QBLOCK, ATTN_KBLOCK
    nkb = seq // tk
    d_attn = N_HEADS * DV
    bias = jnp.asarray(_attn_bias_tiles())
    kpos = jnp.asarray(_attn_key_position_columns())
    slopes = jnp.asarray([LOG2E * a for a in ALIBI_SLOPES], F32)
    return pl.pallas_call(
        _attn_kernel,
        grid=(batch, N_HEADS),
        in_specs=[
            pl.BlockSpec(memory_space=pltpu.SMEM),
            pl.BlockSpec((1, seq // tq, DV, tq), lambda b, h: (b, 0, h, 0)),
            pl.BlockSpec((1, seq, DV), lambda b, h: (b, 0, h)),
            pl.BlockSpec((1, nkb, DV, tk), lambda b, h: (b, 0, h, 0)),
            pl.BlockSpec((1,) + k_sq.shape[1:], lambda b, h: (b, 0, 0, 0)),
            pl.BlockSpec((1,) + bias.shape[1:], lambda b, h: (h, 0, 0, 0)),
            pl.BlockSpec((1,) + kpos.shape[1:], lambda b, h: (h, 0, 0, 0)),
            pl.BlockSpec(lam_params.shape, lambda b, h: (0, 0)),
            pl.BlockSpec(subln_g.shape, lambda b, h: (0, 0)),
        ],
        out_specs=pl.BlockSpec((1, seq, DV), lambda b, h: (b, 0, h)),
        out_shape=jax.ShapeDtypeStruct((batch, seq, d_attn), BF16),
        scratch_shapes=[pltpu.VMEM((2, DV + ATTN_ONES_ROWS, tq), F32),
                        pltpu.VMEM((2, tk, tq), F32),
                        pltpu.VMEM((2, 2, tk, tq), BF16)],
        compiler_params=pltpu.CompilerParams(
            dimension_semantics=("arbitrary", "arbitrary"),
            vmem_limit_bytes=V7X_VMEM_LIMIT),
        name="diff_attn",
    )(slopes, qt, k3, vt, k_sq, bias, kpos, lam_params, subln_g)


def _post_kernel(x_ref, o_ref, ma_ref, sgb_ref, mod_ref, gf_ref, gfin_ref,
                 wob_ref, wo_ref, wg_ref, wu_ref, wd_ref, y_ref):
    mod = mod_ref[0]
    tm = x_ref.shape[0]
    groups = [slice(g * tm // POST_ROW_GROUPS, (g + 1) * tm // POST_ROW_GROUPS)
              for g in range(POST_ROW_GROUPS)]

    def mixer_residual(rows):
        y_b = _dot(o_ref[rows, :], wob_ref[...])
        merged = ma_ref[rows, :].astype(F32) + sgb_ref[rows, :].astype(F32) * y_b
        return x_ref[rows, :] + mod[2:3, :] * _dot(merged.astype(BF16), wo_ref[...])

    def ffn_input(x1):
        xn = x1 * lax.rsqrt(jnp.mean(x1 * x1, axis=-1, keepdims=True) + EPS)
        return ((xn * gf_ref[...]) * (1.0 + mod[4:5, :]) + mod[3:4, :]).astype(BF16)

    def ffn_hidden(h2):
        gate = _dot(h2, wg_ref[...])
        up = _dot(h2, wu_ref[...])
        return ((gate * _sigmoid(gate)) * up).astype(BF16)

    def finish(rows, x1, act):
        x2 = x1 + mod[5:6, :] * _dot(act, wd_ref[...])
        y_ref[rows, :] = (x2 * lax.rsqrt(jnp.mean(x2 * x2, axis=-1, keepdims=True) + EPS)) * gfin_ref[...]

    x1 = [mixer_residual(rows) for rows in groups]
    h2 = [ffn_input(v) for v in x1]
    act = [ffn_hidden(v) for v in h2]
    for rows, a, b in zip(groups, x1, act):
        finish(rows, a, b)


def _post_call(x2, o2, ma, sgb, mod3, g_ffn, g_final, wob, wo, wg, wu, wd, *, seq):
    t, d = x2.shape
    tm = POST_BLOCK_M
    tps = seq // tm
    full = lambda a: pl.BlockSpec(a.shape, lambda i: (0,) * a.ndim)
    row = lambda a: pl.BlockSpec((tm, a.shape[1]), lambda i: (i, 0))
    return pl.pallas_call(
        _post_kernel,
        grid=(t // tm,),
        in_specs=[
            row(x2), row(o2), row(ma), row(sgb),
            pl.BlockSpec((1,) + mod3.shape[1:], lambda i: (i // tps, 0, 0)),
            full(g_ffn), full(g_final), full(wob), full(wo), full(wg), full(wu), full(wd),
        ],
        out_specs=pl.BlockSpec((tm, d), lambda i: (i, 0)),
        out_shape=jax.ShapeDtypeStruct((t, d), F32),
        compiler_params=pltpu.CompilerParams(
            dimension_semantics=("arbitrary",), vmem_limit_bytes=V7X_VMEM_LIMIT),
        name="mixer_out_ffn",
    )(x2, o2, ma, sgb, mod3, g_ffn, g_final, wob, wo, wg, wu, wd)


def kernel(x, c, w_ada, b_ada, g_mix, w_in, conv_w, w_out_a, lambda_q1, lambda_k1, lambda_q2,
           lambda_k2, subln_g, w_out_b, w_out, g_ffn, w_gate, w_up, w_down, g_final):
    batch, seq, d = x.shape
    depth = w_ada.shape[0]
    assert depth == 1 and seq % PRE_BLOCK_M == 0 and seq % POST_BLOCK_M == 0
    assert PRE_BLOCK_M % ATTN_KBLOCK == 0 and ATTN_KBLOCK % CHUNK == 0
    assert ATTN_KBLOCK % (PRE_BLOCK_M // PRE_ROW_GROUPS) == 0 and PRE_BLOCK_M % (8 * PRE_ROW_GROUPS) == 0
    assert POST_BLOCK_M % (8 * POST_ROW_GROUPS) == 0
    assert ATTN_QBLOCK == 2 * ATTN_KBLOCK and seq % ATTN_QBLOCK == 0
    assert PRE_BLOCK_M % ATTN_QBLOCK == 0 and ATTN_QBLOCK % (PRE_BLOCK_M // PRE_ROW_GROUPS) == 0
    l = 0
    x2 = x.reshape(batch * seq, d)

    c_pad = jnp.pad(c, ((0, 8 - batch), (0, 0)))
    mod = _ada_call(c_pad, w_ada[l], b_ada[l].reshape(1, -1))
    mod3 = mod[:batch].reshape(batch, 6, d)

    ma, sgb, qt, k2, vt, k_sq = _pre_call(
        x2, mod3, g_mix[l].reshape(1, d), w_in[l].astype(BF16), conv_w[l],
        w_out_a[l].astype(BF16), batch=batch, seq=seq)

    lam_params = jnp.stack([lambda_q1[l], lambda_k1[l], lambda_q2[l], lambda_k2[l]], axis=0)
    o = _attn_call(qt, k2.reshape(batch, seq, -1), vt, k_sq.reshape(batch, -1, 8, 128),
                   lam_params, subln_g[l].reshape(1, DV), batch=batch, seq=seq)

    y = _post_call(
        x2, o.reshape(batch * seq, -1), ma, sgb, mod3, g_ffn[l].reshape(1, d),
        g_final.reshape(1, d), w_out_b[l].astype(BF16), w_out[l].astype(BF16),
        w_gate[l].astype(BF16), w_up[l].astype(BF16), w_down[l].astype(BF16), seq=seq)
    return y.reshape(batch, seq, d)
```

```python
import functools
import math

import jax
import jax.numpy as jnp
import ml_dtypes
import numpy as np
from jax import lax
from jax.experimental import pallas as pl
from jax.experimental.pallas import tpu as pltpu

CHUNK = 64
CONV_WIDTH = 3
N_HEADS = 4
DH = 64
DV = 2 * DH
EPS = 1e-6
NEG_INF = -1e30
LAMBDA_INIT = 0.8 - 0.6 * math.exp(-0.3 * 1)
LOG2E = math.log2(math.e)
ALIBI_SLOPES = tuple(2.0 ** (-8.0 * (i + 1) / N_HEADS) for i in range(N_HEADS))

ADA_BLOCK_N = 1536
PRE_BLOCK_M = 1024
PRE_ROW_GROUPS = 4
POST_BLOCK_M = 512
POST_ROW_GROUPS = 2
ATTN_KBLOCK = 512
ATTN_QBLOCK = 1024
ATTN_ONES_ROWS = 16
ATTN_POS_COLS = 3
ATTN_BOUND_SLACK = 1.01
ATTN_BOUND_MARGIN = 0.01
ATTN_MIN_EXPONENT = -100.0
V7X_VMEM_LIMIT = 56 * 1024 * 1024

BF16 = jnp.bfloat16
F32 = jnp.float32


def _dot(a, b):
    return jnp.dot(a, b, preferred_element_type=F32)


def _sigmoid(x):
    return 1.0 / (1.0 + jnp.exp(-x))


def _ada_kernel(c_ref, w_ref, b_ref, o_ref):
    c = c_ref[...]
    sc = (c * _sigmoid(c)).astype(BF16)
    o_ref[...] = _dot(sc, w_ref[...].astype(BF16)) + b_ref[...]


def _ada_call(c_pad, w_ada, b_ada):
    rows, d = c_pad.shape
    n = w_ada.shape[1]
    return pl.pallas_call(
        _ada_kernel,
        grid=(n // ADA_BLOCK_N,),
        in_specs=[
            pl.BlockSpec((rows, d), lambda j: (0, 0)),
            pl.BlockSpec((d, ADA_BLOCK_N), lambda j: (0, j)),
            pl.BlockSpec((1, ADA_BLOCK_N), lambda j: (0, j)),
        ],
        out_specs=pl.BlockSpec((rows, ADA_BLOCK_N), lambda j: (0, j)),
        out_shape=jax.ShapeDtypeStruct((rows, n), F32),
        compiler_params=pltpu.CompilerParams(
            dimension_semantics=("arbitrary",), vmem_limit_bytes=V7X_VMEM_LIMIT),
        name="ada_mod",
    )(c_pad, w_ada, b_ada)


def _pre_kernel(x_ref, mod_ref, g_ref, win_ref, cw_ref, woa_ref,
                ma_ref, sgb_ref, qt_ref, k_ref, vt_ref, ksq_ref, carry_ref,
                *, tiles_per_seq, d_conv, d_attn, d_model):
    i = pl.program_id(0)
    tm = x_ref.shape[0]
    mod = mod_ref[0]

    @pl.when(i % tiles_per_seq == 0)
    def _():
        carry_ref[...] = jnp.zeros_like(carry_ref)

    rg = tm // PRE_ROW_GROUPS
    groups = [slice(g * rg, (g + 1) * rg) for g in range(PRE_ROW_GROUPS)]

    a0 = 3 * d_conv
    g0 = a0 + 3 * d_attn

    def normed(rows):
        x = x_ref[rows, :]
        xn = x * lax.rsqrt(jnp.mean(x * x, axis=-1, keepdims=True) + EPS)
        return ((xn * g_ref[...]) * (1.0 + mod[1:2, :]) + mod[0:1, :]).astype(BF16)

    hb = [normed(rows) for rows in groups]

    pc = [_dot(h, win_ref[:, 0:a0]) for h in hb]
    cw = cw_ref[...]
    before = carry_ref[...]
    y_a = []
    for p in pc:
        u = p[:, 0:d_conv]
        gb = p[:, d_conv:2 * d_conv]
        gc = p[:, 2 * d_conv:3 * d_conv]
        cu = gc * u
        ext = jnp.concatenate([before, cu], axis=0)
        prev1 = pltpu.roll(ext, 1, axis=0)[8:, :]
        prev2 = pltpu.roll(ext, 2, axis=0)[8:, :]
        before = cu[rg - 8:, :]
        z = cw[0:1, :] * prev2 + cw[1:2, :] * prev1 + cw[2:3, :] * cu
        y_a.append(_dot((gb * z).astype(BF16), woa_ref[...]))
    carry_ref[...] = before

    for rows, h, ya in zip(groups, hb, y_a):
        pg = _dot(h, win_ref[:, g0:g0 + 2 * d_model])
        ma_ref[rows, :] = (_sigmoid(pg[:, 0:d_model]) * ya).astype(BF16)
        sgb_ref[rows, :] = _sigmoid(pg[:, d_model:2 * d_model]).astype(BF16)

    lane = lax.broadcasted_iota(jnp.int32, ksq_ref.shape[1:], 1)
    tile = jnp.zeros(ksq_ref.shape[1:], F32)
    for g, (rows, h) in enumerate(zip(groups, hb)):
        pa = _dot(h, win_ref[:, a0:g0])
        q = pa[:, 0:d_attn] * (LOG2E / math.sqrt(DH))
        blk, off = divmod(g * rg, ATTN_QBLOCK)
        qt_ref[0, blk, :, off:off + rg] = q.T.astype(BF16)
        kb = pa[:, d_attn:2 * d_attn].astype(BF16)
        k_ref[rows, :] = kb
        kf = kb.astype(F32)
        sq = kf * kf
        for hd in range(N_HEADS):
            norm_sq = jnp.sum(sq[:, hd * DV:(hd + 1) * DV], axis=-1, keepdims=True)
            tile = jnp.where(lane == hd,
                             jnp.maximum(tile, jnp.max(norm_sq, axis=0, keepdims=True)), tile)
        v = pa[:, 2 * d_attn:3 * d_attn]
        blk, off = divmod(g * rg, ATTN_KBLOCK)
        vt_ref[0, blk, :, off:off + rg] = v.T.astype(BF16)
    ksq_ref[0] = tile


def _pre_call(x2, mod3, g_mix, w_in_b, conv_w, w_out_a_b, *, batch, seq):
    t, d = x2.shape
    d_conv = conv_w.shape[1]
    d_attn = N_HEADS * DV
    tm = PRE_BLOCK_M
    tps = seq // tm
    nkb = seq // ATTN_KBLOCK
    kern = functools.partial(_pre_kernel, tiles_per_seq=tps, d_conv=d_conv,
                             d_attn=d_attn, d_model=d)
    full = lambda a: pl.BlockSpec(a.shape, lambda i: (0,) * a.ndim)
    return pl.pallas_call(
        kern,
        grid=(t // tm,),
        in_specs=[
            pl.BlockSpec((tm, d), lambda i: (i, 0)),
            pl.BlockSpec((1,) + mod3.shape[1:], lambda i: (i // tps, 0, 0)),
            full(g_mix), full(w_in_b), full(conv_w), full(w_out_a_b),
        ],
        out_specs=[
            pl.BlockSpec((tm, d), lambda i: (i, 0)),
            pl.BlockSpec((tm, d), lambda i: (i, 0)),
            pl.BlockSpec((1, tm // ATTN_QBLOCK, d_attn, ATTN_QBLOCK),
                         lambda i: (i // tps, i % tps, 0, 0)),
            pl.BlockSpec((tm, d_attn), lambda i: (i, 0)),
            pl.BlockSpec((1, tm // ATTN_KBLOCK, d_attn, ATTN_KBLOCK),
                         lambda i: (i // tps, i % tps, 0, 0)),
            pl.BlockSpec((1, 8, 128), lambda i: (i, 0, 0)),
        ],
        out_shape=[
            jax.ShapeDtypeStruct((t, d), BF16),
            jax.ShapeDtypeStruct((t, d), BF16),
            jax.ShapeDtypeStruct((batch, seq // ATTN_QBLOCK, d_attn, ATTN_QBLOCK), BF16),
            jax.ShapeDtypeStruct((t, d_attn), BF16),
            jax.ShapeDtypeStruct((batch, nkb, d_attn, ATTN_KBLOCK), BF16),
            jax.ShapeDtypeStruct((t // tm, 8, 128), F32),
        ],
        scratch_shapes=[pltpu.VMEM((8, d_conv), F32)],
        compiler_params=pltpu.CompilerParams(
            dimension_semantics=("arbitrary",), vmem_limit_bytes=V7X_VMEM_LIMIT),
        name="mixer_in",
    )(x2, mod3, g_mix, w_in_b, conv_w, w_out_a_b)


def _attn_kernel(slope_ref, qt_ref, *refs):
    h = pl.program_id(1)

    def query_block(qi, carry):
        _attn_query_block(qi, h, slope_ref, qt_ref, *refs)
        return carry

    lax.fori_loop(0, qt_ref.shape[1], query_block, 0)


def _attn_query_block(qi, h, slope_ref, qt_ref, k_ref, vt_ref, ksq_ref, bias_ref, kpos_ref, lam_ref,
                      g_ref, o_ref, acc_ref, s_ref, p_ref):
    tq, tk = ATTN_QBLOCK, ATTN_KBLOCK
    ratio = tq // tk
    slope = slope_ref[h]

    qt = qt_ref[0, qi]
    row = lax.broadcasted_iota(jnp.int32, qt.shape, 0)
    zero = jnp.zeros_like(qt)
    qmaps = (jnp.where(row < DH, qt, zero), jnp.where(row >= DH, qt, zero))
    ones = jnp.ones((ATTN_ONES_ROWS, tk), BF16)

    last = ratio * qi + ratio - 1

    def key_block(kj):
        return k_ref[0, pl.ds(pl.multiple_of(kj * tk, tk), tk), :]

    def bias_tile(kj):
        return bias_ref[0, jnp.clip(kj - ratio * qi + 1, 0, ratio)]

    def bias_shift(kj):
        return -slope * jnp.asarray(qi * tq - kj * tk, F32)

    def accumulate(kj, slot, alphas=None):
        vb = jnp.concatenate([vt_ref[0, kj], ones], axis=0)
        for c in range(2):
            old = acc_ref[c] if alphas is None else alphas[c] * acc_ref[c]
            acc_ref[c] = old + _dot(vb, p_ref[slot, c])

    k_sq_all = jnp.max(ksq_ref[0], axis=0)
    lane = lax.broadcasted_iota(jnp.int32, k_sq_all.shape, 1)
    k_sq = jnp.max(jnp.where(lane == h, k_sq_all, 0.0), axis=-1, keepdims=True)[0:1]
    qf = qt.astype(F32)
    q_sq = qf * qf
    stab = [jnp.sqrt(jnp.sum(q_sq[c * DH:(c + 1) * DH, :], axis=0, keepdims=True) * k_sq)
            * ATTN_BOUND_SLACK + ATTN_BOUND_MARGIN for c in range(2)]

    def store_exponentials(e, slot, c, lo, hi, e_max):
        p_ref[slot, c, :, lo:hi] = jnp.exp2(e).astype(BF16)
        return jnp.maximum(e_max, jnp.max(e, axis=0, keepdims=True))

    def exponentials_diagonal(kj, tile, slot, lo, hi, e_max):
        kb = key_block(kj)
        bias = bias_ref[0, tile, :, lo:hi]
        shift = bias_shift(kj)
        return tuple(store_exponentials(
            _dot(kb, qmaps[c][:, lo:hi]) + bias + (shift - stab[c][:, lo:hi]), slot, c, lo, hi,
            e_max[c]) for c in range(2))

    lane_k = lax.broadcasted_iota(jnp.int32, (tk, 2 * DH), 1)
    ones_rows = (((row >= DH) & (row < DH + ATTN_POS_COLS)).astype(F32).astype(BF16),
                 (row < ATTN_POS_COLS).astype(F32).astype(BF16))
    q_aug = tuple(qmaps[c] + ones_rows[c] for c in range(2))
    pos = lax.broadcasted_iota(jnp.int32, (1, tq), 1).astype(F32)
    base = tuple(stab[c] + slope * pos for c in range(2))

    def exponentials_below(kj, slot, lo, hi, e_max):
        kb = key_block(kj)
        lhs = (jnp.where(lane_k < DH, kb, kpos_ref[0, 0]), jnp.where(lane_k >= DH, kb, kpos_ref[0, 1]))
        shift = bias_shift(kj)
        return tuple(store_exponentials(
            _dot(lhs[c], q_aug[c][:, lo:hi]) - (base[c][:, lo:hi] - shift), slot, c, lo, hi,
            e_max[c]) for c in range(2))

    def accumulate_columns(kj, slot, lo, hi):
        vb = jnp.concatenate([vt_ref[0, kj], ones], axis=0)
        for c in range(2):
            acc_ref[c, :, lo:hi] = acc_ref[c, :, lo:hi] + _dot(vb, p_ref[slot, c, :, lo:hi])

    lp = lam_ref[...]
    lam = (jnp.exp(jnp.sum(lp[0:1, :] * lp[1:2, :], axis=-1, keepdims=True))
           - jnp.exp(jnp.sum(lp[2:3, :] * lp[3:4, :], axis=-1, keepdims=True))
           + LAMBDA_INIT)

    def write_output(lo, hi):
        a1 = acc_ref[0, :, lo:hi]
        a2 = acc_ref[1, :, lo:hi]
        ot = (a1[0:DV, :] * (1.0 / a1[DV:DV + 1, :])
              - (lam * a2[0:DV, :]) * (1.0 / a2[DV:DV + 1, :]))
        o = ot.T
        o = o * lax.rsqrt(jnp.mean(o * o, axis=-1, keepdims=True) + EPS)
        rows = pl.ds(pl.multiple_of(qi * tq + lo, hi - lo), hi - lo)
        o_ref[0, rows, :] = ((o * g_ref[...]) * (1.0 - LAMBDA_INIT)).astype(BF16)

    acc_ref[...] = jnp.zeros_like(acc_ref)
    e0 = jnp.full((1, tq), NEG_INF, F32)
    half = tq // 2

    def diagonal_region(e_max, pending_pv=()):
        d0 = 2 * qi
        left = tuple(e[:, 0:half] for e in e_max)
        right = tuple(e[:, half:tq] for e in e_max)
        for stage in pending_pv[0:1]:
            stage()
        left = exponentials_diagonal(d0, 1, 0, 0, half, left)
        right = exponentials_below(d0, 0, half, tq, right)
        for stage in pending_pv[1:2]:
            stage()
        right = exponentials_diagonal(d0 + 1, 2, 1, half, tq, right)
        accumulate_columns(d0, 0, 0, half)
        write_output(0, half)
        accumulate_columns(d0, 0, half, tq)
        accumulate_columns(d0 + 1, 1, half, tq)
        write_output(half, tq)
        return left, right

    def first_query_block():
        return diagonal_region((e0, e0))

    def later_query_block():
        def trip(t, e_max):
            e_max = exponentials_below(2 * t + 1, 1, 0, tq, e_max)
            accumulate(2 * t, 0)
            e_max = exponentials_below(2 * t + 2, 0, 0, tq, e_max)
            accumulate(2 * t + 1, 1)
            return e_max

        e_max = lax.fori_loop(0, qi - 1, trip, exponentials_below(0, 0, 0, tq, (e0, e0)))
        d0 = 2 * qi
        e_max = exponentials_below(d0 - 1, 1, 0, tq, e_max)
        return diagonal_region(e_max, (lambda: accumulate(d0 - 2, 0), lambda: accumulate(d0 - 1, 1)))

    left, right = lax.cond(qi == 0, first_query_block, later_query_block)
    smallest = functools.reduce(jnp.minimum, [jnp.min(e) for e in left + right])

    @pl.when(smallest < ATTN_MIN_EXPONENT)
    def _():
        def body(kj, m_run):
            kb = key_block(kj)
            bias = bias_tile(kj)
            shift = bias_shift(kj)
            m_new, alpha = [], []
            for c in range(2):
                s = _dot(kb, qmaps[c]) + bias
                s_ref[c] = s
                m_c = jnp.maximum(m_run[c], jnp.max(s, axis=0, keepdims=True) + shift)
                alpha.append(jnp.exp2(m_run[c] - m_c))
                p_ref[0, c] = jnp.exp2(s_ref[c] - (m_c - shift)).astype(BF16)
                m_new.append(m_c)
            accumulate(kj, 0, alpha)
            return tuple(m_new)

        acc_ref[...] = jnp.zeros_like(acc_ref)
        lax.fori_loop(0, last + 1, body, (e0, e0))
        write_output(0, tq)


def _attn_bias_tiles():
    tq, tk = ATTN_QBLOCK, ATTN_KBLOCK
    jj = np.arange(tk, dtype=np.int64)[:, None]
    ii = np.arange(tq, dtype=np.int64)[None, :]
    slopes = np.asarray(ALIBI_SLOPES, np.float64)[:, None, None]
    tiles = [-slopes * (ii - jj)[None].astype(np.float64)]
    for r in range(tq // tk):
        key = r * tk + jj
        allowed = (key // CHUNK) <= (ii // CHUNK)
        tiles.append(np.where(allowed[None], -slopes * (np.abs(ii - key)[None] + r * tk), NEG_INF))
    return (LOG2E * np.stack(tiles, axis=1)).astype(np.float32)


def _attn_key_position_columns():
    tk = ATTN_KBLOCK
    rest = LOG2E * np.asarray(ALIBI_SLOPES, np.float64)[:, None] * np.arange(tk, dtype=np.float64)[None, :]
    out = np.zeros((N_HEADS, 2, tk, 2 * DH), np.float32)
    for n in range(ATTN_POS_COLS):
        part = rest.astype(ml_dtypes.bfloat16).astype(np.float64)
        out[:, 0, :, DH + n] = part
        out[:, 1, :, n] = part
        rest = rest - part
    return out.astype(ml_dtypes.bfloat16)


def _attn_call(qt, k3, vt, k_sq, lam_params, subln_g, *, batch, seq):
    tq, tk = ATTN_QBLOCK, ATTN_KBLOCK
    nkb = seq // tk
    d_attn = N_HEADS * DV
    bias = jnp.asarray(_attn_bias_tiles())
    kpos = jnp.asarray(_attn_key_position_columns())
    slopes = jnp.asarray([LOG2E * a for a in ALIBI_SLOPES], F32)
    return pl.pallas_call(
        _attn_kernel,
        grid=(batch, N_HEADS),
        in_specs=[
            pl.BlockSpec(memory_space=pltpu.SMEM),
            pl.BlockSpec((1, seq // tq, DV, tq), lambda b, h: (b, 0, h, 0)),
            pl.BlockSpec((1, seq, DV), lambda b, h: (b, 0, h)),
            pl.BlockSpec((1, nkb, DV, tk), lambda b, h: (b, 0, h, 0)),
            pl.BlockSpec((1,) + k_sq.shape[1:], lambda b, h: (b, 0, 0, 0)),
            pl.BlockSpec((1,) + bias.shape[1:], lambda b, h: (h, 0, 0, 0)),
            pl.BlockSpec((1,) + kpos.shape[1:], lambda b, h: (h, 0, 0, 0)),
            pl.BlockSpec(lam_params.shape, lambda b, h: (0, 0)),
            pl.BlockSpec(subln_g.shape, lambda b, h: (0, 0)),
        ],
        out_specs=pl.BlockSpec((1, seq, DV), lambda b, h: (b, 0, h)),
        out_shape=jax.ShapeDtypeStruct((batch, seq, d_attn), BF16),
        scratch_shapes=[pltpu.VMEM((2, DV + ATTN_ONES_ROWS, tq), F32),
                        pltpu.VMEM((2, tk, tq), F32),
                        pltpu.VMEM((2, 2, tk, tq), BF16)],
        compiler_params=pltpu.CompilerParams(
            dimension_semantics=("arbitrary", "arbitrary"),
            vmem_limit_bytes=V7X_VMEM_LIMIT),
        name="diff_attn",
    )(slopes, qt, k3, vt, k_sq, bias, kpos, lam_params, subln_g)


def _post_kernel(x_ref, o_ref, ma_ref, sgb_ref, mod_ref, gf_ref, gfin_ref,
                 wob_ref, wo_ref, wg_ref, wu_ref, wd_ref, y_ref):
    mod = mod_ref[0]
    tm = x_ref.shape[0]
    groups = [slice(g * tm // POST_ROW_GROUPS, (g + 1) * tm // POST_ROW_GROUPS)
              for g in range(POST_ROW_GROUPS)]

    def mixer_residual(rows):
        y_b = _dot(o_ref[rows, :], wob_ref[...])
        merged = ma_ref[rows, :].astype(F32) + sgb_ref[rows, :].astype(F32) * y_b
        return x_ref[rows, :] + mod[2:3, :] * _dot(merged.astype(BF16), wo_ref[...])

    def ffn_input(x1):
        xn = x1 * lax.rsqrt(jnp.mean(x1 * x1, axis=-1, keepdims=True) + EPS)
        return ((xn * gf_ref[...]) * (1.0 + mod[4:5, :]) + mod[3:4, :]).astype(BF16)

    def ffn_hidden(h2):
        gate = _dot(h2, wg_ref[...])
        up = _dot(h2, wu_ref[...])
        return ((gate * _sigmoid(gate)) * up).astype(BF16)

    def finish(rows, x1, act):
        x2 = x1 + mod[5:6, :] * _dot(act, wd_ref[...])
        y_ref[rows, :] = (x2 * lax.rsqrt(jnp.mean(x2 * x2, axis=-1, keepdims=True) + EPS)) * gfin_ref[...]

    x1 = [mixer_residual(rows) for rows in groups]
    h2 = [ffn_input(v) for v in x1]
    act = [ffn_hidden(v) for v in h2]
    for rows, a, b in zip(groups, x1, act):
        finish(rows, a, b)


def _post_call(x2, o2, ma, sgb, mod3, g_ffn, g_final, wob, wo, wg, wu, wd, *, seq):
    t, d = x2.shape
    tm = POST_BLOCK_M
    tps = seq // tm
    full = lambda a: pl.BlockSpec(a.shape, lambda i: (0,) * a.ndim)
    row = lambda a: pl.BlockSpec((tm, a.shape[1]), lambda i: (i, 0))
    return pl.pallas_call(
        _post_kernel,
        grid=(t // tm,),
        in_specs=[
            row(x2), row(o2), row(ma), row(sgb),
            pl.BlockSpec((1,) + mod3.shape[1:], lambda i: (i // tps, 0, 0)),
            full(g_ffn), full(g_final), full(wob), full(wo), full(wg), full(wu), full(wd),
        ],
        out_specs=pl.BlockSpec((tm, d), lambda i: (i, 0)),
        out_shape=jax.ShapeDtypeStruct((t, d), F32),
        compiler_params=pltpu.CompilerParams(
            dimension_semantics=("arbitrary",), vmem_limit_bytes=V7X_VMEM_LIMIT),
        name="mixer_out_ffn",
    )(x2, o2, ma, sgb, mod3, g_ffn, g_final, wob, wo, wg, wu, wd)


def kernel(x, c, w_ada, b_ada, g_mix, w_in, conv_w, w_out_a, lambda_q1, lambda_k1, lambda_q2,
           lambda_k2, subln_g, w_out_b, w_out, g_ffn, w_gate, w_up, w_down, g_final):
    batch, seq, d = x.shape
    depth = w_ada.shape[0]
    assert depth == 1 and seq % PRE_BLOCK_M == 0 and seq % POST_BLOCK_M == 0
    assert PRE_BLOCK_M % ATTN_KBLOCK == 0 and ATTN_KBLOCK % CHUNK == 0
    assert ATTN_KBLOCK % (PRE_BLOCK_M // PRE_ROW_GROUPS) == 0 and PRE_BLOCK_M % (8 * PRE_ROW_GROUPS) == 0
    assert POST_BLOCK_M % (8 * POST_ROW_GROUPS) == 0
    assert ATTN_QBLOCK == 2 * ATTN_KBLOCK and seq % ATTN_QBLOCK == 0
    assert PRE_BLOCK_M % ATTN_QBLOCK == 0 and ATTN_QBLOCK % (PRE_BLOCK_M // PRE_ROW_GROUPS) == 0
    l = 0
    x2 = x.reshape(batch * seq, d)

    c_pad = jnp.pad(c, ((0, 8 - batch), (0, 0)))
    mod = _ada_call(c_pad, w_ada[l], b_ada[l].reshape(1, -1))
    mod3 = mod.reshape(-1, 6, d)

    ma, sgb, qt, k2, vt, k_sq = _pre_call(
        x2, mod3, g_mix[l].reshape(1, d), w_in[l].astype(BF16), conv_w[l],
        w_out_a[l].astype(BF16), batch=batch, seq=seq)

    lam_params = jnp.stack([lambda_q1[l], lambda_k1[l], lambda_q2[l], lambda_k2[l]], axis=0)
    o = _attn_call(qt, k2.reshape(batch, seq, -1), vt, k_sq.reshape(batch, -1, 8, 128),
                   lam_params, subln_g[l].reshape(1, DV), batch=batch, seq=seq)

    y = _post_call(
        x2, o.reshape(batch * seq, -1), ma, sgb, mod3, g_ffn[l].reshape(1, d),
        g_final.reshape(1, d), w_out_b[l].astype(BF16), w_out[l].astype(BF16),
        w_gate[l].astype(BF16), w_up[l].astype(BF16), w_down[l].astype(BF16), seq=seq)
    return y.reshape(batch, seq, d)
```

```python
import functools
import math

import jax
import jax.numpy as jnp
import ml_dtypes
import numpy as np
from jax import lax
from jax.experimental import pallas as pl
from jax.experimental.pallas import tpu as pltpu

CHUNK = 64
CONV_WIDTH = 3
N_HEADS = 4
DH = 64
DV = 2 * DH
EPS = 1e-6
NEG_INF = -1e30
LAMBDA_INIT = 0.8 - 0.6 * math.exp(-0.3 * 1)
LOG2E = math.log2(math.e)
ALIBI_SLOPES = tuple(2.0 ** (-8.0 * (i + 1) / N_HEADS) for i in range(N_HEADS))

ADA_BLOCK_N = 1536
PRE_BLOCK_M = 1024
PRE_ROW_GROUPS = 4
POST_BLOCK_M = 512
POST_ROW_GROUPS = 2
ATTN_KBLOCK = 512
ATTN_QBLOCK = 1024
ATTN_ONES_ROWS = 16
ATTN_POS_COLS = 3
ATTN_BOUND_SLACK = 1.01
ATTN_BOUND_MARGIN = 0.01
ATTN_MIN_EXPONENT = -100.0
V7X_VMEM_LIMIT = 56 * 1024 * 1024

BF16 = jnp.bfloat16
F32 = jnp.float32


def _dot(a, b):
    return jnp.dot(a, b, preferred_element_type=F32)


def _sigmoid(x):
    return 1.0 / (1.0 + jnp.exp(-x))


def _ada_kernel(c_ref, w_ref, b_ref, o_ref):
    c = c_ref[...]
    sc = (c * _sigmoid(c)).astype(BF16)
    o_ref[...] = _dot(sc, w_ref[...].astype(BF16)) + b_ref[...]


def _ada_call(c_pad, w_ada, b_ada):
    rows, d = c_pad.shape
    n = w_ada.shape[1]
    return pl.pallas_call(
        _ada_kernel,
        grid=(n // ADA_BLOCK_N,),
        in_specs=[
            pl.BlockSpec((rows, d), lambda j: (0, 0)),
            pl.BlockSpec((d, ADA_BLOCK_N), lambda j: (0, j)),
            pl.BlockSpec((1, ADA_BLOCK_N), lambda j: (0, j)),
        ],
        out_specs=pl.BlockSpec((rows, ADA_BLOCK_N), lambda j: (0, j)),
        out_shape=jax.ShapeDtypeStruct((rows, n), F32),
        compiler_params=pltpu.CompilerParams(
            dimension_semantics=("arbitrary",), vmem_limit_bytes=V7X_VMEM_LIMIT),
        name="ada_mod",
    )(c_pad, w_ada, b_ada)


def _pre_kernel(x_ref, mod_ref, g_ref, win_ref, cw_ref, woa_ref,
                ma_ref, sgb_ref, qt_ref, k_ref, vt_ref, ksq_ref, carry_ref,
                *, tiles_per_seq, d_conv, d_attn, d_model):
    i = pl.program_id(0)
    tm = x_ref.shape[0]
    mod = mod_ref[0]

    @pl.when(i % tiles_per_seq == 0)
    def _():
        carry_ref[...] = jnp.zeros_like(carry_ref)

    rg = tm // PRE_ROW_GROUPS
    groups = [slice(g * rg, (g + 1) * rg) for g in range(PRE_ROW_GROUPS)]

    a0 = 3 * d_conv
    g0 = a0 + 3 * d_attn

    def normed(rows):
        x = x_ref[rows, :]
        xn = x * lax.rsqrt(jnp.mean(x * x, axis=-1, keepdims=True) + EPS)
        return ((xn * g_ref[...]) * (1.0 + mod[1:2, :]) + mod[0:1, :]).astype(BF16)

    hb = [normed(rows) for rows in groups]

    pc = [_dot(h, win_ref[:, 0:a0]) for h in hb]
    cw = cw_ref[...]
    before = carry_ref[...]
    y_a = []
    for p in pc:
        u = p[:, 0:d_conv]
        gb = p[:, d_conv:2 * d_conv]
        gc = p[:, 2 * d_conv:3 * d_conv]
        cu = gc * u
        ext = jnp.concatenate([before, cu], axis=0)
        prev1 = pltpu.roll(ext, 1, axis=0)[8:, :]
        prev2 = pltpu.roll(ext, 2, axis=0)[8:, :]
        before = cu[rg - 8:, :]
        z = cw[0:1, :] * prev2 + cw[1:2, :] * prev1 + cw[2:3, :] * cu
        y_a.append(_dot((gb * z).astype(BF16), woa_ref[...]))
    carry_ref[...] = before

    for rows, h, ya in zip(groups, hb, y_a):
        pg = _dot(h, win_ref[:, g0:g0 + 2 * d_model])
        ma_ref[rows, :] = (_sigmoid(pg[:, 0:d_model]) * ya).astype(BF16)
        sgb_ref[rows, :] = _sigmoid(pg[:, d_model:2 * d_model]).astype(BF16)

    lane = lax.broadcasted_iota(jnp.int32, ksq_ref.shape[1:], 1)
    tile = jnp.zeros(ksq_ref.shape[1:], F32)
    for g, (rows, h) in enumerate(zip(groups, hb)):
        pa = _dot(h, win_ref[:, a0:g0])
        q = pa[:, 0:d_attn] * (LOG2E / math.sqrt(DH))
        blk, off = divmod(g * rg, ATTN_QBLOCK)
        qt_ref[0, blk, :, off:off + rg] = q.T.astype(BF16)
        kb = pa[:, d_attn:2 * d_attn].astype(BF16)
        k_ref[rows, :] = kb
        kf = kb.astype(F32)
        sq = kf * kf
        for hd in range(N_HEADS):
            norm_sq = jnp.sum(sq[:, hd * DV:(hd + 1) * DV], axis=-1, keepdims=True)
            tile = jnp.where(lane == hd,
                             jnp.maximum(tile, jnp.max(norm_sq, axis=0, keepdims=True)), tile)
        v = pa[:, 2 * d_attn:3 * d_attn]
        blk, off = divmod(g * rg, ATTN_KBLOCK)
        vt_ref[0, blk, :, off:off + rg] = v.T.astype(BF16)
    ksq_ref[0] = tile


def _pre_call(x2, mod3, g_mix, w_in_b, conv_w, w_out_a_b, *, batch, seq):
    t, d = x2.shape
    d_conv = conv_w.shape[1]
    d_attn = N_HEADS * DV
    tm = PRE_BLOCK_M
    tps = seq // tm
    nkb = seq // ATTN_KBLOCK
    kern = functools.partial(_pre_kernel, tiles_per_seq=tps, d_conv=d_conv,
                             d_attn=d_attn, d_model=d)
    full = lambda a: pl.BlockSpec(a.shape, lambda i: (0,) * a.ndim)
    return pl.pallas_call(
        kern,
        grid=(t // tm,),
        in_specs=[
            pl.BlockSpec((tm, d), lambda i: (i, 0)),
            pl.BlockSpec((1,) + mod3.shape[1:], lambda i: (i // tps, 0, 0)),
            full(g_mix), full(w_in_b), full(conv_w), full(w_out_a_b),
        ],
        out_specs=[
            pl.BlockSpec((tm, d), lambda i: (i, 0)),
            pl.BlockSpec((tm, d), lambda i: (i, 0)),
            pl.BlockSpec((1, tm // ATTN_QBLOCK, d_attn, ATTN_QBLOCK),
                         lambda i: (i // tps, i % tps, 0, 0)),
            pl.BlockSpec((tm, d_attn), lambda i: (i, 0)),
            pl.BlockSpec((1, tm // ATTN_KBLOCK, d_attn, ATTN_KBLOCK),
                         lambda i: (i // tps, i % tps, 0, 0)),
            pl.BlockSpec((1, 8, 128), lambda i: (i, 0, 0)),
        ],
        out_shape=[
            jax.ShapeDtypeStruct((t, d), BF16),
            jax.ShapeDtypeStruct((t, d), BF16),
            jax.ShapeDtypeStruct((batch, seq // ATTN_QBLOCK, d_attn, ATTN_QBLOCK), BF16),
            jax.ShapeDtypeStruct((t, d_attn), BF16),
            jax.ShapeDtypeStruct((batch, nkb, d_attn, ATTN_KBLOCK), BF16),
            jax.ShapeDtypeStruct((t // tm, 8, 128), F32),
        ],
        scratch_shapes=[pltpu.VMEM((8, d_conv), F32)],
        compiler_params=pltpu.CompilerParams(
            dimension_semantics=("arbitrary",), vmem_limit_bytes=V7X_VMEM_LIMIT),
        name="mixer_in",
    )(x2, mod3, g_mix, w_in_b, conv_w, w_out_a_b)


def _attn_kernel(slope_ref, qt_ref, *refs):
    h = pl.program_id(1)

    def query_block(qi, carry):
        _attn_query_block(qi, h, slope_ref, qt_ref, *refs)
        return carry

    lax.fori_loop(0, qt_ref.shape[1], query_block, 0)


def _attn_query_block(qi, h, slope_ref, qt_ref, k_ref, vt_ref, ksq_ref, bias_ref, kpos_ref, lam_ref,
                      g_ref, o_ref, acc_ref, s_ref, p_ref):
    tq, tk = ATTN_QBLOCK, ATTN_KBLOCK
    ratio = tq // tk
    slope = slope_ref[h]

    qt = qt_ref[0, qi]
    row = lax.broadcasted_iota(jnp.int32, qt.shape, 0)
    zero = jnp.zeros_like(qt)
    qmaps = (jnp.where(row < DH, qt, zero), jnp.where(row >= DH, qt, zero))
    ones = jnp.ones((ATTN_ONES_ROWS, tk), BF16)

    last = ratio * qi + ratio - 1

    def key_block(kj):
        return k_ref[0, pl.ds(pl.multiple_of(kj * tk, tk), tk), :]

    def bias_tile(kj):
        return bias_ref[0, jnp.clip(kj - ratio * qi + 1, 0, ratio)]

    def bias_shift(kj):
        return -slope * jnp.asarray(qi * tq - kj * tk, F32)

    def accumulate(kj, slot, alphas=None):
        vb = jnp.concatenate([vt_ref[0, kj], ones], axis=0)
        for c in range(2):
            old = acc_ref[c] if alphas is None else alphas[c] * acc_ref[c]
            acc_ref[c] = old + _dot(vb, p_ref[slot, c])

    k_sq_all = jnp.max(ksq_ref[0], axis=0)
    lane = lax.broadcasted_iota(jnp.int32, k_sq_all.shape, 1)
    k_sq = jnp.max(jnp.where(lane == h, k_sq_all, 0.0), axis=-1, keepdims=True)[0:1]
    qf = qt.astype(F32)
    q_sq = qf * qf
    stab = [jnp.sqrt(jnp.sum(q_sq[c * DH:(c + 1) * DH, :], axis=0, keepdims=True) * k_sq)
            * ATTN_BOUND_SLACK + ATTN_BOUND_MARGIN for c in range(2)]

    def store_exponentials(e, slot, c, lo, hi, e_max):
        p_ref[slot, c, :, lo:hi] = jnp.exp2(e).astype(BF16)
        return jnp.maximum(e_max, jnp.max(e, axis=0, keepdims=True))

    def exponentials_diagonal(kj, tile, slot, lo, hi, e_max):
        kb = key_block(kj)
        bias = bias_ref[0, tile, :, lo:hi]
        shift = bias_shift(kj)
        return tuple(store_exponentials(
            _dot(kb, qmaps[c][:, lo:hi]) + bias + (shift - stab[c][:, lo:hi]), slot, c, lo, hi,
            e_max[c]) for c in range(2))

    lane_k = lax.broadcasted_iota(jnp.int32, (tk, 2 * DH), 1)
    ones_rows = (((row >= DH) & (row < DH + ATTN_POS_COLS)).astype(F32).astype(BF16),
                 (row < ATTN_POS_COLS).astype(F32).astype(BF16))
    q_aug = tuple(qmaps[c] + ones_rows[c] for c in range(2))
    pos = lax.broadcasted_iota(jnp.int32, (1, tq), 1).astype(F32)
    base = tuple(stab[c] + slope * pos for c in range(2))

    def exponentials_below(kj, slot, lo, hi, e_max):
        kb = key_block(kj)
        lhs = (jnp.where(lane_k < DH, kb, kpos_ref[0, 0]), jnp.where(lane_k >= DH, kb, kpos_ref[0, 1]))
        shift = bias_shift(kj)
        return tuple(store_exponentials(
            _dot(lhs[c], q_aug[c][:, lo:hi]) - (base[c][:, lo:hi] - shift), slot, c, lo, hi,
            e_max[c]) for c in range(2))

    def accumulate_columns(kj, slot, lo, hi):
        vb = jnp.concatenate([vt_ref[0, kj], ones], axis=0)
        for c in range(2):
            acc_ref[c, :, lo:hi] = acc_ref[c, :, lo:hi] + _dot(vb, p_ref[slot, c, :, lo:hi])

    lp = lam_ref[...]
    lam = (jnp.exp(jnp.sum(lp[0:1, :] * lp[1:2, :], axis=-1, keepdims=True))
           - jnp.exp(jnp.sum(lp[2:3, :] * lp[3:4, :], axis=-1, keepdims=True))
           + LAMBDA_INIT)

    def write_output(lo, hi):
        a1 = acc_ref[0, :, lo:hi]
        a2 = acc_ref[1, :, lo:hi]
        ot = (a1[0:DV, :] * (1.0 / a1[DV:DV + 1, :])
              - (lam * a2[0:DV, :]) * (1.0 / a2[DV:DV + 1, :]))
        o = ot.T
        o = o * lax.rsqrt(jnp.mean(o * o, axis=-1, keepdims=True) + EPS)
        rows = pl.ds(pl.multiple_of(qi * tq + lo, hi - lo), hi - lo)
        o_ref[0, rows, :] = ((o * g_ref[...]) * (1.0 - LAMBDA_INIT)).astype(BF16)

    acc_ref[...] = jnp.zeros_like(acc_ref)
    e0 = jnp.full((1, tq), NEG_INF, F32)
    half = tq // 2

    def diagonal_region(e_max, pending_pv=()):
        d0 = 2 * qi
        left = tuple(e[:, 0:half] for e in e_max)
        right = tuple(e[:, half:tq] for e in e_max)
        for stage in pending_pv[0:1]:
            stage()
        left = exponentials_diagonal(d0, 1, 0, 0, half, left)
        right = exponentials_below(d0, 0, half, tq, right)
        for stage in pending_pv[1:2]:
            stage()
        right = exponentials_diagonal(d0 + 1, 2, 1, half, tq, right)
        accumulate_columns(d0, 0, 0, half)
        write_output(0, half)
        accumulate_columns(d0, 0, half, tq)
        accumulate_columns(d0 + 1, 1, half, tq)
        write_output(half, tq)
        return left, right

    def first_query_block():
        return diagonal_region((e0, e0))

    def later_query_block():
        def trip(t, e_max):
            e_max = exponentials_below(2 * t + 1, 1, 0, tq, e_max)
            accumulate(2 * t, 0)
            e_max = exponentials_below(2 * t + 2, 0, 0, tq, e_max)
            accumulate(2 * t + 1, 1)
            return e_max

        pairs = qi - 1
        e_max = lax.fori_loop(0, pairs // 2, lambda u, e: trip(2 * u + 1, trip(2 * u, e)),
                              exponentials_below(0, 0, 0, tq, (e0, e0)))
        e_max = lax.fori_loop(2 * (pairs // 2), pairs, trip, e_max)
        d0 = 2 * qi
        e_max = exponentials_below(d0 - 1, 1, 0, tq, e_max)
        return diagonal_region(e_max, (lambda: accumulate(d0 - 2, 0), lambda: accumulate(d0 - 1, 1)))

    left, right = lax.cond(qi == 0, first_query_block, later_query_block)
    smallest = functools.reduce(jnp.minimum, [jnp.min(e) for e in left + right])

    @pl.when(smallest < ATTN_MIN_EXPONENT)
    def _():
        def body(kj, m_run):
            kb = key_block(kj)
            bias = bias_tile(kj)
            shift = bias_shift(kj)
            m_new, alpha = [], []
            for c in range(2):
                s = _dot(kb, qmaps[c]) + bias
                s_ref[c] = s
                m_c = jnp.maximum(m_run[c], jnp.max(s, axis=0, keepdims=True) + shift)
                alpha.append(jnp.exp2(m_run[c] - m_c))
                p_ref[0, c] = jnp.exp2(s_ref[c] - (m_c - shift)).astype(BF16)
                m_new.append(m_c)
            accumulate(kj, 0, alpha)
            return tuple(m_new)

        acc_ref[...] = jnp.zeros_like(acc_ref)
        lax.fori_loop(0, last + 1, body, (e0, e0))
        write_output(0, tq)


def _attn_bias_tiles():
    tq, tk = ATTN_QBLOCK, ATTN_KBLOCK
    jj = np.arange(tk, dtype=np.int64)[:, None]
    ii = np.arange(tq, dtype=np.int64)[None, :]
    slopes = np.asarray(ALIBI_SLOPES, np.float64)[:, None, None]
    tiles = [-slopes * (ii - jj)[None].astype(np.float64)]
    for r in range(tq // tk):
        key = r * tk + jj
        allowed = (key // CHUNK) <= (ii // CHUNK)
        tiles.append(np.where(allowed[None], -slopes * (np.abs(ii - key)[None] + r * tk), NEG_INF))
    return (LOG2E * np.stack(tiles, axis=1)).astype(np.float32)


def _attn_key_position_columns():
    tk = ATTN_KBLOCK
    rest = LOG2E * np.asarray(ALIBI_SLOPES, np.float64)[:, None] * np.arange(tk, dtype=np.float64)[None, :]
    out = np.zeros((N_HEADS, 2, tk, 2 * DH), np.float32)
    for n in range(ATTN_POS_COLS):
        part = rest.astype(ml_dtypes.bfloat16).astype(np.float64)
        out[:, 0, :, DH + n] = part
        out[:, 1, :, n] = part
        rest = rest - part
    return out.astype(ml_dtypes.bfloat16)


def _attn_call(qt, k3, vt, k_sq, lam_params, subln_g, *, batch, seq):
    tq, tk = ATTN_QBLOCK, ATTN_KBLOCK
    nkb = seq // tk
    d_attn = N_HEADS * DV
    bias = jnp.asarray(_attn_bias_tiles())
    kpos = jnp.asarray(_attn_key_position_columns())
    slopes = jnp.asarray([LOG2E * a for a in ALIBI_SLOPES], F32)
    return pl.pallas_call(
        _attn_kernel,
        grid=(batch, N_HEADS),
        in_specs=[
            pl.BlockSpec(memory_space=pltpu.SMEM),
            pl.BlockSpec((1, seq // tq, DV, tq), lambda b, h: (b, 0, h, 0)),
            pl.BlockSpec((1, seq, DV), lambda b, h: (b, 0, h)),
            pl.BlockSpec((1, nkb, DV, tk), lambda b, h: (b, 0, h, 0)),
            pl.BlockSpec((1,) + k_sq.shape[1:], lambda b, h: (b, 0, 0, 0)),
            pl.BlockSpec((1,) + bias.shape[1:], lambda b, h: (h, 0, 0, 0)),
            pl.BlockSpec((1,) + kpos.shape[1:], lambda b, h: (h, 0, 0, 0)),
            pl.BlockSpec(lam_params.shape, lambda b, h: (0, 0)),
            pl.BlockSpec(subln_g.shape, lambda b, h: (0, 0)),
        ],
        out_specs=pl.BlockSpec((1, seq, DV), lambda b, h: (b, 0, h)),
        out_shape=jax.ShapeDtypeStruct((batch, seq, d_attn), BF16),
        scratch_shapes=[pltpu.VMEM((2, DV + ATTN_ONES_ROWS, tq), F32),
                        pltpu.VMEM((2, tk, tq), F32),
                        pltpu.VMEM((2, 2, tk, tq), BF16)],
        compiler_params=pltpu.CompilerParams(
            dimension_semantics=("arbitrary", "arbitrary"),
            vmem_limit_bytes=V7X_VMEM_LIMIT),
        name="diff_attn",
    )(slopes, qt, k3, vt, k_sq, bias, kpos, lam_params, subln_g)


def _post_kernel(x_ref, o_ref, ma_ref, sgb_ref, mod_ref, gf_ref, gfin_ref,
                 wob_ref, wo_ref, wg_ref, wu_ref, wd_ref, y_ref):
    mod = mod_ref[0]
    tm = x_ref.shape[0]
    groups = [slice(g * tm // POST_ROW_GROUPS, (g + 1) * tm // POST_ROW_GROUPS)
              for g in range(POST_ROW_GROUPS)]

    def mixer_residual(rows):
        y_b = _dot(o_ref[rows, :], wob_ref[...])
        merged = ma_ref[rows, :].astype(F32) + sgb_ref[rows, :].astype(F32) * y_b
        return x_ref[rows, :] + mod[2:3, :] * _dot(merged.astype(BF16), wo_ref[...])

    def ffn_input(x1):
        xn = x1 * lax.rsqrt(jnp.mean(x1 * x1, axis=-1, keepdims=True) + EPS)
        return ((xn * gf_ref[...]) * (1.0 + mod[4:5, :]) + mod[3:4, :]).astype(BF16)

    def ffn_hidden(h2):
        gate = _dot(h2, wg_ref[...])
        up = _dot(h2, wu_ref[...])
        return ((gate * _sigmoid(gate)) * up).astype(BF16)

    def finish(rows, x1, act):
        x2 = x1 + mod[5:6, :] * _dot(act, wd_ref[...])
        y_ref[rows, :] = (x2 * lax.rsqrt(jnp.mean(x2 * x2, axis=-1, keepdims=True) + EPS)) * gfin_ref[...]

    x1 = [mixer_residual(rows) for rows in groups]
    h2 = [ffn_input(v) for v in x1]
    act = [ffn_hidden(v) for v in h2]
    for rows, a, b in zip(groups, x1, act):
        finish(rows, a, b)


def _post_call(x2, o2, ma, sgb, mod3, g_ffn, g_final, wob, wo, wg, wu, wd, *, seq):
    t, d = x2.shape
    tm = POST_BLOCK_M
    tps = seq // tm
    full = lambda a: pl.BlockSpec(a.shape, lambda i: (0,) * a.ndim)
    row = lambda a: pl.BlockSpec((tm, a.shape[1]), lambda i: (i, 0))
    return pl.pallas_call(
        _post_kernel,
        grid=(t // tm,),
        in_specs=[
            row(x2), row(o2), row(ma), row(sgb),
            pl.BlockSpec((1,) + mod3.shape[1:], lambda i: (i // tps, 0, 0)),
            full(g_ffn), full(g_final), full(wob), full(wo), full(wg), full(wu), full(wd),
        ],
        out_specs=pl.BlockSpec((tm, d), lambda i: (i, 0)),
        out_shape=jax.ShapeDtypeStruct((t, d), F32),
        compiler_params=pltpu.CompilerParams(
            dimension_semantics=("arbitrary",), vmem_limit_bytes=V7X_VMEM_LIMIT),
        name="mixer_out_ffn",
    )(x2, o2, ma, sgb, mod3, g_ffn, g_final, wob, wo, wg, wu, wd)


def kernel(x, c, w_ada, b_ada, g_mix, w_in, conv_w, w_out_a, lambda_q1, lambda_k1, lambda_q2,
           lambda_k2, subln_g, w_out_b, w_out, g_ffn, w_gate, w_up, w_down, g_final):
    batch, seq, d = x.shape
    depth = w_ada.shape[0]
    assert depth == 1 and seq % PRE_BLOCK_M == 0 and seq % POST_BLOCK_M == 0
    assert PRE_BLOCK_M % ATTN_KBLOCK == 0 and ATTN_KBLOCK % CHUNK == 0
    assert ATTN_KBLOCK % (PRE_BLOCK_M // PRE_ROW_GROUPS) == 0 and PRE_BLOCK_M % (8 * PRE_ROW_GROUPS) == 0
    assert POST_BLOCK_M % (8 * POST_ROW_GROUPS) == 0
    assert ATTN_QBLOCK == 2 * ATTN_KBLOCK and seq % ATTN_QBLOCK == 0
    assert PRE_BLOCK_M % ATTN_QBLOCK == 0 and ATTN_QBLOCK % (PRE_BLOCK_M // PRE_ROW_GROUPS) == 0
    l = 0
    x2 = x.reshape(batch * seq, d)

    c_pad = jnp.pad(c, ((0, 8 - batch), (0, 0)))
    mod = _ada_call(c_pad, w_ada[l], b_ada[l].reshape(1, -1))
    mod3 = mod.reshape(-1, 6, d)

    ma, sgb, qt, k2, vt, k_sq = _pre_call(
        x2, mod3, g_mix[l].reshape(1, d), w_in[l].astype(BF16), conv_w[l],
        w_out_a[l].astype(BF16), batch=batch, seq=seq)

    lam_params = jnp.stack([lambda_q1[l], lambda_k1[l], lambda_q2[l], lambda_k2[l]], axis=0)
    o = _attn_call(qt, k2.reshape(batch, seq, -1), vt, k_sq.reshape(batch, -1, 8, 128),
                   lam_params, subln_g[l].reshape(1, DV), batch=batch, seq=seq)

    y = _post_call(
        x2, o.reshape(batch * seq, -1), ma, sgb, mod3, g_ffn[l].reshape(1, d),
        g_final.reshape(1, d), w_out_b[l].astype(BF16), w_out[l].astype(BF16),
        w_gate[l].astype(BF16), w_up[l].astype(BF16), w_down[l].astype(BF16), seq=seq)
    return y.reshape(batch, seq, d)
```

```python
import functools
import math

import jax
import jax.numpy as jnp
import ml_dtypes
import numpy as np
from jax import lax
from jax.experimental import pallas as pl
from jax.experimental.pallas import tpu as pltpu

CHUNK = 64
CONV_WIDTH = 3
N_HEADS = 4
DH = 64
DV = 2 * DH
EPS = 1e-6
NEG_INF = -1e30
LAMBDA_INIT = 0.8 - 0.6 * math.exp(-0.3 * 1)
LOG2E = math.log2(math.e)
ALIBI_SLOPES = tuple(2.0 ** (-8.0 * (i + 1) / N_HEADS) for i in range(N_HEADS))

ADA_BLOCK_N = 1536
PRE_BLOCK_M = 1024
PRE_ROW_GROUPS = 4
POST_BLOCK_M = 512
POST_ROW_GROUPS = 2
ATTN_KBLOCK = 512
ATTN_QBLOCK = 1024
ATTN_ONES_ROWS = 16
ATTN_POS_COLS = 3
ATTN_BOUND_SLACK = 1.01
ATTN_BOUND_MARGIN = 0.01
ATTN_MIN_EXPONENT = -100.0
V7X_VMEM_LIMIT = 56 * 1024 * 1024

BF16 = jnp.bfloat16
F32 = jnp.float32


def _dot(a, b):
    return jnp.dot(a, b, preferred_element_type=F32)


def _sigmoid(x):
    return 1.0 / (1.0 + jnp.exp(-x))


def _ada_kernel(c_ref, w_ref, b_ref, o_ref):
    c = c_ref[...]
    sc = (c * _sigmoid(c)).astype(BF16)
    o_ref[...] = _dot(sc, w_ref[...].astype(BF16)) + b_ref[...]


def _ada_call(c_pad, w_ada, b_ada):
    rows, d = c_pad.shape
    n = w_ada.shape[1]
    return pl.pallas_call(
        _ada_kernel,
        grid=(n // ADA_BLOCK_N,),
        in_specs=[
            pl.BlockSpec((rows, d), lambda j: (0, 0)),
            pl.BlockSpec((d, ADA_BLOCK_N), lambda j: (0, j)),
            pl.BlockSpec((1, ADA_BLOCK_N), lambda j: (0, j)),
        ],
        out_specs=pl.BlockSpec((rows, ADA_BLOCK_N), lambda j: (0, j)),
        out_shape=jax.ShapeDtypeStruct((rows, n), F32),
        compiler_params=pltpu.CompilerParams(
            dimension_semantics=("arbitrary",), vmem_limit_bytes=V7X_VMEM_LIMIT),
        name="ada_mod",
    )(c_pad, w_ada, b_ada)


def _pre_kernel(x_ref, mod_ref, g_ref, win_ref, cw_ref, woa_ref,
                ma_ref, sgb_ref, qt_ref, k_ref, vt_ref, ksq_ref, carry_ref,
                *, tiles_per_seq, d_conv, d_attn, d_model):
    i = pl.program_id(0)
    tm = x_ref.shape[0]
    mod = mod_ref[0]

    @pl.when(i % tiles_per_seq == 0)
    def _():
        carry_ref[...] = jnp.zeros_like(carry_ref)

    rg = tm // PRE_ROW_GROUPS
    groups = [slice(g * rg, (g + 1) * rg) for g in range(PRE_ROW_GROUPS)]

    a0 = 3 * d_conv
    g0 = a0 + 3 * d_attn

    def normed(rows):
        x = x_ref[rows, :]
        xn = x * lax.rsqrt(jnp.mean(x * x, axis=-1, keepdims=True) + EPS)
        return ((xn * g_ref[...]) * (1.0 + mod[1:2, :]) + mod[0:1, :]).astype(BF16)

    hb = [normed(rows) for rows in groups]

    pc = [_dot(h, win_ref[:, 0:a0]) for h in hb]
    cw = cw_ref[...]
    before = carry_ref[...]
    y_a = []
    for p in pc:
        u = p[:, 0:d_conv]
        gb = p[:, d_conv:2 * d_conv]
        gc = p[:, 2 * d_conv:3 * d_conv]
        cu = gc * u
        ext = jnp.concatenate([before, cu], axis=0)
        prev1 = pltpu.roll(ext, 1, axis=0)[8:, :]
        prev2 = pltpu.roll(ext, 2, axis=0)[8:, :]
        before = cu[rg - 8:, :]
        z = cw[0:1, :] * prev2 + cw[1:2, :] * prev1 + cw[2:3, :] * cu
        y_a.append(_dot((gb * z).astype(BF16), woa_ref[...]))
    carry_ref[...] = before

    for rows, h, ya in zip(groups, hb, y_a):
        pg = _dot(h, win_ref[:, g0:g0 + 2 * d_model])
        ma_ref[rows, :] = (_sigmoid(pg[:, 0:d_model]) * ya).astype(BF16)
        sgb_ref[rows, :] = _sigmoid(pg[:, d_model:2 * d_model]).astype(BF16)

    lane = lax.broadcasted_iota(jnp.int32, ksq_ref.shape[1:], 1)
    tile = jnp.zeros(ksq_ref.shape[1:], F32)
    for g, (rows, h) in enumerate(zip(groups, hb)):
        pa = _dot(h, win_ref[:, a0:g0])
        q = pa[:, 0:d_attn] * (LOG2E / math.sqrt(DH))
        blk, off = divmod(g * rg, ATTN_QBLOCK)
        qt_ref[0, blk, :, off:off + rg] = q.T.astype(BF16)
        kb = pa[:, d_attn:2 * d_attn].astype(BF16)
        k_ref[rows, :] = kb
        kf = kb.astype(F32)
        sq = kf * kf
        for hd in range(N_HEADS):
            norm_sq = jnp.sum(sq[:, hd * DV:(hd + 1) * DV], axis=-1, keepdims=True)
            tile = jnp.where(lane == hd,
                             jnp.maximum(tile, jnp.max(norm_sq, axis=0, keepdims=True)), tile)
        v = pa[:, 2 * d_attn:3 * d_attn]
        blk, off = divmod(g * rg, ATTN_KBLOCK)
        vt_ref[0, blk, :, off:off + rg] = v.T.astype(BF16)
    ksq_ref[0] = tile


def _pre_call(x2, mod3, g_mix, w_in_b, conv_w, w_out_a_b, *, batch, seq):
    t, d = x2.shape
    d_conv = conv_w.shape[1]
    d_attn = N_HEADS * DV
    tm = PRE_BLOCK_M
    tps = seq // tm
    nkb = seq // ATTN_KBLOCK
    kern = functools.partial(_pre_kernel, tiles_per_seq=tps, d_conv=d_conv,
                             d_attn=d_attn, d_model=d)
    full = lambda a: pl.BlockSpec(a.shape, lambda i: (0,) * a.ndim)
    return pl.pallas_call(
        kern,
        grid=(t // tm,),
        in_specs=[
            pl.BlockSpec((tm, d), lambda i: (i, 0)),
            pl.BlockSpec((1,) + mod3.shape[1:], lambda i: (i // tps, 0, 0)),
            full(g_mix), full(w_in_b), full(conv_w), full(w_out_a_b),
        ],
        out_specs=[
            pl.BlockSpec((tm, d), lambda i: (i, 0)),
            pl.BlockSpec((tm, d), lambda i: (i, 0)),
            pl.BlockSpec((1, tm // ATTN_QBLOCK, d_attn, ATTN_QBLOCK),
                         lambda i: (i // tps, i % tps, 0, 0)),
            pl.BlockSpec((tm, d_attn), lambda i: (i, 0)),
            pl.BlockSpec((1, tm // ATTN_KBLOCK, d_attn, ATTN_KBLOCK),
                         lambda i: (i // tps, i % tps, 0, 0)),
            pl.BlockSpec((1, 8, 128), lambda i: (i, 0, 0)),
        ],
        out_shape=[
            jax.ShapeDtypeStruct((t, d), BF16),
            jax.ShapeDtypeStruct((t, d), BF16),
            jax.ShapeDtypeStruct((batch, seq // ATTN_QBLOCK, d_attn, ATTN_QBLOCK), BF16),
            jax.ShapeDtypeStruct((t, d_attn), BF16),
            jax.ShapeDtypeStruct((batch, nkb, d_attn, ATTN_KBLOCK), BF16),
            jax.ShapeDtypeStruct((t // tm, 8, 128), F32),
        ],
        scratch_shapes=[pltpu.VMEM((8, d_conv), F32)],
        compiler_params=pltpu.CompilerParams(
            dimension_semantics=("arbitrary",), vmem_limit_bytes=V7X_VMEM_LIMIT),
        name="mixer_in",
    )(x2, mod3, g_mix, w_in_b, conv_w, w_out_a_b)


def _attn_kernel(slope_ref, qt_ref, *refs):
    h = pl.program_id(1)

    def query_block(qi, carry):
        _attn_query_block(qi, h, slope_ref, qt_ref, *refs)
        return carry

    lax.fori_loop(0, qt_ref.shape[1], query_block, 0)


def _attn_query_block(qi, h, slope_ref, qt_ref, k_ref, vt_ref, ksq_ref, bias_ref, kpos_ref, lam_ref,
                      g_ref, o_ref, acc_ref, s_ref, p_ref):
    tq, tk = ATTN_QBLOCK, ATTN_KBLOCK
    ratio = tq // tk
    slope = slope_ref[h]

    qt = qt_ref[0, qi]
    row = lax.broadcasted_iota(jnp.int32, qt.shape, 0)
    zero = jnp.zeros_like(qt)
    qmaps = (jnp.where(row < DH, qt, zero), jnp.where(row >= DH, qt, zero))
    ones = jnp.ones((ATTN_ONES_ROWS, tk), BF16)

    last = ratio * qi + ratio - 1

    def key_block(kj):
        return k_ref[0, pl.ds(pl.multiple_of(kj * tk, tk), tk), :]

    def bias_tile(kj):
        return bias_ref[0, jnp.clip(kj - ratio * qi + 1, 0, ratio)]

    def bias_shift(kj):
        return -slope * jnp.asarray(qi * tq - kj * tk, F32)

    def accumulate(kj, slot, alphas=None):
        vb = jnp.concatenate([vt_ref[0, kj], ones], axis=0)
        for c in range(2):
            old = acc_ref[c] if alphas is None else alphas[c] * acc_ref[c]
            acc_ref[c] = old + _dot(vb, p_ref[slot, c])

    k_sq_all = jnp.max(ksq_ref[0], axis=0)
    lane = lax.broadcasted_iota(jnp.int32, k_sq_all.shape, 1)
    k_sq = jnp.max(jnp.where(lane == h, k_sq_all, 0.0), axis=-1, keepdims=True)[0:1]
    qf = qt.astype(F32)
    q_sq = qf * qf
    stab = [jnp.sqrt(jnp.sum(q_sq[c * DH:(c + 1) * DH, :], axis=0, keepdims=True) * k_sq)
            * ATTN_BOUND_SLACK + ATTN_BOUND_MARGIN for c in range(2)]

    def store_exponentials(e, slot, c, lo, hi, e_max):
        p_ref[slot, c, :, lo:hi] = jnp.exp2(e).astype(BF16)
        return jnp.maximum(e_max, jnp.max(e, axis=0, keepdims=True))

    def exponentials_diagonal(kj, tile, slot, lo, hi, e_max):
        kb = key_block(kj)
        bias = bias_ref[0, tile, :, lo:hi]
        shift = bias_shift(kj)
        return tuple(store_exponentials(
            _dot(kb, qmaps[c][:, lo:hi]) + bias + (shift - stab[c][:, lo:hi]), slot, c, lo, hi,
            e_max[c]) for c in range(2))

    lane_k = lax.broadcasted_iota(jnp.int32, (tk, 2 * DH), 1)
    ones_rows = (((row >= DH) & (row < DH + ATTN_POS_COLS)).astype(F32).astype(BF16),
                 (row < ATTN_POS_COLS).astype(F32).astype(BF16))
    q_aug = tuple(qmaps[c] + ones_rows[c] for c in range(2))
    pos = lax.broadcasted_iota(jnp.int32, (1, tq), 1).astype(F32)
    base = tuple(stab[c] + slope * pos for c in range(2))

    def exponentials_below(kj, slot, lo, hi, e_max):
        kb = key_block(kj)
        lhs = (jnp.where(lane_k < DH, kb, kpos_ref[0, 0]), jnp.where(lane_k >= DH, kb, kpos_ref[0, 1]))
        shift = bias_shift(kj)
        return tuple(store_exponentials(
            _dot(lhs[c], q_aug[c][:, lo:hi]) - (base[c][:, lo:hi] - shift), slot, c, lo, hi,
            e_max[c]) for c in range(2))

    def accumulate_columns(kj, slot, lo, hi):
        vb = jnp.concatenate([vt_ref[0, kj], ones], axis=0)
        for c in range(2):
            acc_ref[c, :, lo:hi] = acc_ref[c, :, lo:hi] + _dot(vb, p_ref[slot, c, :, lo:hi])

    lp = lam_ref[...]
    lam = (jnp.exp(jnp.sum(lp[0:1, :] * lp[1:2, :], axis=-1, keepdims=True))
           - jnp.exp(jnp.sum(lp[2:3, :] * lp[3:4, :], axis=-1, keepdims=True))
           + LAMBDA_INIT)

    def write_output(lo, hi):
        a1 = acc_ref[0, :, lo:hi]
        a2 = acc_ref[1, :, lo:hi]
        ot = (a1[0:DV, :] * (1.0 / a1[DV:DV + 1, :])
              - (lam * a2[0:DV, :]) * (1.0 / a2[DV:DV + 1, :]))
        o = ot.T
        o = o * lax.rsqrt(jnp.mean(o * o, axis=-1, keepdims=True) + EPS)
        rows = pl.ds(pl.multiple_of(qi * tq + lo, hi - lo), hi - lo)
        o_ref[0, rows, :] = ((o * g_ref[...]) * (1.0 - LAMBDA_INIT)).astype(BF16)

    acc_ref[...] = jnp.zeros_like(acc_ref)
    e0 = jnp.full((1, tq), NEG_INF, F32)
    half = tq // 2

    def diagonal_region(e_max, pending_pv=()):
        d0 = 2 * qi
        left = tuple(e[:, 0:half] for e in e_max)
        right = tuple(e[:, half:tq] for e in e_max)
        for stage in pending_pv[0:1]:
            stage()
        left = exponentials_diagonal(d0, 1, 0, 0, half, left)
        right = exponentials_below(d0, 0, half, tq, right)
        for stage in pending_pv[1:2]:
            stage()
        right = exponentials_diagonal(d0 + 1, 2, 1, half, tq, right)
        accumulate_columns(d0, 0, 0, half)
        write_output(0, half)
        accumulate_columns(d0, 0, half, tq)
        accumulate_columns(d0 + 1, 1, half, tq)
        write_output(half, tq)
        return left, right

    def first_query_block():
        return diagonal_region((e0, e0))

    def later_query_block():
        def trip(t, e_max):
            e_max = exponentials_below(2 * t + 1, 1, 0, tq, e_max)
            accumulate(2 * t, 0)
            e_max = exponentials_below(2 * t + 2, 0, 0, tq, e_max)
            accumulate(2 * t + 1, 1)
            return e_max

        pairs = qi - 1
        e_max = lax.fori_loop(
            0, pairs // 4,
            lambda v, e: trip(4 * v + 3, trip(4 * v + 2, trip(4 * v + 1, trip(4 * v, e)))),
            exponentials_below(0, 0, 0, tq, (e0, e0)))
        done = 4 * (pairs // 4)
        e_max = lax.fori_loop(0, (pairs - done) // 2,
                              lambda u, e: trip(done + 2 * u + 1, trip(done + 2 * u, e)), e_max)
        e_max = lax.fori_loop(done + 2 * ((pairs - done) // 2), pairs, trip, e_max)
        d0 = 2 * qi
        e_max = exponentials_below(d0 - 1, 1, 0, tq, e_max)
        return diagonal_region(e_max, (lambda: accumulate(d0 - 2, 0), lambda: accumulate(d0 - 1, 1)))

    left, right = lax.cond(qi == 0, first_query_block, later_query_block)
    smallest = functools.reduce(jnp.minimum, [jnp.min(e) for e in left + right])

    @pl.when(smallest < ATTN_MIN_EXPONENT)
    def _():
        def body(kj, m_run):
            kb = key_block(kj)
            bias = bias_tile(kj)
            shift = bias_shift(kj)
            m_new, alpha = [], []
            for c in range(2):
                s = _dot(kb, qmaps[c]) + bias
                s_ref[c] = s
                m_c = jnp.maximum(m_run[c], jnp.max(s, axis=0, keepdims=True) + shift)
                alpha.append(jnp.exp2(m_run[c] - m_c))
                p_ref[0, c] = jnp.exp2(s_ref[c] - (m_c - shift)).astype(BF16)
                m_new.append(m_c)
            accumulate(kj, 0, alpha)
            return tuple(m_new)

        acc_ref[...] = jnp.zeros_like(acc_ref)
        lax.fori_loop(0, last + 1, body, (e0, e0))
        write_output(0, tq)


def _attn_bias_tiles():
    tq, tk = ATTN_QBLOCK, ATTN_KBLOCK
    jj = np.arange(tk, dtype=np.int64)[:, None]
    ii = np.arange(tq, dtype=np.int64)[None, :]
    slopes = np.asarray(ALIBI_SLOPES, np.float64)[:, None, None]
    tiles = [-slopes * (ii - jj)[None].astype(np.float64)]
    for r in range(tq // tk):
        key = r * tk + jj
        allowed = (key // CHUNK) <= (ii // CHUNK)
        tiles.append(np.where(allowed[None], -slopes * (np.abs(ii - key)[None] + r * tk), NEG_INF))
    return (LOG2E * np.stack(tiles, axis=1)).astype(np.float32)


def _attn_key_position_columns():
    tk = ATTN_KBLOCK
    rest = LOG2E * np.asarray(ALIBI_SLOPES, np.float64)[:, None] * np.arange(tk, dtype=np.float64)[None, :]
    out = np.zeros((N_HEADS, 2, tk, 2 * DH), np.float32)
    for n in range(ATTN_POS_COLS):
        part = rest.astype(ml_dtypes.bfloat16).astype(np.float64)
        out[:, 0, :, DH + n] = part
        out[:, 1, :, n] = part
        rest = rest - part
    return out.astype(ml_dtypes.bfloat16)


def _attn_call(qt, k3, vt, k_sq, lam_params, subln_g, *, batch, seq):
    tq, tk = ATTN_QBLOCK, ATTN_KBLOCK
    nkb = seq // tk
    d_attn = N_HEADS * DV
    bias = jnp.asarray(_attn_bias_tiles())
    kpos = jnp.asarray(_attn_key_position_columns())
    slopes = jnp.asarray([LOG2E * a for a in ALIBI_SLOPES], F32)
    return pl.pallas_call(
        _attn_kernel,
        grid=(batch, N_HEADS),
        in_specs=[
            pl.BlockSpec(memory_space=pltpu.SMEM),
            pl.BlockSpec((1, seq // tq, DV, tq), lambda b, h: (b, 0, h, 0)),
            pl.BlockSpec((1, seq, DV), lambda b, h: (b, 0, h)),
            pl.BlockSpec((1, nkb, DV, tk), lambda b, h: (b, 0, h, 0)),
            pl.BlockSpec((1,) + k_sq.shape[1:], lambda b, h: (b, 0, 0, 0)),
            pl.BlockSpec((1,) + bias.shape[1:], lambda b, h: (h, 0, 0, 0)),
            pl.BlockSpec((1,) + kpos.shape[1:], lambda b, h: (h, 0, 0, 0)),
            pl.BlockSpec(lam_params.shape, lambda b, h: (0, 0)),
            pl.BlockSpec(subln_g.shape, lambda b, h: (0, 0)),
        ],
        out_specs=pl.BlockSpec((1, seq, DV), lambda b, h: (b, 0, h)),
        out_shape=jax.ShapeDtypeStruct((batch, seq, d_attn), BF16),
        scratch_shapes=[pltpu.VMEM((2, DV + ATTN_ONES_ROWS, tq), F32),
                        pltpu.VMEM((2, tk, tq), F32),
                        pltpu.VMEM((2, 2, tk, tq), BF16)],
        compiler_params=pltpu.CompilerParams(
            dimension_semantics=("arbitrary", "arbitrary"),
            vmem_limit_bytes=V7X_VMEM_LIMIT),
        name="diff_attn",
    )(slopes, qt, k3, vt, k_sq, bias, kpos, lam_params, subln_g)


def _post_kernel(x_ref, o_ref, ma_ref, sgb_ref, mod_ref, gf_ref, gfin_ref,
                 wob_ref, wo_ref, wg_ref, wu_ref, wd_ref, y_ref):
    mod = mod_ref[0]
    tm = x_ref.shape[0]
    groups = [slice(g * tm // POST_ROW_GROUPS, (g + 1) * tm // POST_ROW_GROUPS)
              for g in range(POST_ROW_GROUPS)]

    def mixer_residual(rows):
        y_b = _dot(o_ref[rows, :], wob_ref[...])
        merged = ma_ref[rows, :].astype(F32) + sgb_ref[rows, :].astype(F32) * y_b
        return x_ref[rows, :] + mod[2:3, :] * _dot(merged.astype(BF16), wo_ref[...])

    def ffn_input(x1):
        xn = x1 * lax.rsqrt(jnp.mean(x1 * x1, axis=-1, keepdims=True) + EPS)
        return ((xn * gf_ref[...]) * (1.0 + mod[4:5, :]) + mod[3:4, :]).astype(BF16)

    def ffn_hidden(h2):
        gate = _dot(h2, wg_ref[...])
        up = _dot(h2, wu_ref[...])
        return ((gate * _sigmoid(gate)) * up).astype(BF16)

    def finish(rows, x1, act):
        x2 = x1 + mod[5:6, :] * _dot(act, wd_ref[...])
        y_ref[rows, :] = (x2 * lax.rsqrt(jnp.mean(x2 * x2, axis=-1, keepdims=True) + EPS)) * gfin_ref[...]

    x1 = [mixer_residual(rows) for rows in groups]
    h2 = [ffn_input(v) for v in x1]
    act = [ffn_hidden(v) for v in h2]
    for rows, a, b in zip(groups, x1, act):
        finish(rows, a, b)


def _post_call(x2, o2, ma, sgb, mod3, g_ffn, g_final, wob, wo, wg, wu, wd, *, seq):
    t, d = x2.shape
    tm = POST_BLOCK_M
    tps = seq // tm
    full = lambda a: pl.BlockSpec(a.shape, lambda i: (0,) * a.ndim)
    row = lambda a: pl.BlockSpec((tm, a.shape[1]), lambda i: (i, 0))
    return pl.pallas_call(
        _post_kernel,
        grid=(t // tm,),
        in_specs=[
            row(x2), row(o2), row(ma), row(sgb),
            pl.BlockSpec((1,) + mod3.shape[1:], lambda i: (i // tps, 0, 0)),
            full(g_ffn), full(g_final), full(wob), full(wo), full(wg), full(wu), full(wd),
        ],
        out_specs=pl.BlockSpec((tm, d), lambda i: (i, 0)),
        out_shape=jax.ShapeDtypeStruct((t, d), F32),
        compiler_params=pltpu.CompilerParams(
            dimension_semantics=("arbitrary",), vmem_limit_bytes=V7X_VMEM_LIMIT),
        name="mixer_out_ffn",
    )(x2, o2, ma, sgb, mod3, g_ffn, g_final, wob, wo, wg, wu, wd)


def kernel(x, c, w_ada, b_ada, g_mix, w_in, conv_w, w_out_a, lambda_q1, lambda_k1, lambda_q2,
           lambda_k2, subln_g, w_out_b, w_out, g_ffn, w_gate, w_up, w_down, g_final):
    batch, seq, d = x.shape
    depth = w_ada.shape[0]
    assert depth == 1 and seq % PRE_BLOCK_M == 0 and seq % POST_BLOCK_M == 0
    assert PRE_BLOCK_M % ATTN_KBLOCK == 0 and ATTN_KBLOCK % CHUNK == 0
    assert ATTN_KBLOCK % (PRE_BLOCK_M // PRE_ROW_GROUPS) == 0 and PRE_BLOCK_M % (8 * PRE_ROW_GROUPS) == 0
    assert POST_BLOCK_M % (8 * POST_ROW_GROUPS) == 0
    assert ATTN_QBLOCK == 2 * ATTN_KBLOCK and seq % ATTN_QBLOCK == 0
    assert PRE_BLOCK_M % ATTN_QBLOCK == 0 and ATTN_QBLOCK % (PRE_BLOCK_M // PRE_ROW_GROUPS) == 0
    l = 0
    x2 = x.reshape(batch * seq, d)

    c_pad = jnp.pad(c, ((0, 8 - batch), (0, 0)))
    mod = _ada_call(c_pad, w_ada[l], b_ada[l].reshape(1, -1))
    mod3 = mod.reshape(-1, 6, d)

    ma, sgb, qt, k2, vt, k_sq = _pre_call(
        x2, mod3, g_mix[l].reshape(1, d), w_in[l].astype(BF16), conv_w[l],
        w_out_a[l].astype(BF16), batch=batch, seq=seq)

    lam_params = jnp.stack([lambda_q1[l], lambda_k1[l], lambda_q2[l], lambda_k2[l]], axis=0)
    o = _attn_call(qt, k2.reshape(batch, seq, -1), vt, k_sq.reshape(batch, -1, 8, 128),
                   lam_params, subln_g[l].reshape(1, DV), batch=batch, seq=seq)

    y = _post_call(
        x2, o.reshape(batch * seq, -1), ma, sgb, mod3, g_ffn[l].reshape(1, d),
        g_final.reshape(1, d), w_out_b[l].astype(BF16), w_out[l].astype(BF16),
        w_gate[l].astype(BF16), w_up[l].astype(BF16), w_down[l].astype(BF16), seq=seq)
    return y.reshape(batch, seq, d)
```

```python
import functools
import math

import jax
import jax.numpy as jnp
import ml_dtypes
import numpy as np
from jax import lax
from jax.experimental import pallas as pl
from jax.experimental.pallas import tpu as pltpu

CHUNK = 64
CONV_WIDTH = 3
N_HEADS = 4
DH = 64
DV = 2 * DH
EPS = 1e-6
NEG_INF = -1e30
LAMBDA_INIT = 0.8 - 0.6 * math.exp(-0.3 * 1)
LOG2E = math.log2(math.e)
ALIBI_SLOPES = tuple(2.0 ** (-8.0 * (i + 1) / N_HEADS) for i in range(N_HEADS))

ADA_BLOCK_N = 1536
PRE_BLOCK_M = 1024
PRE_ROW_GROUPS = 4
POST_BLOCK_M = 512
POST_ROW_GROUPS = 2
ATTN_KBLOCK = 512
ATTN_QBLOCK = 1024
ATTN_ONES_ROWS = 16
ATTN_POS_COLS = 3
ATTN_BOUND_SLACK = 1.01
ATTN_BOUND_MARGIN = 0.01
ATTN_MIN_EXPONENT = -100.0
V7X_VMEM_LIMIT = 56 * 1024 * 1024

BF16 = jnp.bfloat16
F32 = jnp.float32


def _dot(a, b):
    return jnp.dot(a, b, preferred_element_type=F32)


def _sigmoid(x):
    return 1.0 / (1.0 + jnp.exp(-x))


def _ada_kernel(c_ref, w_ref, b_ref, o_ref):
    c = c_ref[...]
    sc = (c * _sigmoid(c)).astype(BF16)
    o_ref[...] = _dot(sc, w_ref[...].astype(BF16)) + b_ref[...]


def _ada_call(c_pad, w_ada, b_ada):
    rows, d = c_pad.shape
    n = w_ada.shape[1]
    return pl.pallas_call(
        _ada_kernel,
        grid=(n // ADA_BLOCK_N,),
        in_specs=[
            pl.BlockSpec((rows, d), lambda j: (0, 0)),
            pl.BlockSpec((d, ADA_BLOCK_N), lambda j: (0, j)),
            pl.BlockSpec((1, ADA_BLOCK_N), lambda j: (0, j)),
        ],
        out_specs=pl.BlockSpec((rows, ADA_BLOCK_N), lambda j: (0, j)),
        out_shape=jax.ShapeDtypeStruct((rows, n), F32),
        compiler_params=pltpu.CompilerParams(
            dimension_semantics=("arbitrary",), vmem_limit_bytes=V7X_VMEM_LIMIT),
        name="ada_mod",
    )(c_pad, w_ada, b_ada)


def _pre_kernel(x_ref, mod_ref, g_ref, win_ref, cw_ref, woa_ref,
                ma_ref, sgb_ref, qt_ref, k_ref, vt_ref, ksq_ref, carry_ref,
                *, tiles_per_seq, d_conv, d_attn, d_model):
    i = pl.program_id(0)
    tm = x_ref.shape[0]
    mod = mod_ref[0]

    @pl.when(i % tiles_per_seq == 0)
    def _():
        carry_ref[...] = jnp.zeros_like(carry_ref)

    rg = tm // PRE_ROW_GROUPS
    groups = [slice(g * rg, (g + 1) * rg) for g in range(PRE_ROW_GROUPS)]

    a0 = 3 * d_conv
    g0 = a0 + 3 * d_attn

    def normed(rows):
        x = x_ref[rows, :]
        xn = x * lax.rsqrt(jnp.mean(x * x, axis=-1, keepdims=True) + EPS)
        return ((xn * g_ref[...]) * (1.0 + mod[1:2, :]) + mod[0:1, :]).astype(BF16)

    hb = [normed(rows) for rows in groups]

    pc = [_dot(h, win_ref[:, 0:a0]) for h in hb]
    cw = cw_ref[...]
    before = carry_ref[...]
    y_a = []
    for p in pc:
        u = p[:, 0:d_conv]
        gb = p[:, d_conv:2 * d_conv]
        gc = p[:, 2 * d_conv:3 * d_conv]
        cu = gc * u
        ext = jnp.concatenate([before, cu], axis=0)
        prev1 = pltpu.roll(ext, 1, axis=0)[8:, :]
        prev2 = pltpu.roll(ext, 2, axis=0)[8:, :]
        before = cu[rg - 8:, :]
        z = cw[0:1, :] * prev2 + cw[1:2, :] * prev1 + cw[2:3, :] * cu
        y_a.append(_dot((gb * z).astype(BF16), woa_ref[...]))
    carry_ref[...] = before

    for rows, h, ya in zip(groups, hb, y_a):
        pg = _dot(h, win_ref[:, g0:g0 + 2 * d_model])
        ma_ref[rows, :] = (_sigmoid(pg[:, 0:d_model]) * ya).astype(BF16)
        sgb_ref[rows, :] = _sigmoid(pg[:, d_model:2 * d_model]).astype(BF16)

    lane = lax.broadcasted_iota(jnp.int32, ksq_ref.shape[1:], 1)
    tile = jnp.zeros(ksq_ref.shape[1:], F32)
    for g, (rows, h) in enumerate(zip(groups, hb)):
        pa = _dot(h, win_ref[:, a0:g0])
        q = pa[:, 0:d_attn] * (LOG2E / math.sqrt(DH))
        blk, off = divmod(g * rg, ATTN_QBLOCK)
        qt_ref[0, blk, :, off:off + rg] = q.T.astype(BF16)
        kb = pa[:, d_attn:2 * d_attn].astype(BF16)
        k_ref[rows, :] = kb
        kf = kb.astype(F32)
        sq = kf * kf
        for hd in range(N_HEADS):
            norm_sq = jnp.sum(sq[:, hd * DV:(hd + 1) * DV], axis=-1, keepdims=True)
            tile = jnp.where(lane == hd,
                             jnp.maximum(tile, jnp.max(norm_sq, axis=0, keepdims=True)), tile)
        v = pa[:, 2 * d_attn:3 * d_attn]
        blk, off = divmod(g * rg, ATTN_KBLOCK)
        vt_ref[0, blk, :, off:off + rg] = v.T.astype(BF16)
    ksq_ref[0] = tile


def _pre_call(x2, mod3, g_mix, w_in_b, conv_w, w_out_a_b, *, batch, seq):
    t, d = x2.shape
    d_conv = conv_w.shape[1]
    d_attn = N_HEADS * DV
    tm = PRE_BLOCK_M
    tps = seq // tm
    nkb = seq // ATTN_KBLOCK
    kern = functools.partial(_pre_kernel, tiles_per_seq=tps, d_conv=d_conv,
                             d_attn=d_attn, d_model=d)
    full = lambda a: pl.BlockSpec(a.shape, lambda i: (0,) * a.ndim)
    return pl.pallas_call(
        kern,
        grid=(t // tm,),
        in_specs=[
            pl.BlockSpec((tm, d), lambda i: (i, 0)),
            pl.BlockSpec((1,) + mod3.shape[1:], lambda i: (i // tps, 0, 0)),
            full(g_mix), full(w_in_b), full(conv_w), full(w_out_a_b),
        ],
        out_specs=[
            pl.BlockSpec((tm, d), lambda i: (i, 0)),
            pl.BlockSpec((tm, d), lambda i: (i, 0)),
            pl.BlockSpec((1, tm // ATTN_QBLOCK, d_attn, ATTN_QBLOCK),
                         lambda i: (i // tps, i % tps, 0, 0)),
            pl.BlockSpec((tm, d_attn), lambda i: (i, 0)),
            pl.BlockSpec((1, tm // ATTN_KBLOCK, d_attn, ATTN_KBLOCK),
                         lambda i: (i // tps, i % tps, 0, 0)),
            pl.BlockSpec((1, 8, 128), lambda i: (i, 0, 0)),
        ],
        out_shape=[
            jax.ShapeDtypeStruct((t, d), BF16),
            jax.ShapeDtypeStruct((t, d), BF16),
            jax.ShapeDtypeStruct((batch, seq // ATTN_QBLOCK, d_attn, ATTN_QBLOCK), BF16),
            jax.ShapeDtypeStruct((t, d_attn), BF16),
            jax.ShapeDtypeStruct((batch, nkb, d_attn, ATTN_KBLOCK), BF16),
            jax.ShapeDtypeStruct((t // tm, 8, 128), F32),
        ],
        scratch_shapes=[pltpu.VMEM((8, d_conv), F32)],
        compiler_params=pltpu.CompilerParams(
            dimension_semantics=("arbitrary",), vmem_limit_bytes=V7X_VMEM_LIMIT),
        name="mixer_in",
    )(x2, mod3, g_mix, w_in_b, conv_w, w_out_a_b)


def _attn_kernel(slope_ref, qt_ref, *refs):
    h = pl.program_id(1)

    def query_block_pair(m, carry):
        _attn_query_block(2 * m, 0, h, slope_ref, qt_ref, *refs)
        _attn_query_block(2 * m + 1, 1, h, slope_ref, qt_ref, *refs)
        return carry

    n_blocks = qt_ref.shape[1]
    lax.fori_loop(0, n_blocks // 2, query_block_pair, 0)
    lam_ref, g_ref, o_ref, acc2_ref = refs[5:9]
    _attn_write_output(acc2_ref.at[(n_blocks - 1) % 2], lam_ref, g_ref, o_ref, n_blocks - 1,
                       ATTN_QBLOCK // 2, ATTN_QBLOCK)


def _attn_write_output(acc_ref, lam_ref, g_ref, o_ref, q_index, lo, hi):
    lp = lam_ref[...]
    lam = (jnp.exp(jnp.sum(lp[0:1, :] * lp[1:2, :], axis=-1, keepdims=True))
           - jnp.exp(jnp.sum(lp[2:3, :] * lp[3:4, :], axis=-1, keepdims=True))
           + LAMBDA_INIT)
    a1 = acc_ref[0, :, lo:hi]
    a2 = acc_ref[1, :, lo:hi]
    ot = (a1[0:DV, :] * (1.0 / a1[DV:DV + 1, :])
          - (lam * a2[0:DV, :]) * (1.0 / a2[DV:DV + 1, :]))
    o = ot.T
    o = o * lax.rsqrt(jnp.mean(o * o, axis=-1, keepdims=True) + EPS)
    rows = pl.ds(pl.multiple_of(q_index * ATTN_QBLOCK + lo, hi - lo), hi - lo)
    o_ref[0, rows, :] = ((o * g_ref[...]) * (1.0 - LAMBDA_INIT)).astype(BF16)


def _attn_query_block(qi, acc_slot, h, slope_ref, qt_ref, k_ref, vt_ref, ksq_ref, bias_ref, kpos_ref,
                      lam_ref, g_ref, o_ref, acc2_ref, s_ref, p_ref):
    acc_ref = acc2_ref.at[acc_slot]
    tq, tk = ATTN_QBLOCK, ATTN_KBLOCK
    ratio = tq // tk
    slope = slope_ref[h]

    qt = qt_ref[0, qi]
    row = lax.broadcasted_iota(jnp.int32, qt.shape, 0)
    zero = jnp.zeros_like(qt)
    qmaps = (jnp.where(row < DH, qt, zero), jnp.where(row >= DH, qt, zero))
    ones = jnp.ones((ATTN_ONES_ROWS, tk), BF16)

    last = ratio * qi + ratio - 1

    def key_block(kj):
        return k_ref[0, pl.ds(pl.multiple_of(kj * tk, tk), tk), :]

    def bias_tile(kj):
        return bias_ref[0, jnp.clip(kj - ratio * qi + 1, 0, ratio)]

    def bias_shift(kj):
        return -slope * jnp.asarray(qi * tq - kj * tk, F32)

    def accumulate(kj, slot, alphas=None):
        vb = jnp.concatenate([vt_ref[0, kj], ones], axis=0)
        for c in range(2):
            old = acc_ref[c] if alphas is None else alphas[c] * acc_ref[c]
            acc_ref[c] = old + _dot(vb, p_ref[slot, c])

    k_sq_all = jnp.max(ksq_ref[0], axis=0)
    lane = lax.broadcasted_iota(jnp.int32, k_sq_all.shape, 1)
    k_sq = jnp.max(jnp.where(lane == h, k_sq_all, 0.0), axis=-1, keepdims=True)[0:1]
    qf = qt.astype(F32)
    q_sq = qf * qf
    stab = [jnp.sqrt(jnp.sum(q_sq[c * DH:(c + 1) * DH, :], axis=0, keepdims=True) * k_sq)
            * ATTN_BOUND_SLACK + ATTN_BOUND_MARGIN for c in range(2)]

    def store_exponentials(e, slot, c, lo, hi, e_max):
        p_ref[slot, c, :, lo:hi] = jnp.exp2(e).astype(BF16)
        return jnp.maximum(e_max, jnp.max(e, axis=0, keepdims=True))

    def exponentials_diagonal(kj, tile, slot, lo, hi, e_max):
        kb = key_block(kj)
        bias = bias_ref[0, tile, :, lo:hi]
        shift = bias_shift(kj)
        return tuple(store_exponentials(
            _dot(kb, qmaps[c][:, lo:hi]) + bias + (shift - stab[c][:, lo:hi]), slot, c, lo, hi,
            e_max[c]) for c in range(2))

    lane_k = lax.broadcasted_iota(jnp.int32, (tk, 2 * DH), 1)
    ones_rows = (((row >= DH) & (row < DH + ATTN_POS_COLS)).astype(F32).astype(BF16),
                 (row < ATTN_POS_COLS).astype(F32).astype(BF16))
    q_aug = tuple(qmaps[c] + ones_rows[c] for c in range(2))
    pos = lax.broadcasted_iota(jnp.int32, (1, tq), 1).astype(F32)
    base = tuple(stab[c] + slope * pos for c in range(2))

    def exponentials_below(kj, slot, lo, hi, e_max):
        kb = key_block(kj)
        lhs = (jnp.where(lane_k < DH, kb, kpos_ref[0, 0]), jnp.where(lane_k >= DH, kb, kpos_ref[0, 1]))
        shift = bias_shift(kj)
        return tuple(store_exponentials(
            _dot(lhs[c], q_aug[c][:, lo:hi]) - (base[c][:, lo:hi] - shift), slot, c, lo, hi,
            e_max[c]) for c in range(2))

    def accumulate_columns(kj, slot, lo, hi):
        vb = jnp.concatenate([vt_ref[0, kj], ones], axis=0)
        for c in range(2):
            acc_ref[c, :, lo:hi] = acc_ref[c, :, lo:hi] + _dot(vb, p_ref[slot, c, :, lo:hi])

    def write_output(lo, hi):
        _attn_write_output(acc_ref, lam_ref, g_ref, o_ref, qi, lo, hi)

    def write_previous_block_output():
        _attn_write_output(acc2_ref.at[1 - acc_slot], lam_ref, g_ref, o_ref, qi - 1, tq // 2, tq)

    acc_ref[...] = jnp.zeros_like(acc_ref)
    e0 = jnp.full((1, tq), NEG_INF, F32)
    half = tq // 2

    def diagonal_region(e_max, pending_pv=()):
        d0 = 2 * qi
        left = tuple(e[:, 0:half] for e in e_max)
        right = tuple(e[:, half:tq] for e in e_max)
        for stage in pending_pv[0:1]:
            stage()
        left = exponentials_diagonal(d0, 1, 0, 0, half, left)
        right = exponentials_below(d0, 0, half, tq, right)
        for stage in pending_pv[1:2]:
            stage()
        right = exponentials_diagonal(d0 + 1, 2, 1, half, tq, right)
        accumulate_columns(d0, 0, 0, half)
        write_output(0, half)
        accumulate_columns(d0, 0, half, tq)
        accumulate_columns(d0 + 1, 1, half, tq)
        return left, right

    def first_query_block():
        return diagonal_region((e0, e0))

    def later_query_block():
        def trip(t, e_max):
            e_max = exponentials_below(2 * t + 1, 1, 0, tq, e_max)
            accumulate(2 * t, 0)
            e_max = exponentials_below(2 * t + 2, 0, 0, tq, e_max)
            accumulate(2 * t + 1, 1)
            return e_max

        pairs = qi - 1
        write_previous_block_output()
        e_max = lax.fori_loop(0, pairs // 2, lambda u, e: trip(2 * u + 1, trip(2 * u, e)),
                              exponentials_below(0, 0, 0, tq, (e0, e0)))
        e_max = lax.fori_loop(2 * (pairs // 2), pairs, trip, e_max)
        d0 = 2 * qi
        e_max = exponentials_below(d0 - 1, 1, 0, tq, e_max)
        return diagonal_region(e_max, (lambda: accumulate(d0 - 2, 0), lambda: accumulate(d0 - 1, 1)))

    left, right = lax.cond(qi == 0, first_query_block, later_query_block)
    smallest = functools.reduce(jnp.minimum, [jnp.min(e) for e in left + right])

    @pl.when(smallest < ATTN_MIN_EXPONENT)
    def _():
        def body(kj, m_run):
            kb = key_block(kj)
            bias = bias_tile(kj)
            shift = bias_shift(kj)
            m_new, alpha = [], []
            for c in range(2):
                s = _dot(kb, qmaps[c]) + bias
                s_ref[c] = s
                m_c = jnp.maximum(m_run[c], jnp.max(s, axis=0, keepdims=True) + shift)
                alpha.append(jnp.exp2(m_run[c] - m_c))
                p_ref[0, c] = jnp.exp2(s_ref[c] - (m_c - shift)).astype(BF16)
                m_new.append(m_c)
            accumulate(kj, 0, alpha)
            return tuple(m_new)

        acc_ref[...] = jnp.zeros_like(acc_ref)
        lax.fori_loop(0, last + 1, body, (e0, e0))
        write_output(0, tq)


def _attn_bias_tiles():
    tq, tk = ATTN_QBLOCK, ATTN_KBLOCK
    jj = np.arange(tk, dtype=np.int64)[:, None]
    ii = np.arange(tq, dtype=np.int64)[None, :]
    slopes = np.asarray(ALIBI_SLOPES, np.float64)[:, None, None]
    tiles = [-slopes * (ii - jj)[None].astype(np.float64)]
    for r in range(tq // tk):
        key = r * tk + jj
        allowed = (key // CHUNK) <= (ii // CHUNK)
        tiles.append(np.where(allowed[None], -slopes * (np.abs(ii - key)[None] + r * tk), NEG_INF))
    return (LOG2E * np.stack(tiles, axis=1)).astype(np.float32)


def _attn_key_position_columns():
    tk = ATTN_KBLOCK
    rest = LOG2E * np.asarray(ALIBI_SLOPES, np.float64)[:, None] * np.arange(tk, dtype=np.float64)[None, :]
    out = np.zeros((N_HEADS, 2, tk, 2 * DH), np.float32)
    for n in range(ATTN_POS_COLS):
        part = rest.astype(ml_dtypes.bfloat16).astype(np.float64)
        out[:, 0, :, DH + n] = part
        out[:, 1, :, n] = part
        rest = rest - part
    return out.astype(ml_dtypes.bfloat16)


def _attn_call(qt, k3, vt, k_sq, lam_params, subln_g, *, batch, seq):
    tq, tk = ATTN_QBLOCK, ATTN_KBLOCK
    nkb = seq // tk
    d_attn = N_HEADS * DV
    bias = jnp.asarray(_attn_bias_tiles())
    kpos = jnp.asarray(_attn_key_position_columns())
    slopes = jnp.asarray([LOG2E * a for a in ALIBI_SLOPES], F32)
    return pl.pallas_call(
        _attn_kernel,
        grid=(batch, N_HEADS),
        in_specs=[
            pl.BlockSpec(memory_space=pltpu.SMEM),
            pl.BlockSpec((1, seq // tq, DV, tq), lambda b, h: (b, 0, h, 0)),
            pl.BlockSpec((1, seq, DV), lambda b, h: (b, 0, h)),
            pl.BlockSpec((1, nkb, DV, tk), lambda b, h: (b, 0, h, 0)),
            pl.BlockSpec((1,) + k_sq.shape[1:], lambda b, h: (b, 0, 0, 0)),
            pl.BlockSpec((1,) + bias.shape[1:], lambda b, h: (h, 0, 0, 0)),
            pl.BlockSpec((1,) + kpos.shape[1:], lambda b, h: (h, 0, 0, 0)),
            pl.BlockSpec(lam_params.shape, lambda b, h: (0, 0)),
            pl.BlockSpec(subln_g.shape, lambda b, h: (0, 0)),
        ],
        out_specs=pl.BlockSpec((1, seq, DV), lambda b, h: (b, 0, h)),
        out_shape=jax.ShapeDtypeStruct((batch, seq, d_attn), BF16),
        scratch_shapes=[pltpu.VMEM((2, 2, DV + ATTN_ONES_ROWS, tq), F32),
                        pltpu.VMEM((2, tk, tq), F32),
                        pltpu.VMEM((2, 2, tk, tq), BF16)],
        compiler_params=pltpu.CompilerParams(
            dimension_semantics=("arbitrary", "arbitrary"),
            vmem_limit_bytes=V7X_VMEM_LIMIT),
        name="diff_attn",
    )(slopes, qt, k3, vt, k_sq, bias, kpos, lam_params, subln_g)


def _post_kernel(x_ref, o_ref, ma_ref, sgb_ref, mod_ref, gf_ref, gfin_ref,
                 wob_ref, wo_ref, wg_ref, wu_ref, wd_ref, y_ref):
    mod = mod_ref[0]
    tm = x_ref.shape[0]
    groups = [slice(g * tm // POST_ROW_GROUPS, (g + 1) * tm // POST_ROW_GROUPS)
              for g in range(POST_ROW_GROUPS)]

    def mixer_residual(rows):
        y_b = _dot(o_ref[rows, :], wob_ref[...])
        merged = ma_ref[rows, :].astype(F32) + sgb_ref[rows, :].astype(F32) * y_b
        return x_ref[rows, :] + mod[2:3, :] * _dot(merged.astype(BF16), wo_ref[...])

    def ffn_input(x1):
        xn = x1 * lax.rsqrt(jnp.mean(x1 * x1, axis=-1, keepdims=True) + EPS)
        return ((xn * gf_ref[...]) * (1.0 + mod[4:5, :]) + mod[3:4, :]).astype(BF16)

    def ffn_hidden(h2):
        gate = _dot(h2, wg_ref[...])
        up = _dot(h2, wu_ref[...])
        return ((gate * _sigmoid(gate)) * up).astype(BF16)

    def finish(rows, x1, act):
        x2 = x1 + mod[5:6, :] * _dot(act, wd_ref[...])
        y_ref[rows, :] = (x2 * lax.rsqrt(jnp.mean(x2 * x2, axis=-1, keepdims=True) + EPS)) * gfin_ref[...]

    x1 = [mixer_residual(rows) for rows in groups]
    h2 = [ffn_input(v) for v in x1]
    act = [ffn_hidden(v) for v in h2]
    for rows, a, b in zip(groups, x1, act):
        finish(rows, a, b)


def _post_call(x2, o2, ma, sgb, mod3, g_ffn, g_final, wob, wo, wg, wu, wd, *, seq):
    t, d = x2.shape
    tm = POST_BLOCK_M
    tps = seq // tm
    full = lambda a: pl.BlockSpec(a.shape, lambda i: (0,) * a.ndim)
    row = lambda a: pl.BlockSpec((tm, a.shape[1]), lambda i: (i, 0))
    return pl.pallas_call(
        _post_kernel,
        grid=(t // tm,),
        in_specs=[
            row(x2), row(o2), row(ma), row(sgb),
            pl.BlockSpec((1,) + mod3.shape[1:], lambda i: (i // tps, 0, 0)),
            full(g_ffn), full(g_final), full(wob), full(wo), full(wg), full(wu), full(wd),
        ],
        out_specs=pl.BlockSpec((tm, d), lambda i: (i, 0)),
        out_shape=jax.ShapeDtypeStruct((t, d), F32),
        compiler_params=pltpu.CompilerParams(
            dimension_semantics=("arbitrary",), vmem_limit_bytes=V7X_VMEM_LIMIT),
        name="mixer_out_ffn",
    )(x2, o2, ma, sgb, mod3, g_ffn, g_final, wob, wo, wg, wu, wd)


def kernel(x, c, w_ada, b_ada, g_mix, w_in, conv_w, w_out_a, lambda_q1, lambda_k1, lambda_q2,
           lambda_k2, subln_g, w_out_b, w_out, g_ffn, w_gate, w_up, w_down, g_final):
    batch, seq, d = x.shape
    depth = w_ada.shape[0]
    assert depth == 1 and seq % PRE_BLOCK_M == 0 and seq % POST_BLOCK_M == 0
    assert PRE_BLOCK_M % ATTN_KBLOCK == 0 and ATTN_KBLOCK % CHUNK == 0
    assert ATTN_KBLOCK % (PRE_BLOCK_M // PRE_ROW_GROUPS) == 0 and PRE_BLOCK_M % (8 * PRE_ROW_GROUPS) == 0
    assert POST_BLOCK_M % (8 * POST_ROW_GROUPS) == 0
    assert ATTN_QBLOCK == 2 * ATTN_KBLOCK and seq % (2 * ATTN_QBLOCK) == 0
    assert PRE_BLOCK_M % ATTN_QBLOCK == 0 and ATTN_QBLOCK % (PRE_BLOCK_M // PRE_ROW_GROUPS) == 0
    l = 0
    x2 = x.reshape(batch * seq, d)

    c_pad = jnp.pad(c, ((0, 8 - batch), (0, 0)))
    mod = _ada_call(c_pad, w_ada[l], b_ada[l].reshape(1, -1))
    mod3 = mod.reshape(-1, 6, d)

    ma, sgb, qt, k2, vt, k_sq = _pre_call(
        x2, mod3, g_mix[l].reshape(1, d), w_in[l].astype(BF16), conv_w[l],
        w_out_a[l].astype(BF16), batch=batch, seq=seq)

    lam_params = jnp.stack([lambda_q1[l], lambda_k1[l], lambda_q2[l], lambda_k2[l]], axis=0)
    o = _attn_call(qt, k2.reshape(batch, seq, -1), vt, k_sq.reshape(batch, -1, 8, 128),
                   lam_params, subln_g[l].reshape(1, DV), batch=batch, seq=seq)

    y = _post_call(
        x2, o.reshape(batch * seq, -1), ma, sgb, mod3, g_ffn[l].reshape(1, d),
        g_final.reshape(1, d), w_out_b[l].astype(BF16), w_out[l].astype(BF16),
        w_gate[l].astype(BF16), w_up[l].astype(BF16), w_down[l].astype(BF16), seq=seq)
    return y.reshape(batch, seq, d)
```
